```python
import jax, jax.numpy as jnp
from jax import lax
import numpy as np

D_MODEL = 1024
BATCH = 2
SEQ = 8192
DEPTH = 2

GRID_W = 64

N_FOURIER_GROUPS = 4
FOURIER_GROUP_DIM = 128
D_FOURIER = N_FOURIER_GROUPS * FOURIER_GROUP_DIM

N_Q_HEADS = 8
N_KV_HEADS = 2
HEAD_DIM = 64
D_ATTN = N_Q_HEADS * HEAD_DIM
D_KV = N_KV_HEADS * HEAD_DIM
ROPE_AXIS_DIM = HEAD_DIM // 2
ROPE_THETA = 10000.0
Q_BLOCK = 128

D_GATE = D_MODEL
SPLITS = [D_FOURIER,
          D_FOURIER + D_ATTN,
          D_FOURIER + D_ATTN + D_KV,
          D_FOURIER + D_ATTN + 2 * D_KV,
          D_FOURIER + D_ATTN + 2 * D_KV + D_GATE]
D_IN_PROJ = D_FOURIER + D_ATTN + 2 * D_KV + 2 * D_GATE

PEER_HEADS = 8
PEER_N_KEYS = 128
PEER_N_EXPERTS = PEER_N_KEYS * PEER_N_KEYS
PEER_TOPK = 16
PEER_QUERY_DIM = 256
PEER_HALF = PEER_QUERY_DIM // 2
PEER_TOKEN_BLOCK = 128

DEEPNORM_ALPHA = (2.0 * DEPTH) ** 0.25
DEEPNORM_BETA = (8.0 * DEPTH) ** -0.25

LN_EPS = 1e-5
RMS_EPS = 1e-6

kernel_name = "hybrid_fnet_gqa_peer_deepnorm_encoder"


def layer_norm(x, g, b):
    xf = x.astype(jnp.float32)
    mu = jnp.mean(xf, axis=-1, keepdims=True)
    xc = xf - mu
    var = jnp.mean(xc * xc, axis=-1, keepdims=True)
    return (xc * lax.rsqrt(var + LN_EPS) * g + b).astype(x.dtype)


def rms_norm(x, g):
    xf = x.astype(jnp.float32)
    ms = jnp.mean(xf * xf, axis=-1, keepdims=True)
    return (xf * lax.rsqrt(ms + RMS_EPS) * g).astype(x.dtype)


def axial_rope_tables(seq_len, dtype):
    rows = seq_len // GRID_W
    row_id = jnp.repeat(jnp.arange(rows, dtype=jnp.float32), GRID_W)
    col_id = jnp.tile(jnp.arange(GRID_W, dtype=jnp.float32), rows)
    inv_freq = 1.0 / (ROPE_THETA ** (jnp.arange(0, ROPE_AXIS_DIM, 2, dtype=jnp.float32) / ROPE_AXIS_DIM))
    ang_r = row_id[:, None] * inv_freq[None, :]
    ang_c = col_id[:, None] * inv_freq[None, :]
    ang = jnp.concatenate([ang_r, ang_r, ang_c, ang_c], axis=-1)
    return jnp.cos(ang).astype(dtype), jnp.sin(ang).astype(dtype)


def apply_axial_rope(x, cos, sin):
    xr = x.reshape(*x.shape[:-1], 2, 2, ROPE_AXIS_DIM // 2)
    rot = jnp.concatenate([-xr[..., 1:2, :], xr[..., 0:1, :]], axis=-2).reshape(x.shape)
    return x * cos[None, :, None, :] + rot * sin[None, :, None, :]


def fourier_mix(xf):
    b, s, _ = xf.shape
    xg = xf.reshape(b, s, N_FOURIER_GROUPS, FOURIER_GROUP_DIM).astype(jnp.float32)
    y = jnp.fft.fft2(xg, axes=(1, 3), norm="ortho").real
    return y.reshape(b, s, D_FOURIER).astype(xf.dtype)


def block_attention(q, k, v):
    b, s = q.shape[:2]
    nblk = s // Q_BLOCK
    grp = N_Q_HEADS // N_KV_HEADS
    qb = q.reshape(b, nblk, Q_BLOCK, N_KV_HEADS, grp, HEAD_DIM).transpose(1, 0, 3, 4, 2, 5)
    kt = k.transpose(0, 2, 1, 3)
    vt = v.transpose(0, 2, 1, 3)
    scale = HEAD_DIM ** -0.5

    def one_block(qblk):
        sc = jnp.einsum('bkgqd,bksd->bkgqs', qblk, kt).astype(jnp.float32) * scale
        p = jax.nn.softmax(sc, axis=-1)
        return jnp.einsum('bkgqs,bksd->bkgqd', p.astype(vt.dtype), vt)

    o = lax.map(one_block, qb)
    return o.transpose(1, 0, 4, 2, 3, 5).reshape(b, s, D_ATTN)


def peer_ffn(x, w_query, sub_keys, u_tab, v_tab):
    b, s, d = x.shape
    xt = x.reshape((b * s) // PEER_TOKEN_BLOCK, PEER_TOKEN_BLOCK, d)

    def one_block(xc):
        q = (xc @ w_query).reshape(-1, PEER_HEADS, 2, PEER_HALF)
        s1 = jnp.einsum('thd,hkd->thk', q[:, :, 0], sub_keys[:, 0]).astype(jnp.float32)
        s2 = jnp.einsum('thd,hkd->thk', q[:, :, 1], sub_keys[:, 1]).astype(jnp.float32)
        v1, i1 = lax.top_k(s1, PEER_TOPK)
        v2, i2 = lax.top_k(s2, PEER_TOPK)
        cand = (v1[..., :, None] + v2[..., None, :]).reshape(*v1.shape[:-1], PEER_TOPK * PEER_TOPK)
        cidx = (i1[..., :, None] * PEER_N_KEYS + i2[..., None, :]).reshape(*i1.shape[:-1], PEER_TOPK * PEER_TOPK)
        sv, pos = lax.top_k(cand, PEER_TOPK)
        eidx = jnp.take_along_axis(cidx, pos, axis=-1)
        gate = jax.nn.softmax(sv, axis=-1)
        u = u_tab[eidx]
        act = jax.nn.gelu(jnp.einsum('thkd,td->thk', u, xc), approximate=False)
        w = gate.astype(act.dtype) * act
        vv = v_tab[eidx]
        return jnp.einsum('thk,thkd->td', w, vv)

    return lax.map(one_block, xt).reshape(b, s, d)


def setup_inputs(seed: int = 0) -> dict:
    key = jax.random.key(seed)
    ks = jax.random.split(key, 20)
    nrm = jax.random.normal
    f32 = jnp.float32
    x = nrm(ks[0], (BATCH, SEQ, D_MODEL), f32)
    ln0_g = 1.0 + 0.02 * nrm(ks[1], (D_MODEL,), f32)
    ln0_b = 0.02 * nrm(ks[2], (D_MODEL,), f32)
    w_in = nrm(ks[3], (DEPTH, D_MODEL, D_IN_PROJ), f32) * D_MODEL ** -0.5
    b_gate = 0.01 * nrm(ks[4], (DEPTH, 2, D_MODEL), f32)
    q_norm_g = 1.0 + 0.02 * nrm(ks[5], (DEPTH, HEAD_DIM), f32)
    k_norm_g = 1.0 + 0.02 * nrm(ks[6], (DEPTH, HEAD_DIM), f32)
    w_branch_fourier = nrm(ks[7], (DEPTH, D_FOURIER, D_MODEL), f32) * D_FOURIER ** -0.5
    w_branch_attn = nrm(ks[8], (DEPTH, D_ATTN, D_MODEL), f32) * D_ATTN ** -0.5
    w_out = nrm(ks[9], (DEPTH, D_MODEL, D_MODEL), f32) * (DEEPNORM_BETA * D_MODEL ** -0.5)
    ln1_g = 1.0 + 0.02 * nrm(ks[10], (DEPTH, D_MODEL), f32)
    ln1_b = 0.02 * nrm(ks[11], (DEPTH, D_MODEL), f32)
    peer_w_query = nrm(ks[12], (DEPTH, D_MODEL, PEER_HEADS * PEER_QUERY_DIM), f32) * D_MODEL ** -0.5
    peer_sub_keys = nrm(ks[13], (DEPTH, PEER_HEADS, 2, PEER_N_KEYS, PEER_HALF), f32) * PEER_HALF ** -0.5
    peer_u = nrm(ks[14], (DEPTH, PEER_N_EXPERTS, D_MODEL), f32) * D_MODEL ** -0.5
    peer_v = nrm(ks[15], (DEPTH, PEER_N_EXPERTS, D_MODEL), f32) * (DEEPNORM_BETA * PEER_HEADS ** -0.5)
    ln2_g = 1.0 + 0.02 * nrm(ks[16], (DEPTH, D_MODEL), f32)
    ln2_b = 0.02 * nrm(ks[17], (DEPTH, D_MODEL), f32)
    return {"x": x, "ln0_g": ln0_g, "ln0_b": ln0_b, "w_in": w_in, "b_gate": b_gate,
            "q_norm_g": q_norm_g, "k_norm_g": k_norm_g,
            "w_branch_fourier": w_branch_fourier, "w_branch_attn": w_branch_attn,
            "w_out": w_out, "ln1_g": ln1_g, "ln1_b": ln1_b,
            "peer_w_query": peer_w_query, "peer_sub_keys": peer_sub_keys,
            "peer_u": peer_u, "peer_v": peer_v, "ln2_g": ln2_g, "ln2_b": ln2_b}


def reference(x, ln0_g, ln0_b, w_in, b_gate, q_norm_g, k_norm_g, w_branch_fourier,
              w_branch_attn, w_out, ln1_g, ln1_b, peer_w_query, peer_sub_keys,
              peer_u, peer_v, ln2_g, ln2_b):
    b, s, _ = x.shape
    x = layer_norm(x, ln0_g, ln0_b)
    cos, sin = axial_rope_tables(s, x.dtype)
    for l in range(DEPTH):
        h = x @ w_in[l]
        xf, q, k, v, g_f, g_a = jnp.split(h, SPLITS, axis=-1)
        y_f = fourier_mix(xf) @ w_branch_fourier[l]
        q = apply_axial_rope(rms_norm(q.reshape(b, s, N_Q_HEADS, HEAD_DIM), q_norm_g[l]), cos, sin)
        k = apply_axial_rope(rms_norm(k.reshape(b, s, N_KV_HEADS, HEAD_DIM), k_norm_g[l]), cos, sin)
        v = v.reshape(b, s, N_KV_HEADS, HEAD_DIM)
        y_a = block_attention(q, k, v) @ w_branch_attn[l]
        mixed = jax.nn.sigmoid(g_f + b_gate[l, 0]) * y_f + jax.nn.sigmoid(g_a + b_gate[l, 1]) * y_a
        x = layer_norm(DEEPNORM_ALPHA * x + mixed @ w_out[l], ln1_g[l], ln1_b[l])
        y_p = peer_ffn(x, peer_w_query[l], peer_sub_keys[l], peer_u[l], peer_v[l])
        x = layer_norm(DEEPNORM_ALPHA * x + y_p, ln2_g[l], ln2_b[l])
    return x
```

```python
import functools
import math

import numpy as np
import jax
import jax.numpy as jnp
from jax import lax
from jax.experimental import pallas as pl
from jax.experimental.pallas import tpu as pltpu

F32 = jnp.float32
BF16 = jnp.bfloat16

D_MODEL = 1024
GRID_W = 64
N_FOURIER_GROUPS = 4
FOURIER_GROUP_DIM = 128
D_FOURIER = N_FOURIER_GROUPS * FOURIER_GROUP_DIM
N_Q_HEADS = 8
N_KV_HEADS = 2
Q_PER_KV = N_Q_HEADS // N_KV_HEADS
HEAD_DIM = 64
D_ATTN = N_Q_HEADS * HEAD_DIM
D_KV = N_KV_HEADS * HEAD_DIM
ROPE_AXIS_DIM = HEAD_DIM // 2
ROPE_HALF = ROPE_AXIS_DIM // 2
ROPE_THETA = 10000.0
PEER_HEADS = 8
PEER_N_KEYS = 128
PEER_TOPK = 16
PEER_HALF = 128
LN_EPS = 1e-5
RMS_EPS = 1e-6

_C_XF = 0
_C_Q = D_FOURIER
_C_K = _C_Q + D_ATTN
_C_V = _C_K + D_KV
_C_GF = _C_V + D_KV
_C_GA = _C_GF + D_MODEL
_C_END = _C_GA + D_MODEL

DFT_N2 = 128

VMEM_LIMIT_BYTES = 56 * 1024 * 1024


def _params(*sem):
    return pltpu.CompilerParams(dimension_semantics=sem, vmem_limit_bytes=VMEM_LIMIT_BYTES)


def _dot(a, b):
    return jnp.dot(a, b, preferred_element_type=F32)


def _dot_nt(a, b):
    return lax.dot_general(a, b, (((1,), (1,)), ((), ())), preferred_element_type=F32)


def _layer_norm(z, g, b):
    mu = jnp.mean(z, axis=-1, keepdims=True)
    zc = z - mu
    var = jnp.mean(zc * zc, axis=-1, keepdims=True)
    return zc * lax.rsqrt(var + LN_EPS) * g + b


def _ln_kernel(x_ref, g_ref, b_ref, o_ref):
    o_ref[...] = _layer_norm(x_ref[...], g_ref[...], b_ref[...])


def _ln_call(x, g, b, tm):
    n, d = x.shape
    return pl.pallas_call(
        _ln_kernel,
        grid=(n // tm,),
        in_specs=[pl.BlockSpec((tm, d), lambda i: (i, 0)),
                  pl.BlockSpec((1, d), lambda i: (0, 0)),
                  pl.BlockSpec((1, d), lambda i: (0, 0))],
        out_specs=pl.BlockSpec((tm, d), lambda i: (i, 0)),
        out_shape=jax.ShapeDtypeStruct((n, d), F32),
        compiler_params=_params("parallel"),
        name="ln0",
    )(x, g.reshape(1, d), b.reshape(1, d))


def _rms_rope(t, gain, head_mean, cos, sin_signed):
    sq = t * t
    hi = sq.astype(BF16)
    lo = (sq - hi.astype(F32)).astype(BF16)
    ms = _dot(hi, head_mean) + _dot(lo, head_mean)
    tn = t * lax.rsqrt(ms + RMS_EPS) * gain
    width = t.shape[-1]
    lane = lax.broadcasted_iota(jnp.int32, tn.shape, 1)
    first_half = (lane % ROPE_AXIS_DIM) < ROPE_HALF
    rot = jnp.where(first_half, pltpu.roll(tn, width - ROPE_HALF, 1), pltpu.roll(tn, ROPE_HALF, 1))
    return tn * cos + rot * sin_signed


def _in_proj_kernel(x_ref, w_ref, bg_ref, qg_ref, kg_ref, cq_ref, sq_ref, ck_ref, sk_ref,
                    cdft_ref, hmq_ref, hmk_ref,
                    zr_ref, zi_ref, q_ref, k_ref, v_ref, gf_ref, ga_ref):
    xb = x_ref[...].astype(BF16)

    def seg(a, b):
        return _dot(xb, w_ref[:, a:b])

    xf = seg(_C_XF, _C_Q).astype(BF16)
    for g in range(N_FOURIER_GROUPS):
        lo, hi = g * FOURIER_GROUP_DIM, (g + 1) * FOURIER_GROUP_DIM
        z = _dot(xf[:, lo:hi], cdft_ref[...])
        zr_ref[:, lo:hi] = z[:, :FOURIER_GROUP_DIM]
        zi_ref[:, lo:hi] = z[:, FOURIER_GROUP_DIM:]

    q = _rms_rope(seg(_C_Q, _C_K), qg_ref[...], hmq_ref[...], cq_ref[...], sq_ref[...])
    q_ref[...] = (q * (HEAD_DIM ** -0.5)).astype(BF16)
    k = _rms_rope(seg(_C_K, _C_V), kg_ref[...], hmk_ref[...], ck_ref[...], sk_ref[...]).astype(BF16)
    v = seg(_C_V, _C_GF).astype(BF16)
    for j in range(N_KV_HEADS):
        k_ref[j] = k[:, j * HEAD_DIM:(j + 1) * HEAD_DIM]
        v_ref[j] = v[:, j * HEAD_DIM:(j + 1) * HEAD_DIM]

    gf_ref[...] = jax.nn.sigmoid(seg(_C_GF, _C_GA) + bg_ref[0:1, :])
    ga_ref[...] = jax.nn.sigmoid(seg(_C_GA, _C_END) + bg_ref[1:2, :])


def _in_proj_call(x, w, bg, qg, kg, cq, sq, ck, sk, cdft, hmq, hmk, tm, seq):
    n, d = x.shape
    pos_blocks = seq // tm
    row = lambda i: (i, 0)
    pos = lambda i: (i % pos_blocks, 0)
    const = lambda i: (0, 0)
    full = lambda a: pl.BlockSpec(a.shape, const)
    return pl.pallas_call(
        _in_proj_kernel,
        grid=(n // tm,),
        in_specs=[pl.BlockSpec((tm, d), row), full(w), full(bg), full(qg), full(kg),
                  pl.BlockSpec((tm, D_ATTN), pos), pl.BlockSpec((tm, D_ATTN), pos),
                  pl.BlockSpec((tm, D_KV), pos), pl.BlockSpec((tm, D_KV), pos),
                  full(cdft), full(hmq), full(hmk)],
        out_specs=[pl.BlockSpec((tm, D_FOURIER), row), pl.BlockSpec((tm, D_FOURIER), row),
                   pl.BlockSpec((tm, D_ATTN), row),
                   pl.BlockSpec((N_KV_HEADS, tm, HEAD_DIM), lambda i: (0, i, 0)),
                   pl.BlockSpec((N_KV_HEADS, tm, HEAD_DIM), lambda i: (0, i, 0)),
                   pl.BlockSpec((tm, d), row), pl.BlockSpec((tm, d), row)],
        out_shape=[jax.ShapeDtypeStruct((n, D_FOURIER), F32), jax.ShapeDtypeStruct((n, D_FOURIER), F32),
                   jax.ShapeDtypeStruct((n, D_ATTN), BF16),
                   jax.ShapeDtypeStruct((N_KV_HEADS, n, HEAD_DIM), BF16),
                   jax.ShapeDtypeStruct((N_KV_HEADS, n, HEAD_DIM), BF16),
                   jax.ShapeDtypeStruct((n, d), F32), jax.ShapeDtypeStruct((n, d), F32)],
        compiler_params=_params("parallel"),
        name="in_proj",
    )(x, w, bg, qg, kg, cq, sq, ck, sk, cdft, hmq, hmk)


def _dft_stage1_kernel(zr_ref, zi_ref, m1_ref, tr_ref, ti_ref, o_ref):
    n1 = zr_ref.shape[1]
    zz = jnp.concatenate([zr_ref[0], zi_ref[0]], axis=0).astype(BF16)
    a = _dot(m1_ref[...], zz)
    ar, ai = a[:n1], a[n1:]
    tr, ti = tr_ref[...], ti_ref[...]
    o_ref[0, 0] = ar * tr - ai * ti
    o_ref[0, 1] = ar * ti + ai * tr


def _dft_stage1_call(zr, zi, m1, tr, ti, cw):
    b, n1, cols = zr.shape
    blk = pl.BlockSpec((1, n1, cw), lambda i, j: (i, 0, j))
    tw = pl.BlockSpec((n1, cw), lambda i, j: (0, j))
    return pl.pallas_call(
        _dft_stage1_kernel,
        grid=(b, cols // cw),
        in_specs=[blk, blk, pl.BlockSpec(m1.shape, lambda i, j: (0, 0)), tw, tw],
        out_specs=pl.BlockSpec((1, 2, n1, cw), lambda i, j: (i, 0, 0, j)),
        out_shape=jax.ShapeDtypeStruct((b, 2, n1, cols), F32),
        compiler_params=_params("parallel", "parallel"),
        name="dft_stage1",
    )(zr, zi, m1, tr, ti)


def _dft_stage2_kernel(a_ref, m3_ref, o_ref):
    kb, n2, c = a_ref.shape[2], a_ref.shape[3], a_ref.shape[4]
    for j in range(kb):
        rhs = jnp.concatenate([a_ref[0, 0, j], a_ref[0, 1, j]], axis=0).astype(BF16)
        o_ref[0, :, j * c:(j + 1) * c] = _dot(m3_ref[...], rhs)


def _dft_stage2_call(a, m3, kb):
    b, _, n1, n2, c = a.shape
    return pl.pallas_call(
        _dft_stage2_kernel,
        grid=(b, n1 // kb),
        in_specs=[pl.BlockSpec((1, 2, kb, n2, c), lambda i, j: (i, 0, j, 0, 0)),
                  pl.BlockSpec(m3.shape, lambda i, j: (0, 0))],
        out_specs=pl.BlockSpec((1, n2, kb * c), lambda i, j: (i, 0, j)),
        out_shape=jax.ShapeDtypeStruct((b, n2, n1 * c), F32),
        compiler_params=_params("parallel", "parallel"),
        name="dft_stage2",
    )(a, m3)


def _dft_constants(seq):
    c = FOURIER_GROUP_DIM
    n1, n2 = seq // DFT_N2, DFT_N2
    jc = np.arange(c)
    ang = 2.0 * np.pi * np.outer(jc, jc) / c
    cdft = np.concatenate([np.cos(ang), -np.sin(ang)], axis=1) / math.sqrt(c)
    j1 = np.arange(n1)
    a1 = 2.0 * np.pi * np.outer(j1, j1) / n1
    fr, fi = np.cos(a1) / math.sqrt(n1), -np.sin(a1) / math.sqrt(n1)
    m1 = np.block([[fr, -fi], [fi, fr]])
    j2 = np.arange(n2)
    a2 = 2.0 * np.pi * np.outer(j2, j2) / n2
    m3 = np.concatenate([np.cos(a2), np.sin(a2)], axis=1) / math.sqrt(n2)
    at = 2.0 * np.pi * np.outer(j1, j2) / seq
    return (jnp.asarray(cdft, BF16), jnp.asarray(m1, BF16), jnp.asarray(m3, BF16),
            jnp.asarray(np.cos(at), F32), jnp.asarray(-np.sin(at), F32))


def _attn_kernel(q_ref, k_ref, v_ref, o_ref, qs_ref, *, tq, tk):
    seq = k_ref.shape[1]
    for g in range(Q_PER_KV):
        qs_ref[g * tq:(g + 1) * tq, :] = q_ref[:, g * HEAD_DIM:(g + 1) * HEAD_DIM]
    qs = qs_ref[...]
    rows = Q_PER_KV * tq

    def body(c, carry):
        m, l, acc = carry
        start = pl.multiple_of(c * tk, tk)
        kc = k_ref[0, pl.ds(start, tk), :]
        vc = v_ref[0, pl.ds(start, tk), :]
        s = _dot_nt(qs, kc)
        m_new = jnp.maximum(m, jnp.max(s, axis=-1, keepdims=True))
        alpha = jnp.exp(m - m_new)
        p = jnp.exp(s - m_new)
        l = alpha * l + jnp.sum(p, axis=-1, keepdims=True)
        acc = alpha * acc + _dot(p.astype(BF16), vc)
        return m_new, l, acc

    init = (jnp.full((rows, 1), -jnp.inf, F32), jnp.zeros((rows, 1), F32), jnp.zeros((rows, HEAD_DIM), F32))
    _, l, acc = lax.fori_loop(0, seq // tk, body, init)
    o = acc / l
    o_ref[...] = jnp.concatenate([o[g * tq:(g + 1) * tq] for g in range(Q_PER_KV)], axis=1)


def _attn_call(q, k, v, batch, seq, tq, tk):
    n = q.shape[0]
    qblocks = seq // tq
    gw = Q_PER_KV * HEAD_DIM
    return pl.pallas_call(
        functools.partial(_attn_kernel, tq=tq, tk=tk),
        grid=(batch, N_KV_HEADS, qblocks),
        in_specs=[pl.BlockSpec((tq, gw), lambda b, j, i: (b * qblocks + i, j)),
                  pl.BlockSpec((1, seq, HEAD_DIM), lambda b, j, i: (j, b, 0)),
                  pl.BlockSpec((1, seq, HEAD_DIM), lambda b, j, i: (j, b, 0))],
        out_specs=pl.BlockSpec((tq, gw), lambda b, j, i: (b * qblocks + i, j)),
        out_shape=jax.ShapeDtypeStruct((n, D_ATTN), F32),
        scratch_shapes=[pltpu.VMEM((Q_PER_KV * tq, HEAD_DIM), BF16)],
        compiler_params=_params("parallel", "parallel", "parallel"),
        name="attention",
    )(q, k, v)


def _mix_kernel(yf_ref, ya_ref, gf_ref, ga_ref, x_ref, wf_ref, wa_ref, wo_ref, g_ref, b_ref, o_ref, *, alpha):
    mixed = (gf_ref[...] * _dot(yf_ref[...].astype(BF16), wf_ref[...])
             + ga_ref[...] * _dot(ya_ref[...].astype(BF16), wa_ref[...]))
    z = alpha * x_ref[...] + _dot(mixed.astype(BF16), wo_ref[...])
    o_ref[...] = _layer_norm(z, g_ref[...], b_ref[...])


def _mix_call(yf, ya, gf, ga, x, wf, wa, wo, g, b, tm, alpha):
    n, d = x.shape
    row = lambda i: (i, 0)
    const = lambda i: (0, 0)
    return pl.pallas_call(
        functools.partial(_mix_kernel, alpha=alpha),
        grid=(n // tm,),
        in_specs=[pl.BlockSpec((tm, D_FOURIER), row), pl.BlockSpec((tm, D_ATTN), row),
                  pl.BlockSpec((tm, d), row), pl.BlockSpec((tm, d), row), pl.BlockSpec((tm, d), row),
                  pl.BlockSpec(wf.shape, const), pl.BlockSpec(wa.shape, const), pl.BlockSpec(wo.shape, const),
                  pl.BlockSpec((1, d), const), pl.BlockSpec((1, d), const)],
        out_specs=pl.BlockSpec((tm, d), row),
        out_shape=jax.ShapeDtypeStruct((n, d), F32),
        compiler_params=_params("parallel"),
        name="mix",
    )(yf, ya, gf, ga, x, wf, wa, wo, g.reshape(1, d), b.reshape(1, d))


_NEG_INF = float("-inf")


def _extract_top(s, order, count):
    rank = jnp.full(s.shape, float(count), F32)
    big = float(2 ** 20)
    vals = []
    for a in range(count):
        m = jnp.max(s, axis=0, keepdims=True)
        first = jnp.min(jnp.where(s == m, order, big), axis=0, keepdims=True)
        sel = order == first
        rank = jnp.where(sel, float(a), rank)
        s = jnp.where(sel, _NEG_INF, s)
        vals.append(m)
    return vals, rank


def _candidate_rows(v1, v2, combine):
    v2_all = jnp.concatenate(v2, axis=0)
    v2_lo = v2_all[:8]
    pieces = [combine(v1[0], v2_all)]
    pieces += [combine(v1[a], v2_lo) for a in range(1, 8)]
    pieces.append(combine(jnp.concatenate(v1[8:], axis=0), v2[0]))
    return jnp.concatenate(pieces, axis=0)


def _candidate_positions(tokens):
    pos = ([b for b in range(16)] + [a * 16 + b for a in range(1, 8) for b in range(8)]
           + [a * 16 for a in range(8, 16)])
    col = jnp.asarray(np.asarray(pos, np.float32).reshape(-1, 1))
    return jnp.broadcast_to(col, (len(pos), tokens))


def _peer_route_kernel(x_ref, wq_ref, keys_ref, pos_ref, gd_ref):
    tokens = x_ref.shape[0]
    xb = x_ref[...].astype(BF16)
    key_order = lax.broadcasted_iota(jnp.int32, (PEER_N_KEYS, tokens), 0).astype(F32)
    cand_pos = pos_ref[...]
    for h in range(PEER_HEADS):
        scores, vals, ranks = [], [], []
        for c in range(2):
            col = (2 * h + c) * PEER_HALF
            qp = _dot(xb, wq_ref[:, col:col + PEER_HALF]).astype(BF16)
            s = _dot_nt(keys_ref[h, c], qp)
            v, r = _extract_top(s, key_order, PEER_TOPK)
            scores.append(s)
            vals.append(v)
            ranks.append(r)
        cand = _candidate_rows(vals[0], vals[1], lambda a, b: a + b)
        _, crank = _extract_top(cand, cand_pos, PEER_TOPK)
        chosen = crank < float(PEER_TOPK)
        e1 = [jnp.exp(v - vals[0][0]) for v in vals[0]]
        e2 = [jnp.exp(v - vals[1][0]) for v in vals[1]]
        ecand = _candidate_rows(e1, e2, lambda a, b: a * b)
        z = jnp.sum(jnp.where(chosen, ecand, 0.0), axis=0, keepdims=True)
        cnt = jnp.where(chosen, 1.0, 0.0)
        limit = [jnp.sum(cnt[0:16], axis=0, keepdims=True)]
        limit += [jnp.sum(cnt[8 + 8 * a:16 + 8 * a], axis=0, keepdims=True) for a in range(1, 8)]
        limit += [cnt[72 + a:73 + a] for a in range(8)]
        lim_dense = jnp.zeros((PEER_N_KEYS, tokens), F32)
        for a in range(PEER_TOPK):
            lim_dense = jnp.where(ranks[0] == float(a), limit[a], lim_dense)
        gd_ref[0, h] = jnp.exp(scores[0] - vals[0][0]) / z
        gd_ref[1, h] = lim_dense
        gd_ref[2, h] = jnp.exp(scores[1] - vals[1][0])
        gd_ref[3, h] = ranks[1]


def _peer_route_call(x, wq, keys, tm):
    n, d = x.shape
    pos = _candidate_positions(tm)
    return pl.pallas_call(
        _peer_route_kernel,
        grid=(n // tm,),
        in_specs=[pl.BlockSpec((tm, d), lambda i: (i, 0)),
                  pl.BlockSpec(wq.shape, lambda i: (0, 0)),
                  pl.BlockSpec(keys.shape, lambda i: (0, 0, 0, 0)),
                  pl.BlockSpec(pos.shape, lambda i: (0, 0))],
        out_specs=pl.BlockSpec((4, PEER_HEADS, PEER_N_KEYS, tm), lambda i: (0, 0, 0, i)),
        out_shape=jax.ShapeDtypeStruct((4, PEER_HEADS, PEER_N_KEYS, n), F32),
        compiler_params=_params("parallel"),
        name="peer_route",
    )(x, wq, keys, pos)


def _peer_expert_kernel(x_ref, gd_ref, u_ref, vt_ref, g_ref, b_ref, o_ref, acc_ref, p_ref, *, alpha, rows_per_step):
    j = pl.program_id(1)

    @pl.when(j == 0)
    def _():
        acc_ref[...] = jnp.zeros_like(acc_ref)

    xb = x_ref[...].astype(BF16)
    act = _dot_nt(u_ref[...], xb)
    act = 0.5 * act * (1.0 + lax.erf(act * math.sqrt(0.5)))
    for r in range(rows_per_step):
        i1 = j * rows_per_step + r
        gate = None
        for h in range(PEER_HEADS):
            e1n = gd_ref[0, h, pl.ds(i1, 1), :]
            lim = gd_ref[1, h, pl.ds(i1, 1), :]
            term = jnp.where(gd_ref[3, h] < lim, gd_ref[2, h] * e1n, 0.0)
            gate = term if gate is None else gate + term
        lo, hi = r * PEER_N_KEYS, (r + 1) * PEER_N_KEYS
        p_ref[lo:hi, :] = (gate * act[lo:hi, :]).astype(BF16)
    acc_ref[...] += _dot(vt_ref[...], p_ref[...])

    @pl.when(j == pl.num_programs(1) - 1)
    def _():
        z = alpha * x_ref[...] + acc_ref[...].T
        o_ref[...] = _layer_norm(z, g_ref[...], b_ref[...])


def _peer_expert_call(x, gd, u, vt, g, b, tm, rows_per_step, alpha):
    n, d = x.shape
    te = rows_per_step * PEER_N_KEYS
    n_exp = u.shape[0]
    return pl.pallas_call(
        functools.partial(_peer_expert_kernel, alpha=alpha, rows_per_step=rows_per_step),
        grid=(n // tm, n_exp // te),
        in_specs=[pl.BlockSpec((tm, d), lambda i, j: (i, 0)),
                  pl.BlockSpec((4, PEER_HEADS, PEER_N_KEYS, tm), lambda i, j: (0, 0, 0, i)),
                  pl.BlockSpec((te, d), lambda i, j: (j, 0)),
                  pl.BlockSpec((d, te), lambda i, j: (0, j)),
                  pl.BlockSpec((1, d), lambda i, j: (0, 0)),
                  pl.BlockSpec((1, d), lambda i, j: (0, 0))],
        out_specs=pl.BlockSpec((tm, d), lambda i, j: (i, 0)),
        out_shape=jax.ShapeDtypeStruct((n, d), F32),
        scratch_shapes=[pltpu.VMEM((d, tm), F32), pltpu.VMEM((te, tm), BF16)],
        compiler_params=_params("parallel", "arbitrary"),
        name="peer_experts",
    )(x, gd, u, vt, g.reshape(1, d), b.reshape(1, d))


def _rope_tables(seq):
    rows = seq // GRID_W
    row_id = jnp.repeat(jnp.arange(rows, dtype=F32), GRID_W)
    col_id = jnp.tile(jnp.arange(GRID_W, dtype=F32), rows)
    inv_freq = 1.0 / (ROPE_THETA ** (jnp.arange(0, ROPE_AXIS_DIM, 2, dtype=F32) / ROPE_AXIS_DIM))
    ang_r = row_id[:, None] * inv_freq[None, :]
    ang_c = col_id[:, None] * inv_freq[None, :]
    ang = jnp.concatenate([ang_r, ang_r, ang_c, ang_c], axis=-1)
    sign = jnp.where((jnp.arange(HEAD_DIM) % ROPE_AXIS_DIM) < ROPE_HALF, -1.0, 1.0).astype(F32)
    return jnp.cos(ang), jnp.sin(ang) * sign[None, :]


def _head_mean_matrix(heads):
    m = np.kron(np.eye(heads), np.full((HEAD_DIM, HEAD_DIM), 1.0 / HEAD_DIM))
    return jnp.asarray(m, BF16)


TM_LN = 512
TM_IN = 256
DFT1_COLS = 2048
DFT2_K1 = 8
ATTN_TQ = 128
ATTN_TK = 512
TM_MIX = 256
TM_ROUTE = 256
TM_EXPERT = 512
EXPERT_ROWS_PER_STEP = 4


def kernel(x, ln0_g, ln0_b, w_in, b_gate, q_norm_g, k_norm_g, w_branch_fourier, w_branch_attn, w_out,
           ln1_g, ln1_b, peer_w_query, peer_sub_keys, peer_u, peer_v, ln2_g, ln2_b):
    batch, seq, d = x.shape
    n = batch * seq
    depth = w_in.shape[0]
    alpha = (2.0 * depth) ** 0.25
    n1 = seq // DFT_N2

    cos, sin_signed = _rope_tables(seq)
    cq, sq = jnp.tile(cos, (1, N_Q_HEADS)), jnp.tile(sin_signed, (1, N_Q_HEADS))
    ck, sk = jnp.tile(cos, (1, N_KV_HEADS)), jnp.tile(sin_signed, (1, N_KV_HEADS))
    hmq, hmk = _head_mean_matrix(N_Q_HEADS), _head_mean_matrix(N_KV_HEADS)
    cdft, m1, m3, tw_r, tw_i = _dft_constants(seq)
    tw_r = jnp.repeat(tw_r, D_FOURIER, axis=1)
    tw_i = jnp.repeat(tw_i, D_FOURIER, axis=1)

    xs = _ln_call(x.reshape(n, d), ln0_g, ln0_b, TM_LN)
    for l in range(depth):
        qg = jnp.tile(q_norm_g[l], N_Q_HEADS).reshape(1, D_ATTN)
        kg = jnp.tile(k_norm_g[l], N_KV_HEADS).reshape(1, D_KV)
        zr, zi, q, k, v, gf, ga = _in_proj_call(
            xs, w_in[l].astype(BF16), b_gate[l], qg, kg, cq, sq, ck, sk, cdft, hmq, hmk, TM_IN, seq)
        a = _dft_stage1_call(zr.reshape(batch, n1, DFT_N2 * D_FOURIER), zi.reshape(batch, n1, DFT_N2 * D_FOURIER),
                             m1, tw_r, tw_i, DFT1_COLS)
        yf = _dft_stage2_call(a.reshape(batch, 2, n1, DFT_N2, D_FOURIER), m3, DFT2_K1)
        ya = _attn_call(q, k, v, batch, seq, ATTN_TQ, ATTN_TK)
        xs = _mix_call(yf.reshape(n, D_FOURIER), ya, gf, ga, xs,
                       w_branch_fourier[l].astype(BF16), w_branch_attn[l].astype(BF16), w_out[l].astype(BF16),
                       ln1_g[l], ln1_b[l], TM_MIX, alpha)
        gd = _peer_route_call(xs, peer_w_query[l].astype(BF16), peer_sub_keys[l].astype(BF16), TM_ROUTE)
        xs = _peer_expert_call(xs, gd, peer_u[l].astype(BF16), peer_v[l].astype(BF16).T,
                               ln2_g[l], ln2_b[l], TM_EXPERT, EXPERT_ROWS_PER_STEP, alpha)
    return xs.reshape(batch, seq, d)
```

```python
import functools
import math

import numpy as np
import jax
import jax.numpy as jnp
from jax import lax
from jax.experimental import pallas as pl
from jax.experimental.pallas import tpu as pltpu

F32 = jnp.float32
BF16 = jnp.bfloat16

D_MODEL = 1024
GRID_W = 64
N_FOURIER_GROUPS = 4
FOURIER_GROUP_DIM = 128
D_FOURIER = N_FOURIER_GROUPS * FOURIER_GROUP_DIM
N_Q_HEADS = 8
N_KV_HEADS = 2
Q_PER_KV = N_Q_HEADS // N_KV_HEADS
HEAD_DIM = 64
D_ATTN = N_Q_HEADS * HEAD_DIM
D_KV = N_KV_HEADS * HEAD_DIM
ROPE_AXIS_DIM = HEAD_DIM // 2
ROPE_HALF = ROPE_AXIS_DIM // 2
ROPE_THETA = 10000.0
PEER_HEADS = 8
PEER_N_KEYS = 128
PEER_TOPK = 16
PEER_HALF = 128
LN_EPS = 1e-5
RMS_EPS = 1e-6

_C_XF = 0
_C_Q = D_FOURIER
_C_K = _C_Q + D_ATTN
_C_V = _C_K + D_KV
_C_GF = _C_V + D_KV
_C_GA = _C_GF + D_MODEL
_C_END = _C_GA + D_MODEL

DFT_N2 = 128

VMEM_LIMIT_BYTES = 56 * 1024 * 1024


def _params(*sem):
    return pltpu.CompilerParams(dimension_semantics=sem, vmem_limit_bytes=VMEM_LIMIT_BYTES)


def _dot(a, b):
    return jnp.dot(a, b, preferred_element_type=F32)


def _dot_nt(a, b):
    return lax.dot_general(a, b, (((1,), (1,)), ((), ())), preferred_element_type=F32)


def _layer_norm(z, g, b):
    mu = jnp.mean(z, axis=-1, keepdims=True)
    zc = z - mu
    var = jnp.mean(zc * zc, axis=-1, keepdims=True)
    return zc * lax.rsqrt(var + LN_EPS) * g + b


def _ln_kernel(x_ref, g_ref, b_ref, o_ref):
    o_ref[...] = _layer_norm(x_ref[...], g_ref[...], b_ref[...])


def _ln_call(x, g, b, tm):
    n, d = x.shape
    return pl.pallas_call(
        _ln_kernel,
        grid=(n // tm,),
        in_specs=[pl.BlockSpec((tm, d), lambda i: (i, 0)),
                  pl.BlockSpec((1, d), lambda i: (0, 0)),
                  pl.BlockSpec((1, d), lambda i: (0, 0))],
        out_specs=pl.BlockSpec((tm, d), lambda i: (i, 0)),
        out_shape=jax.ShapeDtypeStruct((n, d), F32),
        compiler_params=_params("parallel"),
        name="ln0",
    )(x, g.reshape(1, d), b.reshape(1, d))


def _rms_rope(t, gain, head_mean, cos, sin_signed):
    sq = t * t
    hi = sq.astype(BF16)
    lo = (sq - hi.astype(F32)).astype(BF16)
    ms = _dot(hi, head_mean) + _dot(lo, head_mean)
    tn = t * lax.rsqrt(ms + RMS_EPS) * gain
    width = t.shape[-1]
    lane = lax.broadcasted_iota(jnp.int32, tn.shape, 1)
    first_half = (lane % ROPE_AXIS_DIM) < ROPE_HALF
    rot = jnp.where(first_half, pltpu.roll(tn, width - ROPE_HALF, 1), pltpu.roll(tn, ROPE_HALF, 1))
    return tn * cos + rot * sin_signed


def _in_proj_kernel(x_ref, w_ref, bg_ref, qg_ref, kg_ref, cq_ref, sq_ref, ck_ref, sk_ref,
                    cdft_ref, hmq_ref, hmk_ref,
                    zr_ref, zi_ref, q_ref, k_ref, vt_ref, gf_ref, ga_ref):
    xb = x_ref[...].astype(BF16)

    def seg(a, b):
        return _dot(xb, w_ref[:, a:b])

    xf = seg(_C_XF, _C_Q).astype(BF16)
    for g in range(N_FOURIER_GROUPS):
        lo, hi = g * FOURIER_GROUP_DIM, (g + 1) * FOURIER_GROUP_DIM
        z = _dot(xf[:, lo:hi], cdft_ref[...])
        zr_ref[:, lo:hi] = z[:, :FOURIER_GROUP_DIM]
        zi_ref[:, lo:hi] = z[:, FOURIER_GROUP_DIM:]

    q = _rms_rope(seg(_C_Q, _C_K), qg_ref[...], hmq_ref[...], cq_ref[...], sq_ref[...])
    q_ref[...] = (q * (HEAD_DIM ** -0.5)).astype(BF16)
    k = _rms_rope(seg(_C_K, _C_V), kg_ref[...], hmk_ref[...], ck_ref[...], sk_ref[...]).astype(BF16)
    vt = seg(_C_V, _C_GF).T.astype(BF16)
    for j in range(N_KV_HEADS):
        k_ref[j] = k[:, j * HEAD_DIM:(j + 1) * HEAD_DIM]
        vt_ref[j, 0] = vt[j * HEAD_DIM:(j + 1) * HEAD_DIM, :]

    gf_ref[...] = jax.nn.sigmoid(seg(_C_GF, _C_GA) + bg_ref[0:1, :])
    ga_ref[...] = jax.nn.sigmoid(seg(_C_GA, _C_END) + bg_ref[1:2, :])


def _in_proj_call(x, w, bg, qg, kg, cq, sq, ck, sk, cdft, hmq, hmk, tm, seq):
    n, d = x.shape
    pos_blocks = seq // tm
    row = lambda i: (i, 0)
    pos = lambda i: (i % pos_blocks, 0)
    const = lambda i: (0, 0)
    full = lambda a: pl.BlockSpec(a.shape, const)
    return pl.pallas_call(
        _in_proj_kernel,
        grid=(n // tm,),
        in_specs=[pl.BlockSpec((tm, d), row), full(w), full(bg), full(qg), full(kg),
                  pl.BlockSpec((tm, D_ATTN), pos), pl.BlockSpec((tm, D_ATTN), pos),
                  pl.BlockSpec((tm, D_KV), pos), pl.BlockSpec((tm, D_KV), pos),
                  full(cdft), full(hmq), full(hmk)],
        out_specs=[pl.BlockSpec((tm, D_FOURIER), row), pl.BlockSpec((tm, D_FOURIER), row),
                   pl.BlockSpec((tm, D_ATTN), row),
                   pl.BlockSpec((N_KV_HEADS, tm, HEAD_DIM), lambda i: (0, i, 0)),
                   pl.BlockSpec((N_KV_HEADS, 1, HEAD_DIM, tm), lambda i: (0, i, 0, 0)),
                   pl.BlockSpec((tm, d), row), pl.BlockSpec((tm, d), row)],
        out_shape=[jax.ShapeDtypeStruct((n, D_FOURIER), F32), jax.ShapeDtypeStruct((n, D_FOURIER), F32),
                   jax.ShapeDtypeStruct((n, D_ATTN), BF16),
                   jax.ShapeDtypeStruct((N_KV_HEADS, n, HEAD_DIM), BF16),
                   jax.ShapeDtypeStruct((N_KV_HEADS, n // tm, HEAD_DIM, tm), BF16),
                   jax.ShapeDtypeStruct((n, d), F32), jax.ShapeDtypeStruct((n, d), F32)],
        compiler_params=_params("parallel"),
        name="in_proj",
    )(x, w, bg, qg, kg, cq, sq, ck, sk, cdft, hmq, hmk)


def _dft_stage1_kernel(zr_ref, zi_ref, m1_ref, tr_ref, ti_ref, o_ref):
    n1 = zr_ref.shape[1]
    zz = jnp.concatenate([zr_ref[0], zi_ref[0]], axis=0).astype(BF16)
    a = _dot(m1_ref[...], zz)
    ar, ai = a[:n1], a[n1:]
    tr, ti = tr_ref[...], ti_ref[...]
    o_ref[0, 0] = ar * tr - ai * ti
    o_ref[0, 1] = ar * ti + ai * tr


def _dft_stage1_call(zr, zi, m1, tr, ti, cw):
    b, n1, cols = zr.shape
    blk = pl.BlockSpec((1, n1, cw), lambda i, j: (i, 0, j))
    tw = pl.BlockSpec((n1, cw), lambda i, j: (0, j))
    return pl.pallas_call(
        _dft_stage1_kernel,
        grid=(b, cols // cw),
        in_specs=[blk, blk, pl.BlockSpec(m1.shape, lambda i, j: (0, 0)), tw, tw],
        out_specs=pl.BlockSpec((1, 2, n1, cw), lambda i, j: (i, 0, 0, j)),
        out_shape=jax.ShapeDtypeStruct((b, 2, n1, cols), F32),
        compiler_params=_params("parallel", "parallel"),
        name="dft_stage1",
    )(zr, zi, m1, tr, ti)


def _dft_stage2_kernel(a_ref, m3_ref, o_ref):
    kb, n2, c = a_ref.shape[2], a_ref.shape[3], a_ref.shape[4]
    for j in range(kb):
        rhs = jnp.concatenate([a_ref[0, 0, j], a_ref[0, 1, j]], axis=0).astype(BF16)
        o_ref[0, :, j * c:(j + 1) * c] = _dot(m3_ref[...], rhs)


def _dft_stage2_call(a, m3, kb):
    b, _, n1, n2, c = a.shape
    return pl.pallas_call(
        _dft_stage2_kernel,
        grid=(b, n1 // kb),
        in_specs=[pl.BlockSpec((1, 2, kb, n2, c), lambda i, j: (i, 0, j, 0, 0)),
                  pl.BlockSpec(m3.shape, lambda i, j: (0, 0))],
        out_specs=pl.BlockSpec((1, n2, kb * c), lambda i, j: (i, 0, j)),
        out_shape=jax.ShapeDtypeStruct((b, n2, n1 * c), F32),
        compiler_params=_params("parallel", "parallel"),
        name="dft_stage2",
    )(a, m3)


def _dft_constants(seq):
    c = FOURIER_GROUP_DIM
    n1, n2 = seq // DFT_N2, DFT_N2
    jc = np.arange(c)
    ang = 2.0 * np.pi * np.outer(jc, jc) / c
    cdft = np.concatenate([np.cos(ang), -np.sin(ang)], axis=1) / math.sqrt(c)
    j1 = np.arange(n1)
    a1 = 2.0 * np.pi * np.outer(j1, j1) / n1
    fr, fi = np.cos(a1) / math.sqrt(n1), -np.sin(a1) / math.sqrt(n1)
    m1 = np.block([[fr, -fi], [fi, fr]])
    j2 = np.arange(n2)
    a2 = 2.0 * np.pi * np.outer(j2, j2) / n2
    m3 = np.concatenate([np.cos(a2), np.sin(a2)], axis=1) / math.sqrt(n2)
    at = 2.0 * np.pi * np.outer(j1, j2) / seq
    return (jnp.asarray(cdft, BF16), jnp.asarray(m1, BF16), jnp.asarray(m3, BF16),
            jnp.asarray(np.cos(at), F32), jnp.asarray(-np.sin(at), F32))


def _attn_kernel(q_ref, k_ref, vt_ref, o_ref, qs_ref, st0_ref, st1_ref, p0_ref, p1_ref, acc_ref, *, tq):
    n_chunks, _, tk = vt_ref.shape[1:]
    assert n_chunks % 2 == 0 and n_chunks >= 4
    nq = Q_PER_KV * tq
    for g in range(Q_PER_KV):
        qs_ref[g * tq:(g + 1) * tq, :] = q_ref[:, g * HEAD_DIM:(g + 1) * HEAD_DIM]
    acc_ref[...] = jnp.zeros_like(acc_ref)

    def scores(c, st_ref):
        start = pl.multiple_of(c * tk, tk)
        st_ref[...] = _dot_nt(k_ref[0, pl.ds(start, tk), :], qs_ref[...])

    def softmax(st_ref, p_ref, m, l):
        st = st_ref[...]
        m_new = jnp.maximum(m, jnp.max(st, axis=0, keepdims=True))
        alpha = jnp.exp(m - m_new)
        pt = jnp.exp(st - m_new)
        p_ref[...] = pt.astype(BF16)
        return m_new, alpha * l + jnp.sum(pt, axis=0, keepdims=True), alpha

    def values(c, p_ref, alpha):
        acc_ref[...] = alpha * acc_ref[...] + _dot(vt_ref[0, c], p_ref[...])

    def step(i, carry, st_cur, st_nxt, p_cur, p_prv, last=False):
        m, l, alpha_prv = carry
        values(i - 1, p_prv, alpha_prv)
        m, l, alpha = softmax(st_cur, p_cur, m, l)
        if not last:
            scores(i + 1, st_nxt)
        return m, l, alpha

    scores(0, st0_ref)
    m, l, alpha = softmax(st0_ref, p0_ref, jnp.full((1, nq), -jnp.inf, F32), jnp.zeros((1, nq), F32))
    scores(1, st1_ref)

    def body(ii, carry):
        i = 2 * ii + 1
        carry = step(i, carry, st1_ref, st0_ref, p1_ref, p0_ref)
        return step(i + 1, carry, st0_ref, st1_ref, p0_ref, p1_ref)

    carry = lax.fori_loop(0, n_chunks // 2 - 1, body, (m, l, alpha))
    m, l, alpha = step(n_chunks - 1, carry, st1_ref, st0_ref, p1_ref, p0_ref, last=True)
    values(n_chunks - 1, p1_ref, alpha)

    ot = acc_ref[...] / l
    o = jnp.concatenate([ot, jnp.zeros_like(ot)], axis=0).T
    o_ref[...] = jnp.concatenate([o[g * tq:(g + 1) * tq, :HEAD_DIM] for g in range(Q_PER_KV)], axis=1)


def _attn_call(q, k, vt, batch, seq, tq):
    n = q.shape[0]
    qblocks = seq // tq
    gw = Q_PER_KV * HEAD_DIM
    tk = vt.shape[-1]
    chunks = seq // tk
    nq = Q_PER_KV * tq
    return pl.pallas_call(
        functools.partial(_attn_kernel, tq=tq),
        grid=(batch, N_KV_HEADS, qblocks),
        in_specs=[pl.BlockSpec((tq, gw), lambda b, j, i: (b * qblocks + i, j)),
                  pl.BlockSpec((1, seq, HEAD_DIM), lambda b, j, i: (j, b, 0)),
                  pl.BlockSpec((1, chunks, HEAD_DIM, tk), lambda b, j, i: (j, b, 0, 0))],
        out_specs=pl.BlockSpec((tq, gw), lambda b, j, i: (b * qblocks + i, j)),
        out_shape=jax.ShapeDtypeStruct((n, D_ATTN), F32),
        scratch_shapes=[pltpu.VMEM((nq, HEAD_DIM), BF16),
                        pltpu.VMEM((tk, nq), F32), pltpu.VMEM((tk, nq), F32),
                        pltpu.VMEM((tk, nq), BF16), pltpu.VMEM((tk, nq), BF16),
                        pltpu.VMEM((HEAD_DIM, nq), F32)],
        compiler_params=_params("parallel", "parallel", "parallel"),
        name="attention",
    )(q, k, vt)


def _mix_kernel(yf_ref, ya_ref, gf_ref, ga_ref, x_ref, wf_ref, wa_ref, wo_ref, g_ref, b_ref, o_ref, *, alpha):
    mixed = (gf_ref[...] * _dot(yf_ref[...].astype(BF16), wf_ref[...])
             + ga_ref[...] * _dot(ya_ref[...].astype(BF16), wa_ref[...]))
    z = alpha * x_ref[...] + _dot(mixed.astype(BF16), wo_ref[...])
    o_ref[...] = _layer_norm(z, g_ref[...], b_ref[...])


def _mix_call(yf, ya, gf, ga, x, wf, wa, wo, g, b, tm, alpha):
    n, d = x.shape
    row = lambda i: (i, 0)
    const = lambda i: (0, 0)
    return pl.pallas_call(
        functools.partial(_mix_kernel, alpha=alpha),
        grid=(n // tm,),
        in_specs=[pl.BlockSpec((tm, D_FOURIER), row), pl.BlockSpec((tm, D_ATTN), row),
                  pl.BlockSpec((tm, d), row), pl.BlockSpec((tm, d), row), pl.BlockSpec((tm, d), row),
                  pl.BlockSpec(wf.shape, const), pl.BlockSpec(wa.shape, const), pl.BlockSpec(wo.shape, const),
                  pl.BlockSpec((1, d), const), pl.BlockSpec((1, d), const)],
        out_specs=pl.BlockSpec((tm, d), row),
        out_shape=jax.ShapeDtypeStruct((n, d), F32),
        compiler_params=_params("parallel"),
        name="mix",
    )(yf, ya, gf, ga, x, wf, wa, wo, g.reshape(1, d), b.reshape(1, d))


_NEG_INF = float("-inf")


def _extract_top(s, order, count):
    rank = jnp.full(s.shape, float(count), F32)
    big = float(2 ** 20)
    vals = []
    for a in range(count):
        m = jnp.max(s, axis=0, keepdims=True)
        first = jnp.min(jnp.where(s == m, order, big), axis=0, keepdims=True)
        sel = order == first
        rank = jnp.where(sel, float(a), rank)
        s = jnp.where(sel, _NEG_INF, s)
        vals.append(m)
    return vals, rank


def _candidate_rows(v1, v2, combine):
    v2_all = jnp.concatenate(v2, axis=0)
    v2_lo = v2_all[:8]
    pieces = [combine(v1[0], v2_all)]
    pieces += [combine(v1[a], v2_lo) for a in range(1, 8)]
    pieces.append(combine(jnp.concatenate(v1[8:], axis=0), v2[0]))
    return jnp.concatenate(pieces, axis=0)


def _candidate_positions(tokens):
    pos = ([b for b in range(16)] + [a * 16 + b for a in range(1, 8) for b in range(8)]
           + [a * 16 for a in range(8, 16)])
    col = jnp.asarray(np.asarray(pos, np.float32).reshape(-1, 1))
    return jnp.broadcast_to(col, (len(pos), tokens))


def _peer_route_kernel(x_ref, wq_ref, keys_ref, pos_ref, g1_ref, g2_ref):
    tokens = x_ref.shape[0]
    xb = x_ref[...].astype(BF16)
    key_order = lax.broadcasted_iota(jnp.int32, (PEER_N_KEYS, tokens), 0).astype(F32)
    cand_pos = pos_ref[...]
    for h in range(PEER_HEADS):
        scores, vals, ranks = [], [], []
        for c in range(2):
            col = (2 * h + c) * PEER_HALF
            qp = _dot(xb, wq_ref[:, col:col + PEER_HALF]).astype(BF16)
            s = _dot_nt(keys_ref[h, c], qp)
            v, r = _extract_top(s, key_order, PEER_TOPK)
            scores.append(s)
            vals.append(v)
            ranks.append(r)
        cand = _candidate_rows(vals[0], vals[1], lambda a, b: a + b)
        _, crank = _extract_top(cand, cand_pos, PEER_TOPK)
        chosen = crank < float(PEER_TOPK)
        e1 = [jnp.exp(v - vals[0][0]) for v in vals[0]]
        e2 = [jnp.exp(v - vals[1][0]) for v in vals[1]]
        ecand = _candidate_rows(e1, e2, lambda a, b: a * b)
        z = jnp.sum(jnp.where(chosen, ecand, 0.0), axis=0, keepdims=True)
        cnt = jnp.where(chosen, 1.0, 0.0)
        limit = [jnp.sum(cnt[0:16], axis=0, keepdims=True)]
        limit += [jnp.sum(cnt[8 + 8 * a:16 + 8 * a], axis=0, keepdims=True) for a in range(1, 8)]
        limit += [cnt[72 + a:73 + a] for a in range(8)]
        lim_dense = jnp.zeros((PEER_N_KEYS, tokens), F32)
        for a in range(PEER_TOPK):
            lim_dense = jnp.where(ranks[0] == float(a), limit[a], lim_dense)
        g1_ref[0, h] = jnp.exp(scores[0] - vals[0][0]) / z
        g1_ref[1, h] = lim_dense
        g2_ref[0, h] = jnp.exp(scores[1] - vals[1][0]).astype(BF16)
        g2_ref[1, h] = ranks[1].astype(BF16)


def _peer_route_call(x, wq, keys, tm):
    n, d = x.shape
    pos = _candidate_positions(tm)
    return pl.pallas_call(
        _peer_route_kernel,
        grid=(n // tm,),
        in_specs=[pl.BlockSpec((tm, d), lambda i: (i, 0)),
                  pl.BlockSpec(wq.shape, lambda i: (0, 0)),
                  pl.BlockSpec(keys.shape, lambda i: (0, 0, 0, 0)),
                  pl.BlockSpec(pos.shape, lambda i: (0, 0))],
        out_specs=[pl.BlockSpec((2, PEER_HEADS, PEER_N_KEYS, tm), lambda i: (0, 0, 0, i)),
                   pl.BlockSpec((2, PEER_HEADS, PEER_N_KEYS, tm), lambda i: (0, 0, 0, i))],
        out_shape=[jax.ShapeDtypeStruct((2, PEER_HEADS, PEER_N_KEYS, n), F32),
                   jax.ShapeDtypeStruct((2, PEER_HEADS, PEER_N_KEYS, n), BF16)],
        compiler_params=_params("parallel"),
        name="peer_route",
    )(x, wq, keys, pos)


def _peer_expert_kernel(x_ref, g1_ref, g2_ref, u_ref, vt_ref, g_ref, b_ref, o_ref, acc_ref, p_ref, *,
                        alpha, rows_per_step):
    j = pl.program_id(1)

    @pl.when(j == 0)
    def _():
        acc_ref[...] = jnp.zeros_like(acc_ref)

    xb = x_ref[...].astype(BF16)
    act = _dot_nt(u_ref[...], xb)
    act = 0.5 * act * (1.0 + lax.erf(act * math.sqrt(0.5)))
    for r in range(rows_per_step):
        i1 = j * rows_per_step + r
        gate = None
        for h in range(PEER_HEADS):
            e1n = g1_ref[0, h, pl.ds(i1, 1), :].astype(BF16)
            lim = g1_ref[1, h, pl.ds(i1, 1), :].astype(BF16)
            term = jnp.where(g2_ref[1, h] < lim, g2_ref[0, h] * e1n, jnp.zeros((), BF16))
            gate = term if gate is None else gate + term
        lo, hi = r * PEER_N_KEYS, (r + 1) * PEER_N_KEYS
        p_ref[lo:hi, :] = gate * act[lo:hi, :].astype(BF16)
    acc_ref[...] += _dot(vt_ref[...], p_ref[...])

    @pl.when(j == pl.num_programs(1) - 1)
    def _():
        z = alpha * x_ref[...] + acc_ref[...].T
        o_ref[...] = _layer_norm(z, g_ref[...], b_ref[...])


def _peer_expert_call(x, g1, g2, u, vt, g, b, tm, rows_per_step, alpha):
    n, d = x.shape
    te = rows_per_step * PEER_N_KEYS
    n_exp = u.shape[0]
    return pl.pallas_call(
        functools.partial(_peer_expert_kernel, alpha=alpha, rows_per_step=rows_per_step),
        grid=(n // tm, n_exp // te),
        in_specs=[pl.BlockSpec((tm, d), lambda i, j: (i, 0)),
                  pl.BlockSpec((2, PEER_HEADS, PEER_N_KEYS, tm), lambda i, j: (0, 0, 0, i)),
                  pl.BlockSpec((2, PEER_HEADS, PEER_N_KEYS, tm), lambda i, j: (0, 0, 0, i)),
                  pl.BlockSpec((te, d), lambda i, j: (j, 0)),
                  pl.BlockSpec((d, te), lambda i, j: (0, j)),
                  pl.BlockSpec((1, d), lambda i, j: (0, 0)),
                  pl.BlockSpec((1, d), lambda i, j: (0, 0))],
        out_specs=pl.BlockSpec((tm, d), lambda i, j: (i, 0)),
        out_shape=jax.ShapeDtypeStruct((n, d), F32),
        scratch_shapes=[pltpu.VMEM((d, tm), F32), pltpu.VMEM((te, tm), BF16)],
        compiler_params=_params("parallel", "arbitrary"),
        name="peer_experts",
    )(x, g1, g2, u, vt, g.reshape(1, d), b.reshape(1, d))


def _rope_tables(seq):
    rows = seq // GRID_W
    row_id = jnp.repeat(jnp.arange(rows, dtype=F32), GRID_W)
    col_id = jnp.tile(jnp.arange(GRID_W, dtype=F32), rows)
    inv_freq = 1.0 / (ROPE_THETA ** (jnp.arange(0, ROPE_AXIS_DIM, 2, dtype=F32) / ROPE_AXIS_DIM))
    ang_r = row_id[:, None] * inv_freq[None, :]
    ang_c = col_id[:, None] * inv_freq[None, :]
    ang = jnp.concatenate([ang_r, ang_r, ang_c, ang_c], axis=-1)
    sign = jnp.where((jnp.arange(HEAD_DIM) % ROPE_AXIS_DIM) < ROPE_HALF, -1.0, 1.0).astype(F32)
    return jnp.cos(ang), jnp.sin(ang) * sign[None, :]


def _head_mean_matrix(heads):
    m = np.kron(np.eye(heads), np.full((HEAD_DIM, HEAD_DIM), 1.0 / HEAD_DIM))
    return jnp.asarray(m, BF16)


TM_LN = 512
TM_IN = 512
DFT1_COLS = 2048
DFT2_K1 = 8
ATTN_TQ = 128
TM_MIX = 256
TM_ROUTE = 256
TM_EXPERT = 512
EXPERT_ROWS_PER_STEP = 4


def kernel(x, ln0_g, ln0_b, w_in, b_gate, q_norm_g, k_norm_g, w_branch_fourier, w_branch_attn, w_out,
           ln1_g, ln1_b, peer_w_query, peer_sub_keys, peer_u, peer_v, ln2_g, ln2_b):
    batch, seq, d = x.shape
    n = batch * seq
    depth = w_in.shape[0]
    alpha = (2.0 * depth) ** 0.25
    n1 = seq // DFT_N2

    cos, sin_signed = _rope_tables(seq)
    cq, sq = jnp.tile(cos, (1, N_Q_HEADS)), jnp.tile(sin_signed, (1, N_Q_HEADS))
    ck, sk = jnp.tile(cos, (1, N_KV_HEADS)), jnp.tile(sin_signed, (1, N_KV_HEADS))
    hmq, hmk = _head_mean_matrix(N_Q_HEADS), _head_mean_matrix(N_KV_HEADS)
    cdft, m1, m3, tw_r, tw_i = _dft_constants(seq)
    tw_r = jnp.repeat(tw_r, D_FOURIER, axis=1)
    tw_i = jnp.repeat(tw_i, D_FOURIER, axis=1)

    xs = _ln_call(x.reshape(n, d), ln0_g, ln0_b, TM_LN)
    for l in range(depth):
        qg = jnp.tile(q_norm_g[l], N_Q_HEADS).reshape(1, D_ATTN)
        kg = jnp.tile(k_norm_g[l], N_KV_HEADS).reshape(1, D_KV)
        zr, zi, q, k, vt, gf, ga = _in_proj_call(
            xs, w_in[l].astype(BF16), b_gate[l], qg, kg, cq, sq, ck, sk, cdft, hmq, hmk, TM_IN, seq)
        a = _dft_stage1_call(zr.reshape(batch, n1, DFT_N2 * D_FOURIER), zi.reshape(batch, n1, DFT_N2 * D_FOURIER),
                             m1, tw_r, tw_i, DFT1_COLS)
        yf = _dft_stage2_call(a.reshape(batch, 2, n1, DFT_N2, D_FOURIER), m3, DFT2_K1)
        ya = _attn_call(q, k, vt, batch, seq, ATTN_TQ)
        xs = _mix_call(yf.reshape(n, D_FOURIER), ya, gf, ga, xs,
                       w_branch_fourier[l].astype(BF16), w_branch_attn[l].astype(BF16), w_out[l].astype(BF16),
                       ln1_g[l], ln1_b[l], TM_MIX, alpha)
        g1, g2 = _peer_route_call(xs, peer_w_query[l].astype(BF16), peer_sub_keys[l].astype(BF16), TM_ROUTE)
        xs = _peer_expert_call(xs, g1, g2, peer_u[l].astype(BF16), peer_v[l].astype(BF16).T,
                               ln2_g[l], ln2_b[l], TM_EXPERT, EXPERT_ROWS_PER_STEP, alpha)
    return xs.reshape(batch, seq, d)
```

```python
import functools
import math

import numpy as np
import jax
import jax.numpy as jnp
from jax import lax
from jax.experimental import pallas as pl
from jax.experimental.pallas import tpu as pltpu

F32 = jnp.float32
BF16 = jnp.bfloat16

D_MODEL = 1024
GRID_W = 64
N_FOURIER_GROUPS = 4
FOURIER_GROUP_DIM = 128
D_FOURIER = N_FOURIER_GROUPS * FOURIER_GROUP_DIM
N_Q_HEADS = 8
N_KV_HEADS = 2
Q_PER_KV = N_Q_HEADS // N_KV_HEADS
HEAD_DIM = 64
D_ATTN = N_Q_HEADS * HEAD_DIM
D_KV = N_KV_HEADS * HEAD_DIM
ROPE_AXIS_DIM = HEAD_DIM // 2
ROPE_HALF = ROPE_AXIS_DIM // 2
ROPE_THETA = 10000.0
PEER_HEADS = 8
PEER_N_KEYS = 128
PEER_TOPK = 16
PEER_HALF = 128
LN_EPS = 1e-5
RMS_EPS = 1e-6

_C_XF = 0
_C_Q = D_FOURIER
_C_K = _C_Q + D_ATTN
_C_V = _C_K + D_KV
_C_GF = _C_V + D_KV
_C_GA = _C_GF + D_MODEL
_C_END = _C_GA + D_MODEL

DFT_N2 = 128

VMEM_LIMIT_BYTES = 56 * 1024 * 1024

def _params(*sem):
    return pltpu.CompilerParams(dimension_semantics=sem, vmem_limit_bytes=VMEM_LIMIT_BYTES)


def _dot(a, b):
    return jnp.dot(a, b, preferred_element_type=F32)


def _dot_nt(a, b):
    return lax.dot_general(a, b, (((1,), (1,)), ((), ())), preferred_element_type=F32)


def _layer_norm(z, g, b):
    mu = jnp.mean(z, axis=-1, keepdims=True)
    zc = z - mu
    var = jnp.mean(zc * zc, axis=-1, keepdims=True)
    return zc * lax.rsqrt(var + LN_EPS) * g + b


def _ln_kernel(x_ref, g_ref, b_ref, o_ref):
    o_ref[...] = _layer_norm(x_ref[...], g_ref[...], b_ref[...])


def _ln_call(x, g, b, tm):
    n, d = x.shape
    return pl.pallas_call(
        _ln_kernel,
        grid=(n // tm,),
        in_specs=[pl.BlockSpec((tm, d), lambda i: (i, 0)),
                  pl.BlockSpec((1, d), lambda i: (0, 0)),
                  pl.BlockSpec((1, d), lambda i: (0, 0))],
        out_specs=pl.BlockSpec((tm, d), lambda i: (i, 0)),
        out_shape=jax.ShapeDtypeStruct((n, d), F32),
        compiler_params=_params("parallel"),
        name="ln0",
    )(x, g.reshape(1, d), b.reshape(1, d))


def _rms_rope(t, gain, head_mean, cos, sin_signed):
    sq = t * t
    hi = sq.astype(BF16)
    lo = (sq - hi.astype(F32)).astype(BF16)
    ms = _dot(hi, head_mean) + _dot(lo, head_mean)
    tn = t * lax.rsqrt(ms + RMS_EPS) * gain
    width = t.shape[-1]
    lane = lax.broadcasted_iota(jnp.int32, tn.shape, 1)
    first_half = (lane % ROPE_AXIS_DIM) < ROPE_HALF
    rot = jnp.where(first_half, pltpu.roll(tn, width - ROPE_HALF, 1), pltpu.roll(tn, ROPE_HALF, 1))
    return tn * cos + rot * sin_signed


def _in_proj_kernel(x_ref, w_ref, bg_ref, qg_ref, kg_ref, cq_ref, sq_ref, ck_ref, sk_ref,
                    cdft_ref, hmq_ref, hmk_ref,
                    zr_ref, zi_ref, q_ref, k_ref, vt_ref, gf_ref, ga_ref):
    xb = x_ref[...].astype(BF16)

    def seg(a, b):
        return _dot(xb, w_ref[:, a:b])

    xf = seg(_C_XF, _C_Q).astype(BF16)
    for g in range(N_FOURIER_GROUPS):
        lo, hi = g * FOURIER_GROUP_DIM, (g + 1) * FOURIER_GROUP_DIM
        z = _dot(xf[:, lo:hi], cdft_ref[...])
        zr_ref[:, lo:hi] = z[:, :FOURIER_GROUP_DIM]
        zi_ref[:, lo:hi] = z[:, FOURIER_GROUP_DIM:]

    q = _rms_rope(seg(_C_Q, _C_K), qg_ref[...], hmq_ref[...], cq_ref[...], sq_ref[...])
    q_ref[...] = (q * (HEAD_DIM ** -0.5)).astype(BF16)
    k = _rms_rope(seg(_C_K, _C_V), kg_ref[...], hmk_ref[...], ck_ref[...], sk_ref[...]).astype(BF16)
    vt = seg(_C_V, _C_GF).T.astype(BF16)
    for j in range(N_KV_HEADS):
        k_ref[j] = k[:, j * HEAD_DIM:(j + 1) * HEAD_DIM]
        vt_ref[j, 0] = vt[j * HEAD_DIM:(j + 1) * HEAD_DIM, :]

    gf_ref[...] = jax.nn.sigmoid(seg(_C_GF, _C_GA) + bg_ref[0:1, :])
    ga_ref[...] = jax.nn.sigmoid(seg(_C_GA, _C_END) + bg_ref[1:2, :])


def _in_proj_call(x, w, bg, qg, kg, cq, sq, ck, sk, cdft, hmq, hmk, tm, seq):
    n, d = x.shape
    pos_blocks = seq // tm
    row = lambda i: (i, 0)
    pos = lambda i: (i % pos_blocks, 0)
    const = lambda i: (0, 0)
    full = lambda a: pl.BlockSpec(a.shape, const)
    return pl.pallas_call(
        _in_proj_kernel,
        grid=(n // tm,),
        in_specs=[pl.BlockSpec((tm, d), row), full(w), full(bg), full(qg), full(kg),
                  pl.BlockSpec((tm, D_ATTN), pos), pl.BlockSpec((tm, D_ATTN), pos),
                  pl.BlockSpec((tm, D_KV), pos), pl.BlockSpec((tm, D_KV), pos),
                  full(cdft), full(hmq), full(hmk)],
        out_specs=[pl.BlockSpec((tm, D_FOURIER), row), pl.BlockSpec((tm, D_FOURIER), row),
                   pl.BlockSpec((tm, D_ATTN), row),
                   pl.BlockSpec((N_KV_HEADS, tm, HEAD_DIM), lambda i: (0, i, 0)),
                   pl.BlockSpec((N_KV_HEADS, 1, HEAD_DIM, tm), lambda i: (0, i, 0, 0)),
                   pl.BlockSpec((tm, d), row), pl.BlockSpec((tm, d), row)],
        out_shape=[jax.ShapeDtypeStruct((n, D_FOURIER), F32), jax.ShapeDtypeStruct((n, D_FOURIER), F32),
                   jax.ShapeDtypeStruct((n, D_ATTN), BF16),
                   jax.ShapeDtypeStruct((N_KV_HEADS, n, HEAD_DIM), BF16),
                   jax.ShapeDtypeStruct((N_KV_HEADS, n // tm, HEAD_DIM, tm), BF16),
                   jax.ShapeDtypeStruct((n, d), F32), jax.ShapeDtypeStruct((n, d), F32)],
        compiler_params=_params("parallel"),
        name="in_proj",
    )(x, w, bg, qg, kg, cq, sq, ck, sk, cdft, hmq, hmk)


def _dft_stage1_kernel(zr_ref, zi_ref, m1_ref, tr_ref, ti_ref, o_ref):
    n1 = zr_ref.shape[1]
    zz = jnp.concatenate([zr_ref[0], zi_ref[0]], axis=0).astype(BF16)
    a = _dot(m1_ref[...], zz)
    ar, ai = a[:n1], a[n1:]
    tr, ti = tr_ref[...], ti_ref[...]
    o_ref[0, 0] = ar * tr - ai * ti
    o_ref[0, 1] = ar * ti + ai * tr


def _dft_stage1_call(zr, zi, m1, tr, ti, cw):
    b, n1, cols = zr.shape
    blk = pl.BlockSpec((1, n1, cw), lambda i, j: (i, 0, j))
    tw = pl.BlockSpec((n1, cw), lambda i, j: (0, j))
    return pl.pallas_call(
        _dft_stage1_kernel,
        grid=(b, cols // cw),
        in_specs=[blk, blk, pl.BlockSpec(m1.shape, lambda i, j: (0, 0)), tw, tw],
        out_specs=pl.BlockSpec((1, 2, n1, cw), lambda i, j: (i, 0, 0, j)),
        out_shape=jax.ShapeDtypeStruct((b, 2, n1, cols), F32),
        compiler_params=_params("parallel", "parallel"),
        name="dft_stage1",
    )(zr, zi, m1, tr, ti)


def _dft_stage2_kernel(a_ref, m3_ref, o_ref):
    kb, n2, c = a_ref.shape[2], a_ref.shape[3], a_ref.shape[4]
    for j in range(kb):
        rhs = jnp.concatenate([a_ref[0, 0, j], a_ref[0, 1, j]], axis=0).astype(BF16)
        o_ref[0, :, j * c:(j + 1) * c] = _dot(m3_ref[...], rhs)


def _dft_stage2_call(a, m3, kb):
    b, _, n1, n2, c = a.shape
    return pl.pallas_call(
        _dft_stage2_kernel,
        grid=(b, n1 // kb),
        in_specs=[pl.BlockSpec((1, 2, kb, n2, c), lambda i, j: (i, 0, j, 0, 0)),
                  pl.BlockSpec(m3.shape, lambda i, j: (0, 0))],
        out_specs=pl.BlockSpec((1, n2, kb * c), lambda i, j: (i, 0, j)),
        out_shape=jax.ShapeDtypeStruct((b, n2, n1 * c), F32),
        compiler_params=_params("parallel", "parallel"),
        name="dft_stage2",
    )(a, m3)


def _dft_constants(seq):
    c = FOURIER_GROUP_DIM
    n1, n2 = seq // DFT_N2, DFT_N2
    jc = np.arange(c)
    ang = 2.0 * np.pi * np.outer(jc, jc) / c
    cdft = np.concatenate([np.cos(ang), -np.sin(ang)], axis=1) / math.sqrt(c)
    j1 = np.arange(n1)
    a1 = 2.0 * np.pi * np.outer(j1, j1) / n1
    fr, fi = np.cos(a1) / math.sqrt(n1), -np.sin(a1) / math.sqrt(n1)
    m1 = np.block([[fr, -fi], [fi, fr]])
    j2 = np.arange(n2)
    a2 = 2.0 * np.pi * np.outer(j2, j2) / n2
    m3 = np.concatenate([np.cos(a2), np.sin(a2)], axis=1) / math.sqrt(n2)
    at = 2.0 * np.pi * np.outer(j1, j2) / seq
    return (jnp.asarray(cdft, BF16), jnp.asarray(m1, BF16), jnp.asarray(m3, BF16),
            jnp.asarray(np.cos(at), F32), jnp.asarray(-np.sin(at), F32))


def _attn_kernel(q_ref, k_ref, vt_ref, o_ref, qs_ref, st0_ref, st1_ref, p0_ref, p1_ref, acc_ref, *, tq):
    n_chunks, _, tk = vt_ref.shape[1:]
    assert n_chunks % 2 == 0 and n_chunks >= 4
    nq = Q_PER_KV * tq
    for g in range(Q_PER_KV):
        qs_ref[g * tq:(g + 1) * tq, :] = q_ref[:, g * HEAD_DIM:(g + 1) * HEAD_DIM]
    acc_ref[...] = jnp.zeros_like(acc_ref)

    def scores(c, st_ref):
        start = pl.multiple_of(c * tk, tk)
        st_ref[...] = _dot_nt(k_ref[0, pl.ds(start, tk), :], qs_ref[...])

    def softmax(st_ref, p_ref, m, l):
        st = st_ref[...]
        m_new = jnp.maximum(m, jnp.max(st, axis=0, keepdims=True))
        alpha = jnp.exp(m - m_new)
        pt = jnp.exp(st - m_new)
        p_ref[...] = pt.astype(BF16)
        return m_new, alpha * l + jnp.sum(pt, axis=0, keepdims=True), alpha

    def values(c, p_ref, alpha):
        acc_ref[...] = alpha * acc_ref[...] + _dot(vt_ref[0, c], p_ref[...])

    def step(i, carry, st_cur, st_nxt, p_cur, p_prv, last=False):
        m, l, alpha_prv = carry
        values(i - 1, p_prv, alpha_prv)
        m, l, alpha = softmax(st_cur, p_cur, m, l)
        if not last:
            scores(i + 1, st_nxt)
        return m, l, alpha

    scores(0, st0_ref)
    m, l, alpha = softmax(st0_ref, p0_ref, jnp.full((1, nq), -jnp.inf, F32), jnp.zeros((1, nq), F32))
    scores(1, st1_ref)

    def body(ii, carry):
        i = 2 * ii + 1
        carry = step(i, carry, st1_ref, st0_ref, p1_ref, p0_ref)
        return step(i + 1, carry, st0_ref, st1_ref, p0_ref, p1_ref)

    carry = lax.fori_loop(0, n_chunks // 2 - 1, body, (m, l, alpha))
    m, l, alpha = step(n_chunks - 1, carry, st1_ref, st0_ref, p1_ref, p0_ref, last=True)
    values(n_chunks - 1, p1_ref, alpha)

    ot = acc_ref[...] / l
    o = jnp.concatenate([ot, jnp.zeros_like(ot)], axis=0).T
    o_ref[...] = jnp.concatenate([o[g * tq:(g + 1) * tq, :HEAD_DIM] for g in range(Q_PER_KV)], axis=1)


def _attn_call(q, k, vt, batch, seq, tq):
    n = q.shape[0]
    qblocks = seq // tq
    gw = Q_PER_KV * HEAD_DIM
    tk = vt.shape[-1]
    chunks = seq // tk
    nq = Q_PER_KV * tq
    return pl.pallas_call(
        functools.partial(_attn_kernel, tq=tq),
        grid=(batch, N_KV_HEADS, qblocks),
        in_specs=[pl.BlockSpec((tq, gw), lambda b, j, i: (b * qblocks + i, j)),
                  pl.BlockSpec((1, seq, HEAD_DIM), lambda b, j, i: (j, b, 0)),
                  pl.BlockSpec((1, chunks, HEAD_DIM, tk), lambda b, j, i: (j, b, 0, 0))],
        out_specs=pl.BlockSpec((tq, gw), lambda b, j, i: (b * qblocks + i, j)),
        out_shape=jax.ShapeDtypeStruct((n, D_ATTN), F32),
        scratch_shapes=[pltpu.VMEM((nq, HEAD_DIM), BF16),
                        pltpu.VMEM((tk, nq), F32), pltpu.VMEM((tk, nq), F32),
                        pltpu.VMEM((tk, nq), BF16), pltpu.VMEM((tk, nq), BF16),
                        pltpu.VMEM((HEAD_DIM, nq), F32)],
        compiler_params=_params("parallel", "parallel", "parallel"),
        name="attention",
    )(q, k, vt)


def _mix_kernel(yf_ref, ya_ref, gf_ref, ga_ref, x_ref, wf_ref, wa_ref, wo_ref, g_ref, b_ref, o_ref, *, alpha):
    mixed = (gf_ref[...] * _dot(yf_ref[...].astype(BF16), wf_ref[...])
             + ga_ref[...] * _dot(ya_ref[...].astype(BF16), wa_ref[...]))
    z = alpha * x_ref[...] + _dot(mixed.astype(BF16), wo_ref[...])
    o_ref[...] = _layer_norm(z, g_ref[...], b_ref[...])


def _mix_call(yf, ya, gf, ga, x, wf, wa, wo, g, b, tm, alpha):
    n, d = x.shape
    row = lambda i: (i, 0)
    const = lambda i: (0, 0)
    return pl.pallas_call(
        functools.partial(_mix_kernel, alpha=alpha),
        grid=(n // tm,),
        in_specs=[pl.BlockSpec((tm, D_FOURIER), row), pl.BlockSpec((tm, D_ATTN), row),
                  pl.BlockSpec((tm, d), row), pl.BlockSpec((tm, d), row), pl.BlockSpec((tm, d), row),
                  pl.BlockSpec(wf.shape, const), pl.BlockSpec(wa.shape, const), pl.BlockSpec(wo.shape, const),
                  pl.BlockSpec((1, d), const), pl.BlockSpec((1, d), const)],
        out_specs=pl.BlockSpec((tm, d), row),
        out_shape=jax.ShapeDtypeStruct((n, d), F32),
        compiler_params=_params("parallel"),
        name="mix",
    )(yf, ya, gf, ga, x, wf, wa, wo, g.reshape(1, d), b.reshape(1, d))


_NEG_INF = float("-inf")


def _extract_top(s, order, count, exact):
    rank = jnp.full(s.shape, float(count), F32)
    big = float(2 ** 20)
    vals = []
    for a in range(count):
        m = jnp.max(s, axis=0, keepdims=True)
        if exact:
            first = jnp.min(jnp.where(s == m, order, big), axis=0, keepdims=True)
            sel = order == first
        else:
            sel = s == m
        rank = jnp.where(sel, float(a), rank)
        s = jnp.where(sel, _NEG_INF, s)
        vals.append(m)
    extracted = jnp.sum(jnp.where(rank < float(count), 1.0, 0.0), axis=0, keepdims=True)
    return vals, rank, jnp.where(extracted == float(count), 0.0, 1.0)


def _candidate_rows(v1, v2, combine):
    v2_all = jnp.concatenate(v2, axis=0)
    v2_lo = v2_all[:8]
    pieces = [combine(v1[0], v2_all)]
    pieces += [combine(v1[a], v2_lo) for a in range(1, 8)]
    pieces.append(combine(jnp.concatenate(v1[8:], axis=0), v2[0]))
    return jnp.concatenate(pieces, axis=0)


def _candidate_positions(tokens):
    pos = ([b for b in range(16)] + [a * 16 + b for a in range(1, 8) for b in range(8)]
           + [a * 16 for a in range(8, 16)])
    col = jnp.asarray(np.asarray(pos, np.float32).reshape(-1, 1))
    return jnp.broadcast_to(col, (len(pos), tokens))


def _peer_route_kernel(x_ref, wq_ref, keys_ref, pos_ref, g1_ref, g2_ref):
    ambiguous = _route_tile(x_ref, wq_ref, keys_ref, pos_ref, g1_ref, g2_ref, exact=False)

    @pl.when(jnp.max(ambiguous) > 0.0)
    def _():
        _route_tile(x_ref, wq_ref, keys_ref, pos_ref, g1_ref, g2_ref, exact=True)


def _route_tile(x_ref, wq_ref, keys_ref, pos_ref, g1_ref, g2_ref, exact):
    tokens = x_ref.shape[0]
    xb = x_ref[...].astype(BF16)
    key_order = lax.broadcasted_iota(jnp.int32, (PEER_N_KEYS, tokens), 0).astype(F32)
    cand_pos = pos_ref[...]
    ambiguous = jnp.zeros((1, tokens), F32)
    for h in range(PEER_HEADS):
        scores, vals, ranks = [], [], []
        for c in range(2):
            col = (2 * h + c) * PEER_HALF
            qp = _dot(xb, wq_ref[:, col:col + PEER_HALF]).astype(BF16)
            s = _dot_nt(keys_ref[h, c], qp)
            v, r, amb = _extract_top(s, key_order, PEER_TOPK, exact)
            ambiguous = jnp.maximum(ambiguous, amb)
            scores.append(s)
            vals.append(v)
            ranks.append(r)
        cand = _candidate_rows(vals[0], vals[1], lambda a, b: a + b)
        _, crank, amb = _extract_top(cand, cand_pos, PEER_TOPK, exact)
        ambiguous = jnp.maximum(ambiguous, amb)
        chosen = crank < float(PEER_TOPK)
        e1 = [jnp.exp(v - vals[0][0]) for v in vals[0]]
        e2 = [jnp.exp(v - vals[1][0]) for v in vals[1]]
        ecand = _candidate_rows(e1, e2, lambda a, b: a * b)
        z = jnp.sum(jnp.where(chosen, ecand, 0.0), axis=0, keepdims=True)
        cnt = jnp.where(chosen, 1.0, 0.0)
        limit = [jnp.sum(cnt[0:16], axis=0, keepdims=True)]
        limit += [jnp.sum(cnt[8 + 8 * a:16 + 8 * a], axis=0, keepdims=True) for a in range(1, 8)]
        limit += [cnt[72 + a:73 + a] for a in range(8)]
        lim_dense = jnp.zeros((PEER_N_KEYS, tokens), F32)
        for a in range(PEER_TOPK):
            lim_dense = jnp.where(ranks[0] == float(a), limit[a], lim_dense)
        g1_ref[0, h] = jnp.exp(scores[0] - vals[0][0]) / z
        g1_ref[1, h] = lim_dense
        g2_ref[0, h] = jnp.exp(scores[1] - vals[1][0]).astype(BF16)
        g2_ref[1, h] = ranks[1].astype(BF16)
    return ambiguous


def _peer_route_call(x, wq, keys, tm):
    n, d = x.shape
    pos = _candidate_positions(tm)
    return pl.pallas_call(
        _peer_route_kernel,
        grid=(n // tm,),
        in_specs=[pl.BlockSpec((tm, d), lambda i: (i, 0)),
                  pl.BlockSpec(wq.shape, lambda i: (0, 0)),
                  pl.BlockSpec(keys.shape, lambda i: (0, 0, 0, 0)),
                  pl.BlockSpec(pos.shape, lambda i: (0, 0))],
        out_specs=[pl.BlockSpec((2, PEER_HEADS, PEER_N_KEYS, tm), lambda i: (0, 0, 0, i)),
                   pl.BlockSpec((2, PEER_HEADS, PEER_N_KEYS, tm), lambda i: (0, 0, 0, i))],
        out_shape=[jax.ShapeDtypeStruct((2, PEER_HEADS, PEER_N_KEYS, n), F32),
                   jax.ShapeDtypeStruct((2, PEER_HEADS, PEER_N_KEYS, n), BF16)],
        compiler_params=_params("parallel"),
        name="peer_route",
    )(x, wq, keys, pos)


def _peer_expert_kernel(x_ref, g1p_ref, g1c_ref, g2_ref, u_ref, vt_ref, g_ref, b_ref, o_ref,
                        xb_ref, act0_ref, act1_ref, p0_ref, p1_ref, acc_ref, *, alpha, rows_per_chunk):
    s = pl.program_id(1)
    last = pl.num_programs(1) - 1
    sub = rows_per_chunk * PEER_N_KEYS

    @pl.when(s == 0)
    def _():
        xb_ref[...] = x_ref[...].astype(BF16)
        acc_ref[...] = jnp.zeros_like(acc_ref)
        act1_ref[...] = jnp.zeros_like(act1_ref)
        p0_ref[...] = jnp.zeros_like(p0_ref)

    def activations(half, act_ref):
        act_ref[...] = _dot_nt(u_ref[half * sub:(half + 1) * sub, :], xb_ref[...])

    def weights(g1_ref, first_row, act_ref, p_ref):
        for r in range(rows_per_chunk):
            i1 = first_row + r
            gate = None
            for h in range(PEER_HEADS):
                e1n = g1_ref[0, h, i1:i1 + 1, :].astype(BF16)
                lim = g1_ref[1, h, i1:i1 + 1, :].astype(BF16)
                term = jnp.where(g2_ref[1, h] < lim, g2_ref[0, h] * e1n, jnp.zeros((), BF16))
                gate = term if gate is None else gate + term
            lo, hi = r * PEER_N_KEYS, (r + 1) * PEER_N_KEYS
            a = act_ref[lo:hi, :]
            gelu = 0.5 * a * (1.0 + lax.erf(a * math.sqrt(0.5)))
            p_ref[lo:hi, :] = gate * gelu.astype(BF16)

    def outputs(half, p_ref):
        acc_ref[...] += _dot(vt_ref[:, half * sub:(half + 1) * sub], p_ref[...])

    activations(0, act0_ref)
    weights(g1p_ref, rows_per_chunk, act1_ref, p1_ref)
    outputs(0, p0_ref)

    @pl.when(s >= 0)
    def _():
        outputs(1, p1_ref)
        weights(g1c_ref, 0, act0_ref, p0_ref)
        activations(1, act1_ref)

    @pl.when(s == last)
    def _():
        z = alpha * x_ref[...] + acc_ref[...].T
        o_ref[...] = _layer_norm(z, g_ref[...], b_ref[...])


def _peer_expert_call(x, g1, g2, u, vt, g, b, tm, rows_per_chunk, alpha):
    n, d = x.shape
    sub = rows_per_chunk * PEER_N_KEYS
    n_exp = u.shape[0]
    steps = n_exp // (2 * sub)
    return pl.pallas_call(
        functools.partial(_peer_expert_kernel, alpha=alpha, rows_per_chunk=rows_per_chunk),
        grid=(n // tm, steps + 1),
        in_specs=[pl.BlockSpec((tm, d), lambda i, s: (i, 0)),
                  pl.BlockSpec((2, PEER_HEADS, 2 * rows_per_chunk, tm),
                               lambda i, s: (0, 0, jnp.maximum(s - 1, 0), i)),
                  pl.BlockSpec((2, PEER_HEADS, 2 * rows_per_chunk, tm),
                               lambda i, s: (0, 0, jnp.minimum(s, steps - 1), i)),
                  pl.BlockSpec((2, PEER_HEADS, PEER_N_KEYS, tm), lambda i, s: (0, 0, 0, i)),
                  pl.BlockSpec((2 * sub, d), lambda i, s: (jnp.minimum(s, steps - 1), 0)),
                  pl.BlockSpec((d, 2 * sub), lambda i, s: (0, jnp.maximum(s - 1, 0))),
                  pl.BlockSpec((1, d), lambda i, s: (0, 0)),
                  pl.BlockSpec((1, d), lambda i, s: (0, 0))],
        out_specs=pl.BlockSpec((tm, d), lambda i, s: (i, 0)),
        out_shape=jax.ShapeDtypeStruct((n, d), F32),
        scratch_shapes=[pltpu.VMEM((tm, d), BF16),
                        pltpu.VMEM((sub, tm), F32), pltpu.VMEM((sub, tm), F32),
                        pltpu.VMEM((sub, tm), BF16), pltpu.VMEM((sub, tm), BF16),
                        pltpu.VMEM((d, tm), F32)],
        compiler_params=_params("parallel", "arbitrary"),
        name="peer_experts",
    )(x, g1, g1, g2, u, vt, g.reshape(1, d), b.reshape(1, d))


def _rope_tables(seq):
    rows = seq // GRID_W
    row_id = jnp.repeat(jnp.arange(rows, dtype=F32), GRID_W)
    col_id = jnp.tile(jnp.arange(GRID_W, dtype=F32), rows)
    inv_freq = 1.0 / (ROPE_THETA ** (jnp.arange(0, ROPE_AXIS_DIM, 2, dtype=F32) / ROPE_AXIS_DIM))
    ang_r = row_id[:, None] * inv_freq[None, :]
    ang_c = col_id[:, None] * inv_freq[None, :]
    ang = jnp.concatenate([ang_r, ang_r, ang_c, ang_c], axis=-1)
    sign = jnp.where((jnp.arange(HEAD_DIM) % ROPE_AXIS_DIM) < ROPE_HALF, -1.0, 1.0).astype(F32)
    return jnp.cos(ang), jnp.sin(ang) * sign[None, :]


def _head_mean_matrix(heads):
    m = np.kron(np.eye(heads), np.full((HEAD_DIM, HEAD_DIM), 1.0 / HEAD_DIM))
    return jnp.asarray(m, BF16)


TM_LN = 512
TM_IN = 512
DFT1_COLS = 2048
DFT2_K1 = 8
ATTN_TQ = 128
TM_MIX = 256
TM_ROUTE = 256
TM_EXPERT = 512
EXPERT_ROWS_PER_STEP = 4


def kernel(x, ln0_g, ln0_b, w_in, b_gate, q_norm_g, k_norm_g, w_branch_fourier, w_branch_attn, w_out,
           ln1_g, ln1_b, peer_w_query, peer_sub_keys, peer_u, peer_v, ln2_g, ln2_b):
    batch, seq, d = x.shape
    n = batch * seq
    depth = w_in.shape[0]
    alpha = (2.0 * depth) ** 0.25
    n1 = seq // DFT_N2

    cos, sin_signed = _rope_tables(seq)
    cq, sq = jnp.tile(cos, (1, N_Q_HEADS)), jnp.tile(sin_signed, (1, N_Q_HEADS))
    ck, sk = jnp.tile(cos, (1, N_KV_HEADS)), jnp.tile(sin_signed, (1, N_KV_HEADS))
    hmq, hmk = _head_mean_matrix(N_Q_HEADS), _head_mean_matrix(N_KV_HEADS)
    cdft, m1, m3, tw_r, tw_i = _dft_constants(seq)
    tw_r = jnp.repeat(tw_r, D_FOURIER, axis=1)
    tw_i = jnp.repeat(tw_i, D_FOURIER, axis=1)

    xs = _ln_call(x.reshape(n, d), ln0_g, ln0_b, TM_LN)
    for l in range(depth):
        qg = jnp.tile(q_norm_g[l], N_Q_HEADS).reshape(1, D_ATTN)
        kg = jnp.tile(k_norm_g[l], N_KV_HEADS).reshape(1, D_KV)
        zr, zi, q, k, vt, gf, ga = _in_proj_call(
            xs, w_in[l].astype(BF16), b_gate[l], qg, kg, cq, sq, ck, sk, cdft, hmq, hmk, TM_IN, seq)
        a = _dft_stage1_call(zr.reshape(batch, n1, DFT_N2 * D_FOURIER), zi.reshape(batch, n1, DFT_N2 * D_FOURIER),
                             m1, tw_r, tw_i, DFT1_COLS)
        yf = _dft_stage2_call(a.reshape(batch, 2, n1, DFT_N2, D_FOURIER), m3, DFT2_K1)
        ya = _attn_call(q, k, vt, batch, seq, ATTN_TQ)
        xs = _mix_call(yf.reshape(n, D_FOURIER), ya, gf, ga, xs,
                       w_branch_fourier[l].astype(BF16), w_branch_attn[l].astype(BF16), w_out[l].astype(BF16),
                       ln1_g[l], ln1_b[l], TM_MIX, alpha)
        g1, g2 = _peer_route_call(xs, peer_w_query[l].astype(BF16), peer_sub_keys[l].astype(BF16), TM_ROUTE)
        xs = _peer_expert_call(xs, g1, g2, peer_u[l].astype(BF16), peer_v[l].astype(BF16).T,
                               ln2_g[l], ln2_b[l], TM_EXPERT, EXPERT_ROWS_PER_STEP, alpha)
    return xs.reshape(batch, seq, d)
```

```python
import functools
import math

import numpy as np
import jax
import jax.numpy as jnp
from jax import lax
from jax.experimental import pallas as pl
from jax.experimental.pallas import tpu as pltpu

F32 = jnp.float32
BF16 = jnp.bfloat16

D_MODEL = 1024
GRID_W = 64
N_FOURIER_GROUPS = 4
FOURIER_GROUP_DIM = 128
D_FOURIER = N_FOURIER_GROUPS * FOURIER_GROUP_DIM
N_Q_HEADS = 8
N_KV_HEADS = 2
Q_PER_KV = N_Q_HEADS // N_KV_HEADS
HEAD_DIM = 64
D_ATTN = N_Q_HEADS * HEAD_DIM
D_KV = N_KV_HEADS * HEAD_DIM
ROPE_AXIS_DIM = HEAD_DIM // 2
ROPE_HALF = ROPE_AXIS_DIM // 2
ROPE_THETA = 10000.0
PEER_HEADS = 8
PEER_N_KEYS = 128
PEER_TOPK = 16
PEER_HALF = 128
LN_EPS = 1e-5
RMS_EPS = 1e-6

_C_XF = 0
_C_Q = D_FOURIER
_C_K = _C_Q + D_ATTN
_C_V = _C_K + D_KV
_C_GF = _C_V + D_KV
_C_GA = _C_GF + D_MODEL
_C_END = _C_GA + D_MODEL

DFT_N2 = 128

VMEM_LIMIT_BYTES = 56 * 1024 * 1024
def _params(*sem):
    return pltpu.CompilerParams(dimension_semantics=sem, vmem_limit_bytes=VMEM_LIMIT_BYTES)


def _dot(a, b):
    return jnp.dot(a, b, preferred_element_type=F32)


def _dot_nt(a, b):
    return lax.dot_general(a, b, (((1,), (1,)), ((), ())), preferred_element_type=F32)


def _layer_norm(z, g, b):
    mu = jnp.mean(z, axis=-1, keepdims=True)
    zc = z - mu
    var = jnp.mean(zc * zc, axis=-1, keepdims=True)
    return zc * lax.rsqrt(var + LN_EPS) * g + b


def _ln_kernel(x_ref, g_ref, b_ref, o_ref):
    o_ref[...] = _layer_norm(x_ref[...], g_ref[...], b_ref[...])


def _ln_call(x, g, b, tm):
    n, d = x.shape
    return pl.pallas_call(
        _ln_kernel,
        grid=(n // tm,),
        in_specs=[pl.BlockSpec((tm, d), lambda i: (i, 0)),
                  pl.BlockSpec((1, d), lambda i: (0, 0)),
                  pl.BlockSpec((1, d), lambda i: (0, 0))],
        out_specs=pl.BlockSpec((tm, d), lambda i: (i, 0)),
        out_shape=jax.ShapeDtypeStruct((n, d), F32),
        compiler_params=_params("parallel"),
        name="ln0",
    )(x, g.reshape(1, d), b.reshape(1, d))


def _rms_rope(t, gain, head_mean, cos, sin_signed):
    sq = t * t
    hi = sq.astype(BF16)
    lo = (sq - hi.astype(F32)).astype(BF16)
    ms = _dot(hi, head_mean) + _dot(lo, head_mean)
    tn = t * lax.rsqrt(ms + RMS_EPS) * gain
    width = t.shape[-1]
    lane = lax.broadcasted_iota(jnp.int32, tn.shape, 1)
    first_half = (lane % ROPE_AXIS_DIM) < ROPE_HALF
    rot = jnp.where(first_half, pltpu.roll(tn, width - ROPE_HALF, 1), pltpu.roll(tn, ROPE_HALF, 1))
    return tn * cos + rot * sin_signed


def _in_proj_kernel(x_ref, w_ref, bg_ref, qg_ref, kg_ref, cq_ref, sq_ref, ck_ref, sk_ref,
                    cdft_ref, hmq_ref, hmk_ref,
                    zr_ref, zi_ref, q_ref, k_ref, vt_ref, gf_ref, ga_ref):
    xb = x_ref[...].astype(BF16)

    def seg(a, b):
        return _dot(xb, w_ref[:, a:b])

    xf = seg(_C_XF, _C_Q).astype(BF16)
    for g in range(N_FOURIER_GROUPS):
        lo, hi = g * FOURIER_GROUP_DIM, (g + 1) * FOURIER_GROUP_DIM
        z = _dot(xf[:, lo:hi], cdft_ref[...])
        zr_ref[:, lo:hi] = z[:, :FOURIER_GROUP_DIM]
        zi_ref[:, lo:hi] = z[:, FOURIER_GROUP_DIM:]

    q = _rms_rope(seg(_C_Q, _C_K), qg_ref[...], hmq_ref[...], cq_ref[...], sq_ref[...])
    q_ref[...] = (q * (HEAD_DIM ** -0.5)).astype(BF16)
    k = _rms_rope(seg(_C_K, _C_V), kg_ref[...], hmk_ref[...], ck_ref[...], sk_ref[...]).astype(BF16)
    vt = seg(_C_V, _C_GF).T.astype(BF16)
    for j in range(N_KV_HEADS):
        k_ref[j] = k[:, j * HEAD_DIM:(j + 1) * HEAD_DIM]
        vt_ref[j, 0] = vt[j * HEAD_DIM:(j + 1) * HEAD_DIM, :]

    gf_ref[...] = jax.nn.sigmoid(seg(_C_GF, _C_GA) + bg_ref[0:1, :])
    ga_ref[...] = jax.nn.sigmoid(seg(_C_GA, _C_END) + bg_ref[1:2, :])


def _in_proj_call(x, w, bg, qg, kg, cq, sq, ck, sk, cdft, hmq, hmk, tm, seq):
    n, d = x.shape
    pos_blocks = seq // tm
    row = lambda i: (i, 0)
    pos = lambda i: (i % pos_blocks, 0)
    const = lambda i: (0, 0)
    full = lambda a: pl.BlockSpec(a.shape, const)
    return pl.pallas_call(
        _in_proj_kernel,
        grid=(n // tm,),
        in_specs=[pl.BlockSpec((tm, d), row), full(w), full(bg), full(qg), full(kg),
                  pl.BlockSpec((tm, D_ATTN), pos), pl.BlockSpec((tm, D_ATTN), pos),
                  pl.BlockSpec((tm, D_KV), pos), pl.BlockSpec((tm, D_KV), pos),
                  full(cdft), full(hmq), full(hmk)],
        out_specs=[pl.BlockSpec((tm, D_FOURIER), row), pl.BlockSpec((tm, D_FOURIER), row),
                   pl.BlockSpec((tm, D_ATTN), row),
                   pl.BlockSpec((N_KV_HEADS, tm, HEAD_DIM), lambda i: (0, i, 0)),
                   pl.BlockSpec((N_KV_HEADS, 1, HEAD_DIM, tm), lambda i: (0, i, 0, 0)),
                   pl.BlockSpec((tm, d), row), pl.BlockSpec((tm, d), row)],
        out_shape=[jax.ShapeDtypeStruct((n, D_FOURIER), F32), jax.ShapeDtypeStruct((n, D_FOURIER), F32),
                   jax.ShapeDtypeStruct((n, D_ATTN), BF16),
                   jax.ShapeDtypeStruct((N_KV_HEADS, n, HEAD_DIM), BF16),
                   jax.ShapeDtypeStruct((N_KV_HEADS, n // tm, HEAD_DIM, tm), BF16),
                   jax.ShapeDtypeStruct((n, d), F32), jax.ShapeDtypeStruct((n, d), F32)],
        compiler_params=_params("parallel"),
        name="in_proj",
    )(x, w, bg, qg, kg, cq, sq, ck, sk, cdft, hmq, hmk)


def _dft_stage1_kernel(zr_ref, zi_ref, m1_ref, tr_ref, ti_ref, o_ref):
    n1 = zr_ref.shape[1]
    zz = jnp.concatenate([zr_ref[0], zi_ref[0]], axis=0).astype(BF16)
    a = _dot(m1_ref[...], zz)
    ar, ai = a[:n1], a[n1:]
    tr, ti = tr_ref[...], ti_ref[...]
    o_ref[0, 0] = ar * tr - ai * ti
    o_ref[0, 1] = ar * ti + ai * tr


def _dft_stage1_call(zr, zi, m1, tr, ti, cw):
    b, n1, cols = zr.shape
    blk = pl.BlockSpec((1, n1, cw), lambda i, j: (i, 0, j))
    tw = pl.BlockSpec((n1, cw), lambda i, j: (0, j))
    return pl.pallas_call(
        _dft_stage1_kernel,
        grid=(b, cols // cw),
        in_specs=[blk, blk, pl.BlockSpec(m1.shape, lambda i, j: (0, 0)), tw, tw],
        out_specs=pl.BlockSpec((1, 2, n1, cw), lambda i, j: (i, 0, 0, j)),
        out_shape=jax.ShapeDtypeStruct((b, 2, n1, cols), F32),
        compiler_params=_params("parallel", "parallel"),
        name="dft_stage1",
    )(zr, zi, m1, tr, ti)


def _dft_stage2_kernel(a_ref, m3_ref, o_ref):
    kb, n2, c = a_ref.shape[2], a_ref.shape[3], a_ref.shape[4]
    for j in range(kb):
        rhs = jnp.concatenate([a_ref[0, 0, j], a_ref[0, 1, j]], axis=0).astype(BF16)
        o_ref[0, :, j * c:(j + 1) * c] = _dot(m3_ref[...], rhs)


def _dft_stage2_call(a, m3, kb):
    b, _, n1, n2, c = a.shape
    return pl.pallas_call(
        _dft_stage2_kernel,
        grid=(b, n1 // kb),
        in_specs=[pl.BlockSpec((1, 2, kb, n2, c), lambda i, j: (i, 0, j, 0, 0)),
                  pl.BlockSpec(m3.shape, lambda i, j: (0, 0))],
        out_specs=pl.BlockSpec((1, n2, kb * c), lambda i, j: (i, 0, j)),
        out_shape=jax.ShapeDtypeStruct((b, n2, n1 * c), F32),
        compiler_params=_params("parallel", "parallel"),
        name="dft_stage2",
    )(a, m3)


def _dft_constants(seq):
    c = FOURIER_GROUP_DIM
    n1, n2 = seq // DFT_N2, DFT_N2
    jc = np.arange(c)
    ang = 2.0 * np.pi * np.outer(jc, jc) / c
    cdft = np.concatenate([np.cos(ang), -np.sin(ang)], axis=1) / math.sqrt(c)
    j1 = np.arange(n1)
    a1 = 2.0 * np.pi * np.outer(j1, j1) / n1
    fr, fi = np.cos(a1) / math.sqrt(n1), -np.sin(a1) / math.sqrt(n1)
    m1 = np.block([[fr, -fi], [fi, fr]])
    j2 = np.arange(n2)
    a2 = 2.0 * np.pi * np.outer(j2, j2) / n2
    m3 = np.concatenate([np.cos(a2), np.sin(a2)], axis=1) / math.sqrt(n2)
    at = 2.0 * np.pi * np.outer(j1, j2) / seq
    return (jnp.asarray(cdft, BF16), jnp.asarray(m1, BF16), jnp.asarray(m3, BF16),
            jnp.asarray(np.cos(at), F32), jnp.asarray(-np.sin(at), F32))


def _attn_kernel(q_ref, k_ref, vt_ref, o_ref, qs_ref, st0_ref, st1_ref, p0_ref, p1_ref, acc_ref, *, tq):
    n_chunks, _, tk = vt_ref.shape[1:]
    assert n_chunks % 2 == 0 and n_chunks >= 4
    nq = Q_PER_KV * tq
    for g in range(Q_PER_KV):
        qs_ref[g * tq:(g + 1) * tq, :] = q_ref[:, g * HEAD_DIM:(g + 1) * HEAD_DIM]
    acc_ref[...] = jnp.zeros_like(acc_ref)

    def scores(c, st_ref):
        start = pl.multiple_of(c * tk, tk)
        st_ref[...] = _dot_nt(k_ref[0, pl.ds(start, tk), :], qs_ref[...])

    def softmax(st_ref, p_ref, m, l):
        st = st_ref[...]
        m_new = jnp.maximum(m, jnp.max(st, axis=0, keepdims=True))
        alpha = jnp.exp(m - m_new)
        pt = jnp.exp(st - m_new)
        p_ref[...] = pt.astype(BF16)
        return m_new, alpha * l + jnp.sum(pt, axis=0, keepdims=True), alpha

    def values(c, p_ref, alpha):
        acc_ref[...] = alpha * acc_ref[...] + _dot(vt_ref[0, c], p_ref[...])

    def step(i, carry, st_cur, st_nxt, p_cur, p_prv, last=False):
        m, l, alpha_prv = carry
        values(i - 1, p_prv, alpha_prv)
        m, l, alpha = softmax(st_cur, p_cur, m, l)
        if not last:
            scores(i + 1, st_nxt)
        return m, l, alpha

    scores(0, st0_ref)
    m, l, alpha = softmax(st0_ref, p0_ref, jnp.full((1, nq), -jnp.inf, F32), jnp.zeros((1, nq), F32))
    scores(1, st1_ref)

    def body(ii, carry):
        i = 2 * ii + 1
        carry = step(i, carry, st1_ref, st0_ref, p1_ref, p0_ref)
        return step(i + 1, carry, st0_ref, st1_ref, p0_ref, p1_ref)

    carry = lax.fori_loop(0, n_chunks // 2 - 1, body, (m, l, alpha))
    m, l, alpha = step(n_chunks - 1, carry, st1_ref, st0_ref, p1_ref, p0_ref, last=True)
    values(n_chunks - 1, p1_ref, alpha)

    ot = acc_ref[...] / l
    o = jnp.concatenate([ot, jnp.zeros_like(ot)], axis=0).T
    o_ref[...] = jnp.concatenate([o[g * tq:(g + 1) * tq, :HEAD_DIM] for g in range(Q_PER_KV)], axis=1)


def _attn_call(q, k, vt, batch, seq, tq):
    n = q.shape[0]
    qblocks = seq // tq
    gw = Q_PER_KV * HEAD_DIM
    tk = vt.shape[-1]
    chunks = seq // tk
    nq = Q_PER_KV * tq
    return pl.pallas_call(
        functools.partial(_attn_kernel, tq=tq),
        grid=(batch, N_KV_HEADS, qblocks),
        in_specs=[pl.BlockSpec((tq, gw), lambda b, j, i: (b * qblocks + i, j)),
                  pl.BlockSpec((1, seq, HEAD_DIM), lambda b, j, i: (j, b, 0)),
                  pl.BlockSpec((1, chunks, HEAD_DIM, tk), lambda b, j, i: (j, b, 0, 0))],
        out_specs=pl.BlockSpec((tq, gw), lambda b, j, i: (b * qblocks + i, j)),
        out_shape=jax.ShapeDtypeStruct((n, D_ATTN), F32),
        scratch_shapes=[pltpu.VMEM((nq, HEAD_DIM), BF16),
                        pltpu.VMEM((tk, nq), F32), pltpu.VMEM((tk, nq), F32),
                        pltpu.VMEM((tk, nq), BF16), pltpu.VMEM((tk, nq), BF16),
                        pltpu.VMEM((HEAD_DIM, nq), F32)],
        compiler_params=_params("parallel", "parallel", "parallel"),
        name="attention",
    )(q, k, vt)


def _mix_kernel(yf_ref, ya_ref, gf_ref, ga_ref, x_ref, wf_ref, wa_ref, wo_ref, g_ref, b_ref, o_ref, *, alpha):
    mixed = (gf_ref[...] * _dot(yf_ref[...].astype(BF16), wf_ref[...])
             + ga_ref[...] * _dot(ya_ref[...].astype(BF16), wa_ref[...]))
    z = alpha * x_ref[...] + _dot(mixed.astype(BF16), wo_ref[...])
    o_ref[...] = _layer_norm(z, g_ref[...], b_ref[...])


def _mix_call(yf, ya, gf, ga, x, wf, wa, wo, g, b, tm, alpha):
    n, d = x.shape
    row = lambda i: (i, 0)
    const = lambda i: (0, 0)
    return pl.pallas_call(
        functools.partial(_mix_kernel, alpha=alpha),
        grid=(n // tm,),
        in_specs=[pl.BlockSpec((tm, D_FOURIER), row), pl.BlockSpec((tm, D_ATTN), row),
                  pl.BlockSpec((tm, d), row), pl.BlockSpec((tm, d), row), pl.BlockSpec((tm, d), row),
                  pl.BlockSpec(wf.shape, const), pl.BlockSpec(wa.shape, const), pl.BlockSpec(wo.shape, const),
                  pl.BlockSpec((1, d), const), pl.BlockSpec((1, d), const)],
        out_specs=pl.BlockSpec((tm, d), row),
        out_shape=jax.ShapeDtypeStruct((n, d), F32),
        compiler_params=_params("parallel"),
        name="mix",
    )(yf, ya, gf, ga, x, wf, wa, wo, g.reshape(1, d), b.reshape(1, d))


_NEG_INF = float("-inf")


def _extract_top(s, order, count, exact):
    rank = jnp.full(s.shape, float(count), F32)
    big = float(2 ** 20)
    vals = []
    for a in range(count):
        m = jnp.max(s, axis=0, keepdims=True)
        if exact:
            first = jnp.min(jnp.where(s == m, order, big), axis=0, keepdims=True)
            sel = order == first
        else:
            sel = s == m
        rank = jnp.where(sel, float(a), rank)
        s = jnp.where(sel, _NEG_INF, s)
        vals.append(m)
    extracted = jnp.sum(jnp.where(rank < float(count), 1.0, 0.0), axis=0, keepdims=True)
    return vals, rank, jnp.where(extracted == float(count), 0.0, 1.0)


def _candidate_rows(v1, v2, combine):
    v2_all = jnp.concatenate(v2, axis=0)
    v2_lo = v2_all[:8]
    pieces = [combine(v1[0], v2_all)]
    pieces += [combine(v1[a], v2_lo) for a in range(1, 8)]
    pieces.append(combine(jnp.concatenate(v1[8:], axis=0), v2[0]))
    return jnp.concatenate(pieces, axis=0)


def _candidate_positions(tokens):
    pos = ([b for b in range(16)] + [a * 16 + b for a in range(1, 8) for b in range(8)]
           + [a * 16 for a in range(8, 16)])
    col = jnp.asarray(np.asarray(pos, np.float32).reshape(-1, 1))
    return jnp.broadcast_to(col, (len(pos), tokens))


def _peer_route_kernel(x_ref, wq_ref, keys_ref, pos_ref, g1_ref, g2_ref):
    ambiguous = _route_tile(x_ref, wq_ref, keys_ref, pos_ref, g1_ref, g2_ref, exact=False)

    @pl.when(jnp.max(ambiguous) > 0.0)
    def _():
        _route_tile(x_ref, wq_ref, keys_ref, pos_ref, g1_ref, g2_ref, exact=True)


def _route_tile(x_ref, wq_ref, keys_ref, pos_ref, g1_ref, g2_ref, exact):
    tokens = x_ref.shape[0]
    xb = x_ref[...].astype(BF16)
    key_order = lax.broadcasted_iota(jnp.int32, (PEER_N_KEYS, tokens), 0).astype(F32)
    cand_pos = pos_ref[...]
    ambiguous = jnp.zeros((1, tokens), F32)
    for h in range(PEER_HEADS):
        scores, vals, ranks = [], [], []
        for c in range(2):
            col = (2 * h + c) * PEER_HALF
            qp = _dot(xb, wq_ref[:, col:col + PEER_HALF]).astype(BF16)
            s = _dot_nt(keys_ref[h, c], qp)
            v, r, amb = _extract_top(s, key_order, PEER_TOPK, exact)
            ambiguous = jnp.maximum(ambiguous, amb)
            scores.append(s)
            vals.append(v)
            ranks.append(r)
        cand = _candidate_rows(vals[0], vals[1], lambda a, b: a + b)
        _, crank, amb = _extract_top(cand, cand_pos, PEER_TOPK, exact)
        ambiguous = jnp.maximum(ambiguous, amb)
        chosen = crank < float(PEER_TOPK)
        e1 = [jnp.exp(v - vals[0][0]) for v in vals[0]]
        e2 = [jnp.exp(v - vals[1][0]) for v in vals[1]]
        ecand = _candidate_rows(e1, e2, lambda a, b: a * b)
        z = jnp.sum(jnp.where(chosen, ecand, 0.0), axis=0, keepdims=True)
        cnt = jnp.where(chosen, 1.0, 0.0)
        limit = [jnp.sum(cnt[0:16], axis=0, keepdims=True)]
        limit += [jnp.sum(cnt[8 + 8 * a:16 + 8 * a], axis=0, keepdims=True) for a in range(1, 8)]
        limit += [cnt[72 + a:73 + a] for a in range(8)]
        lim_dense = jnp.zeros((PEER_N_KEYS, tokens), F32)
        for a in range(PEER_TOPK):
            lim_dense = jnp.where(ranks[0] == float(a), limit[a], lim_dense)
        g1_ref[0, h] = jnp.exp(scores[0] - vals[0][0]) / z
        g1_ref[1, h] = lim_dense
        g2_ref[0, h] = jnp.exp(scores[1] - vals[1][0]).astype(BF16)
        g2_ref[1, h] = ranks[1].astype(BF16)
    return ambiguous


def _peer_route_call(x, wq, keys, tm):
    n, d = x.shape
    pos = _candidate_positions(tm)
    return pl.pallas_call(
        _peer_route_kernel,
        grid=(n // tm,),
        in_specs=[pl.BlockSpec((tm, d), lambda i: (i, 0)),
                  pl.BlockSpec(wq.shape, lambda i: (0, 0)),
                  pl.BlockSpec(keys.shape, lambda i: (0, 0, 0, 0)),
                  pl.BlockSpec(pos.shape, lambda i: (0, 0))],
        out_specs=[pl.BlockSpec((2, PEER_HEADS, PEER_N_KEYS, tm), lambda i: (0, 0, 0, i)),
                   pl.BlockSpec((2, PEER_HEADS, PEER_N_KEYS, tm), lambda i: (0, 0, 0, i))],
        out_shape=[jax.ShapeDtypeStruct((2, PEER_HEADS, PEER_N_KEYS, n), F32),
                   jax.ShapeDtypeStruct((2, PEER_HEADS, PEER_N_KEYS, n), BF16)],
        compiler_params=_params("parallel"),
        name="peer_route",
    )(x, wq, keys, pos)


def _peer_expert_kernel(x_ref, g1_ref, g2_ref, u_ref, vt_ref, g_ref, b_ref, o_ref, xb_ref, acc_ref, p_ref, *,
                        alpha, rows_per_step):
    j = pl.program_id(1)

    @pl.when(j == 0)
    def _():
        xb_ref[...] = x_ref[...].astype(BF16)
        acc_ref[...] = jnp.zeros_like(acc_ref)

    act = _dot_nt(u_ref[...], xb_ref[...])
    act = 0.5 * act * (1.0 + lax.erf(act * math.sqrt(0.5)))
    for r in range(rows_per_step):
        i1 = j * rows_per_step + r
        gate = None
        for h in range(PEER_HEADS):
            e1n = g1_ref[0, h, pl.ds(i1, 1), :].astype(BF16)
            lim = g1_ref[1, h, pl.ds(i1, 1), :].astype(BF16)
            term = jnp.where(g2_ref[1, h] < lim, g2_ref[0, h] * e1n, jnp.zeros((), BF16))
            gate = term if gate is None else gate + term
        lo, hi = r * PEER_N_KEYS, (r + 1) * PEER_N_KEYS
        p_ref[lo:hi, :] = gate * act[lo:hi, :].astype(BF16)
    acc_ref[...] += _dot(vt_ref[...], p_ref[...])

    @pl.when(j == pl.num_programs(1) - 1)
    def _():
        z = alpha * x_ref[...] + acc_ref[...].T
        o_ref[...] = _layer_norm(z, g_ref[...], b_ref[...])


def _peer_expert_call(x, g1, g2, u, vt, g, b, tm, rows_per_step, alpha):
    n, d = x.shape
    te = rows_per_step * PEER_N_KEYS
    n_exp = u.shape[0]
    return pl.pallas_call(
        functools.partial(_peer_expert_kernel, alpha=alpha, rows_per_step=rows_per_step),
        grid=(n // tm, n_exp // te),
        in_specs=[pl.BlockSpec((tm, d), lambda i, j: (i, 0)),
                  pl.BlockSpec((2, PEER_HEADS, PEER_N_KEYS, tm), lambda i, j: (0, 0, 0, i)),
                  pl.BlockSpec((2, PEER_HEADS, PEER_N_KEYS, tm), lambda i, j: (0, 0, 0, i)),
                  pl.BlockSpec((te, d), lambda i, j: (j, 0)),
                  pl.BlockSpec((d, te), lambda i, j: (0, j)),
                  pl.BlockSpec((1, d), lambda i, j: (0, 0)),
                  pl.BlockSpec((1, d), lambda i, j: (0, 0))],
        out_specs=pl.BlockSpec((tm, d), lambda i, j: (i, 0)),
        out_shape=jax.ShapeDtypeStruct((n, d), F32),
        scratch_shapes=[pltpu.VMEM((tm, d), BF16), pltpu.VMEM((d, tm), F32), pltpu.VMEM((te, tm), BF16)],
        compiler_params=_params("parallel", "arbitrary"),
        name="peer_experts",
    )(x, g1, g2, u, vt, g.reshape(1, d), b.reshape(1, d))


def _rope_tables(seq):
    rows = seq // GRID_W
    row_id = jnp.repeat(jnp.arange(rows, dtype=F32), GRID_W)
    col_id = jnp.tile(jnp.arange(GRID_W, dtype=F32), rows)
    inv_freq = 1.0 / (ROPE_THETA ** (jnp.arange(0, ROPE_AXIS_DIM, 2, dtype=F32) / ROPE_AXIS_DIM))
    ang_r = row_id[:, None] * inv_freq[None, :]
    ang_c = col_id[:, None] * inv_freq[None, :]
    ang = jnp.concatenate([ang_r, ang_r, ang_c, ang_c], axis=-1)
    sign = jnp.where((jnp.arange(HEAD_DIM) % ROPE_AXIS_DIM) < ROPE_HALF, -1.0, 1.0).astype(F32)
    return jnp.cos(ang), jnp.sin(ang) * sign[None, :]


def _head_mean_matrix(heads):
    m = np.kron(np.eye(heads), np.full((HEAD_DIM, HEAD_DIM), 1.0 / HEAD_DIM))
    return jnp.asarray(m, BF16)


TM_LN = 512
TM_IN = 512
DFT1_COLS = 2048
DFT2_K1 = 8
ATTN_TQ = 128
TM_MIX = 256
TM_ROUTE = 256
TM_EXPERT = 512
EXPERT_ROWS_PER_STEP = 8


def kernel(x, ln0_g, ln0_b, w_in, b_gate, q_norm_g, k_norm_g, w_branch_fourier, w_branch_attn, w_out,
           ln1_g, ln1_b, peer_w_query, peer_sub_keys, peer_u, peer_v, ln2_g, ln2_b):
    batch, seq, d = x.shape
    n = batch * seq
    depth = w_in.shape[0]
    alpha = (2.0 * depth) ** 0.25
    n1 = seq // DFT_N2

    cos, sin_signed = _rope_tables(seq)
    cq, sq = jnp.tile(cos, (1, N_Q_HEADS)), jnp.tile(sin_signed, (1, N_Q_HEADS))
    ck, sk = jnp.tile(cos, (1, N_KV_HEADS)), jnp.tile(sin_signed, (1, N_KV_HEADS))
    hmq, hmk = _head_mean_matrix(N_Q_HEADS), _head_mean_matrix(N_KV_HEADS)
    cdft, m1, m3, tw_r, tw_i = _dft_constants(seq)
    tw_r = jnp.repeat(tw_r, D_FOURIER, axis=1)
    tw_i = jnp.repeat(tw_i, D_FOURIER, axis=1)

    xs = _ln_call(x.reshape(n, d), ln0_g, ln0_b, TM_LN)
    for l in range(depth):
        qg = jnp.tile(q_norm_g[l], N_Q_HEADS).reshape(1, D_ATTN)
        kg = jnp.tile(k_norm_g[l], N_KV_HEADS).reshape(1, D_KV)
        zr, zi, q, k, vt, gf, ga = _in_proj_call(
            xs, w_in[l].astype(BF16), b_gate[l], qg, kg, cq, sq, ck, sk, cdft, hmq, hmk, TM_IN, seq)
        a = _dft_stage1_call(zr.reshape(batch, n1, DFT_N2 * D_FOURIER), zi.reshape(batch, n1, DFT_N2 * D_FOURIER),
                             m1, tw_r, tw_i, DFT1_COLS)
        yf = _dft_stage2_call(a.reshape(batch, 2, n1, DFT_N2, D_FOURIER), m3, DFT2_K1)
        ya = _attn_call(q, k, vt, batch, seq, ATTN_TQ)
        xs = _mix_call(yf.reshape(n, D_FOURIER), ya, gf, ga, xs,
                       w_branch_fourier[l].astype(BF16), w_branch_attn[l].astype(BF16), w_out[l].astype(BF16),
                       ln1_g[l], ln1_b[l], TM_MIX, alpha)
        g1, g2 = _peer_route_call(xs, peer_w_query[l].astype(BF16), peer_sub_keys[l].astype(BF16), TM_ROUTE)
        xs = _peer_expert_call(xs, g1, g2, peer_u[l].astype(BF16), peer_v[l].astype(BF16).T,
                               ln2_g[l], ln2_b[l], TM_EXPERT, EXPERT_ROWS_PER_STEP, alpha)
    return xs.reshape(batch, seq, d)
```

```python
import functools
import math

import numpy as np
import jax
import jax.numpy as jnp
from jax import lax
from jax.experimental import pallas as pl
from jax.experimental.pallas import tpu as pltpu

F32 = jnp.float32
BF16 = jnp.bfloat16

D_MODEL = 1024
GRID_W = 64
N_FOURIER_GROUPS = 4
FOURIER_GROUP_DIM = 128
D_FOURIER = N_FOURIER_GROUPS * FOURIER_GROUP_DIM
N_Q_HEADS = 8
N_KV_HEADS = 2
Q_PER_KV = N_Q_HEADS // N_KV_HEADS
HEAD_DIM = 64
D_ATTN = N_Q_HEADS * HEAD_DIM
D_KV = N_KV_HEADS * HEAD_DIM
ROPE_AXIS_DIM = HEAD_DIM // 2
ROPE_HALF = ROPE_AXIS_DIM // 2
ROPE_THETA = 10000.0
LOG2_E = math.log2(math.e)
V_ROWS = HEAD_DIM + 16
PEER_HEADS = 8
PEER_N_KEYS = 128
PEER_TOPK = 16
PEER_HALF = 128
LN_EPS = 1e-5
RMS_EPS = 1e-6

_C_XF = 0
_C_Q = D_FOURIER
_C_K = _C_Q + D_ATTN
_C_V = _C_K + D_KV
_C_GF = _C_V + D_KV
_C_GA = _C_GF + D_MODEL
_C_END = _C_GA + D_MODEL

DFT_N2 = 128

VMEM_LIMIT_BYTES = 56 * 1024 * 1024
def _params(*sem):
    return pltpu.CompilerParams(dimension_semantics=sem, vmem_limit_bytes=VMEM_LIMIT_BYTES)


def _dot(a, b):
    return jnp.dot(a, b, preferred_element_type=F32)


def _dot_nt(a, b):
    return lax.dot_general(a, b, (((1,), (1,)), ((), ())), preferred_element_type=F32)


def _layer_norm(z, g, b):
    mu = jnp.mean(z, axis=-1, keepdims=True)
    zc = z - mu
    var = jnp.mean(zc * zc, axis=-1, keepdims=True)
    return zc * lax.rsqrt(var + LN_EPS) * g + b


def _ln_kernel(x_ref, g_ref, b_ref, o_ref):
    o_ref[...] = _layer_norm(x_ref[...], g_ref[...], b_ref[...])


def _ln_call(x, g, b, tm):
    n, d = x.shape
    return pl.pallas_call(
        _ln_kernel,
        grid=(n // tm,),
        in_specs=[pl.BlockSpec((tm, d), lambda i: (i, 0)),
                  pl.BlockSpec((1, d), lambda i: (0, 0)),
                  pl.BlockSpec((1, d), lambda i: (0, 0))],
        out_specs=pl.BlockSpec((tm, d), lambda i: (i, 0)),
        out_shape=jax.ShapeDtypeStruct((n, d), F32),
        compiler_params=_params("parallel"),
        name="ln0",
    )(x, g.reshape(1, d), b.reshape(1, d))


def _rms_rope(t, gain, head_mean, cos, sin_signed):
    sq = t * t
    hi = sq.astype(BF16)
    lo = (sq - hi.astype(F32)).astype(BF16)
    ms = _dot(hi, head_mean) + _dot(lo, head_mean)
    tn = t * lax.rsqrt(ms + RMS_EPS) * gain
    width = t.shape[-1]
    lane = lax.broadcasted_iota(jnp.int32, tn.shape, 1)
    first_half = (lane % ROPE_AXIS_DIM) < ROPE_HALF
    rot = jnp.where(first_half, pltpu.roll(tn, width - ROPE_HALF, 1), pltpu.roll(tn, ROPE_HALF, 1))
    return tn * cos + rot * sin_signed


def _in_proj_kernel(x_ref, w_ref, bg_ref, qg_ref, kg_ref, cq_ref, sq_ref, ck_ref, sk_ref,
                    cdft_ref, hmq_ref, hmk_ref,
                    zr_ref, zi_ref, q_ref, k_ref, vt_ref, gf_ref, ga_ref):
    xb = x_ref[...].astype(BF16)

    def seg(a, b):
        return _dot(xb, w_ref[:, a:b])

    xf = seg(_C_XF, _C_Q).astype(BF16)
    for g in range(N_FOURIER_GROUPS):
        lo, hi = g * FOURIER_GROUP_DIM, (g + 1) * FOURIER_GROUP_DIM
        z = _dot(xf[:, lo:hi], cdft_ref[...])
        zr_ref[:, lo:hi] = z[:, :FOURIER_GROUP_DIM]
        zi_ref[:, lo:hi] = z[:, FOURIER_GROUP_DIM:]

    q = _rms_rope(seg(_C_Q, _C_K), qg_ref[...], hmq_ref[...], cq_ref[...], sq_ref[...])
    q_ref[...] = (q * (HEAD_DIM ** -0.5 * LOG2_E)).astype(BF16)
    k = _rms_rope(seg(_C_K, _C_V), kg_ref[...], hmk_ref[...], ck_ref[...], sk_ref[...]).astype(BF16)
    vt = seg(_C_V, _C_GF).T.astype(BF16)
    pad_row = lax.broadcasted_iota(jnp.int32, (V_ROWS - HEAD_DIM, vt.shape[1]), 0)
    pad = jnp.where(pad_row == 0, 1.0, 0.0).astype(BF16)
    for j in range(N_KV_HEADS):
        k_ref[j] = k[:, j * HEAD_DIM:(j + 1) * HEAD_DIM]
        vt_ref[j, 0] = jnp.concatenate([vt[j * HEAD_DIM:(j + 1) * HEAD_DIM, :], pad], axis=0)

    gf_ref[...] = jax.nn.sigmoid(seg(_C_GF, _C_GA) + bg_ref[0:1, :])
    ga_ref[...] = jax.nn.sigmoid(seg(_C_GA, _C_END) + bg_ref[1:2, :])


def _in_proj_call(x, w, bg, qg, kg, cq, sq, ck, sk, cdft, hmq, hmk, tm, seq):
    n, d = x.shape
    pos_blocks = seq // tm
    row = lambda i: (i, 0)
    pos = lambda i: (i % pos_blocks, 0)
    const = lambda i: (0, 0)
    full = lambda a: pl.BlockSpec(a.shape, const)
    return pl.pallas_call(
        _in_proj_kernel,
        grid=(n // tm,),
        in_specs=[pl.BlockSpec((tm, d), row), full(w), full(bg), full(qg), full(kg),
                  pl.BlockSpec((tm, D_ATTN), pos), pl.BlockSpec((tm, D_ATTN), pos),
                  pl.BlockSpec((tm, D_KV), pos), pl.BlockSpec((tm, D_KV), pos),
                  full(cdft), full(hmq), full(hmk)],
        out_specs=[pl.BlockSpec((tm, D_FOURIER), row), pl.BlockSpec((tm, D_FOURIER), row),
                   pl.BlockSpec((tm, D_ATTN), row),
                   pl.BlockSpec((N_KV_HEADS, tm, HEAD_DIM), lambda i: (0, i, 0)),
                   pl.BlockSpec((N_KV_HEADS, 1, V_ROWS, tm), lambda i: (0, i, 0, 0)),
                   pl.BlockSpec((tm, d), row), pl.BlockSpec((tm, d), row)],
        out_shape=[jax.ShapeDtypeStruct((n, D_FOURIER), F32), jax.ShapeDtypeStruct((n, D_FOURIER), F32),
                   jax.ShapeDtypeStruct((n, D_ATTN), BF16),
                   jax.ShapeDtypeStruct((N_KV_HEADS, n, HEAD_DIM), BF16),
                   jax.ShapeDtypeStruct((N_KV_HEADS, n // tm, V_ROWS, tm), BF16),
                   jax.ShapeDtypeStruct((n, d), F32), jax.ShapeDtypeStruct((n, d), F32)],
        compiler_params=_params("parallel"),
        name="in_proj",
    )(x, w, bg, qg, kg, cq, sq, ck, sk, cdft, hmq, hmk)


def _dft_stage1_kernel(zr_ref, zi_ref, m1_ref, tr_ref, ti_ref, o_ref):
    n1 = zr_ref.shape[1]
    zz = jnp.concatenate([zr_ref[0], zi_ref[0]], axis=0).astype(BF16)
    a = _dot(m1_ref[...], zz)
    ar, ai = a[:n1], a[n1:]
    tr, ti = tr_ref[...], ti_ref[...]
    o_ref[0, 0] = ar * tr - ai * ti
    o_ref[0, 1] = ar * ti + ai * tr


def _dft_stage1_call(zr, zi, m1, tr, ti, cw):
    b, n1, cols = zr.shape
    blk = pl.BlockSpec((1, n1, cw), lambda i, j: (i, 0, j))
    tw = pl.BlockSpec((n1, cw), lambda i, j: (0, j))
    return pl.pallas_call(
        _dft_stage1_kernel,
        grid=(b, cols // cw),
        in_specs=[blk, blk, pl.BlockSpec(m1.shape, lambda i, j: (0, 0)), tw, tw],
        out_specs=pl.BlockSpec((1, 2, n1, cw), lambda i, j: (i, 0, 0, j)),
        out_shape=jax.ShapeDtypeStruct((b, 2, n1, cols), F32),
        compiler_params=_params("parallel", "parallel"),
        name="dft_stage1",
    )(zr, zi, m1, tr, ti)


def _dft_stage2_kernel(a_ref, m3_ref, o_ref):
    kb, n2, c = a_ref.shape[2], a_ref.shape[3], a_ref.shape[4]
    for j in range(kb):
        rhs = jnp.concatenate([a_ref[0, 0, j], a_ref[0, 1, j]], axis=0).astype(BF16)
        o_ref[0, :, j * c:(j + 1) * c] = _dot(m3_ref[...], rhs)


def _dft_stage2_call(a, m3, kb):
    b, _, n1, n2, c = a.shape
    return pl.pallas_call(
        _dft_stage2_kernel,
        grid=(b, n1 // kb),
        in_specs=[pl.BlockSpec((1, 2, kb, n2, c), lambda i, j: (i, 0, j, 0, 0)),
                  pl.BlockSpec(m3.shape, lambda i, j: (0, 0))],
        out_specs=pl.BlockSpec((1, n2, kb * c), lambda i, j: (i, 0, j)),
        out_shape=jax.ShapeDtypeStruct((b, n2, n1 * c), F32),
        compiler_params=_params("parallel", "parallel"),
        name="dft_stage2",
    )(a, m3)


def _dft_constants(seq):
    c = FOURIER_GROUP_DIM
    n1, n2 = seq // DFT_N2, DFT_N2
    jc = np.arange(c)
    ang = 2.0 * np.pi * np.outer(jc, jc) / c
    cdft = np.concatenate([np.cos(ang), -np.sin(ang)], axis=1) / math.sqrt(c)
    j1 = np.arange(n1)
    a1 = 2.0 * np.pi * np.outer(j1, j1) / n1
    fr, fi = np.cos(a1) / math.sqrt(n1), -np.sin(a1) / math.sqrt(n1)
    m1 = np.block([[fr, -fi], [fi, fr]])
    j2 = np.arange(n2)
    a2 = 2.0 * np.pi * np.outer(j2, j2) / n2
    m3 = np.concatenate([np.cos(a2), np.sin(a2)], axis=1) / math.sqrt(n2)
    at = 2.0 * np.pi * np.outer(j1, j2) / seq
    return (jnp.asarray(cdft, BF16), jnp.asarray(m1, BF16), jnp.asarray(m3, BF16),
            jnp.asarray(np.cos(at), F32), jnp.asarray(-np.sin(at), F32))


def _attn_kernel(q_ref, k_ref, vt_ref, o_ref, qs_ref, st0_ref, st1_ref, p0_ref, p1_ref, acc_ref, *, tq):
    n_chunks, _, tk = vt_ref.shape[1:]
    assert n_chunks % 2 == 0 and n_chunks >= 4
    nq = Q_PER_KV * tq
    for g in range(Q_PER_KV):
        qs_ref[g * tq:(g + 1) * tq, :] = q_ref[:, g * HEAD_DIM:(g + 1) * HEAD_DIM]
    acc_ref[...] = jnp.zeros_like(acc_ref)

    def scores(c, st_ref):
        start = pl.multiple_of(c * tk, tk)
        st_ref[...] = _dot_nt(k_ref[0, pl.ds(start, tk), :], qs_ref[...])

    def softmax(st_ref, p_ref, m):
        st = st_ref[...]
        m_new = jnp.maximum(m, jnp.max(st, axis=0, keepdims=True))
        p_ref[...] = jnp.exp2(st - m_new).astype(BF16)
        return m_new, jnp.exp2(m - m_new)

    def values(c, p_ref, alpha):
        acc_ref[...] = alpha * acc_ref[...] + _dot(vt_ref[0, c], p_ref[...])

    def step(i, carry, st_cur, st_nxt, p_cur, p_prv, last=False):
        m, alpha_prv = carry
        values(i - 1, p_prv, alpha_prv)
        m, alpha = softmax(st_cur, p_cur, m)
        if not last:
            scores(i + 1, st_nxt)
        return m, alpha

    scores(0, st0_ref)
    carry = softmax(st0_ref, p0_ref, jnp.full((1, nq), -jnp.inf, F32))
    scores(1, st1_ref)

    def body(ii, carry):
        i = 2 * ii + 1
        carry = step(i, carry, st1_ref, st0_ref, p1_ref, p0_ref)
        return step(i + 1, carry, st0_ref, st1_ref, p0_ref, p1_ref)

    carry = lax.fori_loop(0, n_chunks // 2 - 1, body, carry)
    _, alpha = step(n_chunks - 1, carry, st1_ref, st0_ref, p1_ref, p0_ref, last=True)
    values(n_chunks - 1, p1_ref, alpha)

    ot = acc_ref[:HEAD_DIM, :] / acc_ref[HEAD_DIM:HEAD_DIM + 1, :]
    o = jnp.concatenate([ot, jnp.zeros_like(ot)], axis=0).T
    o_ref[...] = jnp.concatenate([o[g * tq:(g + 1) * tq, :HEAD_DIM] for g in range(Q_PER_KV)], axis=1)


def _attn_call(q, k, vt, batch, seq, tq):
    n = q.shape[0]
    qblocks = seq // tq
    gw = Q_PER_KV * HEAD_DIM
    tk = vt.shape[-1]
    chunks = seq // tk
    nq = Q_PER_KV * tq
    return pl.pallas_call(
        functools.partial(_attn_kernel, tq=tq),
        grid=(batch, N_KV_HEADS, qblocks),
        in_specs=[pl.BlockSpec((tq, gw), lambda b, j, i: (b * qblocks + i, j)),
                  pl.BlockSpec((1, seq, HEAD_DIM), lambda b, j, i: (j, b, 0)),
                  pl.BlockSpec((1, chunks, V_ROWS, tk), lambda b, j, i: (j, b, 0, 0))],
        out_specs=pl.BlockSpec((tq, gw), lambda b, j, i: (b * qblocks + i, j)),
        out_shape=jax.ShapeDtypeStruct((n, D_ATTN), F32),
        scratch_shapes=[pltpu.VMEM((nq, HEAD_DIM), BF16),
                        pltpu.VMEM((tk, nq), F32), pltpu.VMEM((tk, nq), F32),
                        pltpu.VMEM((tk, nq), BF16), pltpu.VMEM((tk, nq), BF16),
                        pltpu.VMEM((V_ROWS, nq), F32)],
        compiler_params=_params("parallel", "parallel", "parallel"),
        name="attention",
    )(q, k, vt)


def _mix_kernel(yf_ref, ya_ref, gf_ref, ga_ref, x_ref, wf_ref, wa_ref, wo_ref, g_ref, b_ref, o_ref, *, alpha):
    mixed = (gf_ref[...] * _dot(yf_ref[...].astype(BF16), wf_ref[...])
             + ga_ref[...] * _dot(ya_ref[...].astype(BF16), wa_ref[...]))
    z = alpha * x_ref[...] + _dot(mixed.astype(BF16), wo_ref[...])
    o_ref[...] = _layer_norm(z, g_ref[...], b_ref[...])


def _mix_call(yf, ya, gf, ga, x, wf, wa, wo, g, b, tm, alpha):
    n, d = x.shape
    row = lambda i: (i, 0)
    const = lambda i: (0, 0)
    return pl.pallas_call(
        functools.partial(_mix_kernel, alpha=alpha),
        grid=(n // tm,),
        in_specs=[pl.BlockSpec((tm, D_FOURIER), row), pl.BlockSpec((tm, D_ATTN), row),
                  pl.BlockSpec((tm, d), row), pl.BlockSpec((tm, d), row), pl.BlockSpec((tm, d), row),
                  pl.BlockSpec(wf.shape, const), pl.BlockSpec(wa.shape, const), pl.BlockSpec(wo.shape, const),
                  pl.BlockSpec((1, d), const), pl.BlockSpec((1, d), const)],
        out_specs=pl.BlockSpec((tm, d), row),
        out_shape=jax.ShapeDtypeStruct((n, d), F32),
        compiler_params=_params("parallel"),
        name="mix",
    )(yf, ya, gf, ga, x, wf, wa, wo, g.reshape(1, d), b.reshape(1, d))


_NEG_INF = float("-inf")


def _extract_top(s, order, count, exact):
    rank = jnp.full(s.shape, float(count), F32)
    big = float(2 ** 20)
    vals = []
    for a in range(count):
        m = jnp.max(s, axis=0, keepdims=True)
        if exact:
            first = jnp.min(jnp.where(s == m, order, big), axis=0, keepdims=True)
            sel = order == first
        else:
            sel = s == m
        rank = jnp.where(sel, float(a), rank)
        s = jnp.where(sel, _NEG_INF, s)
        vals.append(m)
    extracted = jnp.sum(jnp.where(rank < float(count), 1.0, 0.0), axis=0, keepdims=True)
    return vals, rank, jnp.where(extracted == float(count), 0.0, 1.0)


def _candidate_rows(v1, v2, combine):
    v2_all = jnp.concatenate(v2, axis=0)
    v2_lo = v2_all[:8]
    pieces = [combine(v1[0], v2_all)]
    pieces += [combine(v1[a], v2_lo) for a in range(1, 8)]
    pieces.append(combine(jnp.concatenate(v1[8:], axis=0), v2[0]))
    return jnp.concatenate(pieces, axis=0)


def _candidate_positions(tokens):
    pos = ([b for b in range(16)] + [a * 16 + b for a in range(1, 8) for b in range(8)]
           + [a * 16 for a in range(8, 16)])
    col = jnp.asarray(np.asarray(pos, np.float32).reshape(-1, 1))
    return jnp.broadcast_to(col, (len(pos), tokens))


def _peer_route_kernel(x_ref, wq_ref, keys_ref, pos_ref, g1_ref, g2_ref):
    ambiguous = _route_tile(x_ref, wq_ref, keys_ref, pos_ref, g1_ref, g2_ref, exact=False)

    @pl.when(jnp.max(ambiguous) > 0.0)
    def _():
        _route_tile(x_ref, wq_ref, keys_ref, pos_ref, g1_ref, g2_ref, exact=True)


def _route_tile(x_ref, wq_ref, keys_ref, pos_ref, g1_ref, g2_ref, exact):
    tokens = x_ref.shape[0]
    xb = x_ref[...].astype(BF16)
    key_order = lax.broadcasted_iota(jnp.int32, (PEER_N_KEYS, tokens), 0).astype(F32)
    cand_pos = pos_ref[...]
    ambiguous = jnp.zeros((1, tokens), F32)
    for h in range(PEER_HEADS):
        scores, vals, ranks = [], [], []
        for c in range(2):
            col = (2 * h + c) * PEER_HALF
            qp = _dot(xb, wq_ref[:, col:col + PEER_HALF]).astype(BF16)
            s = _dot_nt(keys_ref[h, c], qp)
            v, r, amb = _extract_top(s, key_order, PEER_TOPK, exact)
            ambiguous = jnp.maximum(ambiguous, amb)
            scores.append(s)
            vals.append(v)
            ranks.append(r)
        cand = _candidate_rows(vals[0], vals[1], lambda a, b: a + b)
        _, crank, amb = _extract_top(cand, cand_pos, PEER_TOPK, exact)
        ambiguous = jnp.maximum(ambiguous, amb)
        chosen = crank < float(PEER_TOPK)
        e1 = [jnp.exp(v - vals[0][0]) for v in vals[0]]
        e2 = [jnp.exp(v - vals[1][0]) for v in vals[1]]
        ecand = _candidate_rows(e1, e2, lambda a, b: a * b)
        z = jnp.sum(jnp.where(chosen, ecand, 0.0), axis=0, keepdims=True)
        cnt = jnp.where(chosen, 1.0, 0.0)
        limit = [jnp.sum(cnt[0:16], axis=0, keepdims=True)]
        limit += [jnp.sum(cnt[8 + 8 * a:16 + 8 * a], axis=0, keepdims=True) for a in range(1, 8)]
        limit += [cnt[72 + a:73 + a] for a in range(8)]
        lim_dense = jnp.zeros((PEER_N_KEYS, tokens), F32)
        for a in range(PEER_TOPK):
            lim_dense = jnp.where(ranks[0] == float(a), limit[a], lim_dense)
        g1_ref[0, h] = jnp.exp(scores[0] - vals[0][0]) / z
        g1_ref[1, h] = lim_dense
        g2_ref[0, h] = jnp.exp(scores[1] - vals[1][0]).astype(BF16)
        g2_ref[1, h] = ranks[1].astype(BF16)
    return ambiguous


def _peer_route_call(x, wq, keys, tm):
    n, d = x.shape
    pos = _candidate_positions(tm)
    return pl.pallas_call(
        _peer_route_kernel,
        grid=(n // tm,),
        in_specs=[pl.BlockSpec((tm, d), lambda i: (i, 0)),
                  pl.BlockSpec(wq.shape, lambda i: (0, 0)),
                  pl.BlockSpec(keys.shape, lambda i: (0, 0, 0, 0)),
                  pl.BlockSpec(pos.shape, lambda i: (0, 0))],
        out_specs=[pl.BlockSpec((2, PEER_HEADS, PEER_N_KEYS, tm), lambda i: (0, 0, 0, i)),
                   pl.BlockSpec((2, PEER_HEADS, PEER_N_KEYS, tm), lambda i: (0, 0, 0, i))],
        out_shape=[jax.ShapeDtypeStruct((2, PEER_HEADS, PEER_N_KEYS, n), F32),
                   jax.ShapeDtypeStruct((2, PEER_HEADS, PEER_N_KEYS, n), BF16)],
        compiler_params=_params("parallel"),
        name="peer_route",
    )(x, wq, keys, pos)


def _peer_expert_kernel(x_ref, g1_ref, g2_ref, u_ref, vt_ref, g_ref, b_ref, o_ref, xb_ref, acc_ref, p_ref, *,
                        alpha, rows_per_step):
    j = pl.program_id(1)

    @pl.when(j == 0)
    def _():
        xb_ref[...] = x_ref[...].astype(BF16)
        acc_ref[...] = jnp.zeros_like(acc_ref)

    act = _dot_nt(u_ref[...], xb_ref[...])
    act = 0.5 * act * (1.0 + lax.erf(act * math.sqrt(0.5)))
    for r in range(rows_per_step):
        i1 = j * rows_per_step + r
        gate = None
        for h in range(PEER_HEADS):
            e1n = g1_ref[0, h, pl.ds(i1, 1), :].astype(BF16)
            lim = g1_ref[1, h, pl.ds(i1, 1), :].astype(BF16)
            term = jnp.where(g2_ref[1, h] < lim, g2_ref[0, h] * e1n, jnp.zeros((), BF16))
            gate = term if gate is None else gate + term
        lo, hi = r * PEER_N_KEYS, (r + 1) * PEER_N_KEYS
        p_ref[lo:hi, :] = gate * act[lo:hi, :].astype(BF16)
    acc_ref[...] += _dot(vt_ref[...], p_ref[...])

    @pl.when(j == pl.num_programs(1) - 1)
    def _():
        z = alpha * x_ref[...] + acc_ref[...].T
        o_ref[...] = _layer_norm(z, g_ref[...], b_ref[...])


def _peer_expert_call(x, g1, g2, u, vt, g, b, tm, rows_per_step, alpha):
    n, d = x.shape
    te = rows_per_step * PEER_N_KEYS
    n_exp = u.shape[0]
    return pl.pallas_call(
        functools.partial(_peer_expert_kernel, alpha=alpha, rows_per_step=rows_per_step),
        grid=(n // tm, n_exp // te),
        in_specs=[pl.BlockSpec((tm, d), lambda i, j: (i, 0)),
                  pl.BlockSpec((2, PEER_HEADS, PEER_N_KEYS, tm), lambda i, j: (0, 0, 0, i)),
                  pl.BlockSpec((2, PEER_HEADS, PEER_N_KEYS, tm), lambda i, j: (0, 0, 0, i)),
                  pl.BlockSpec((te, d), lambda i, j: (j, 0)),
                  pl.BlockSpec((d, te), lambda i, j: (0, j)),
                  pl.BlockSpec((1, d), lambda i, j: (0, 0)),
                  pl.BlockSpec((1, d), lambda i, j: (0, 0))],
        out_specs=pl.BlockSpec((tm, d), lambda i, j: (i, 0)),
        out_shape=jax.ShapeDtypeStruct((n, d), F32),
        scratch_shapes=[pltpu.VMEM((tm, d), BF16), pltpu.VMEM((d, tm), F32), pltpu.VMEM((te, tm), BF16)],
        compiler_params=_params("parallel", "arbitrary"),
        name="peer_experts",
    )(x, g1, g2, u, vt, g.reshape(1, d), b.reshape(1, d))


def _rope_tables(seq):
    rows = seq // GRID_W
    row_id = jnp.repeat(jnp.arange(rows, dtype=F32), GRID_W)
    col_id = jnp.tile(jnp.arange(GRID_W, dtype=F32), rows)
    inv_freq = 1.0 / (ROPE_THETA ** (jnp.arange(0, ROPE_AXIS_DIM, 2, dtype=F32) / ROPE_AXIS_DIM))
    ang_r = row_id[:, None] * inv_freq[None, :]
    ang_c = col_id[:, None] * inv_freq[None, :]
    ang = jnp.concatenate([ang_r, ang_r, ang_c, ang_c], axis=-1)
    sign = jnp.where((jnp.arange(HEAD_DIM) % ROPE_AXIS_DIM) < ROPE_HALF, -1.0, 1.0).astype(F32)
    return jnp.cos(ang), jnp.sin(ang) * sign[None, :]


def _head_mean_matrix(heads):
    m = np.kron(np.eye(heads), np.full((HEAD_DIM, HEAD_DIM), 1.0 / HEAD_DIM))
    return jnp.asarray(m, BF16)


TM_LN = 512
TM_IN = 512
DFT1_COLS = 2048
DFT2_K1 = 8
ATTN_TQ = 128
TM_MIX = 256
TM_ROUTE = 256
TM_EXPERT = 512
EXPERT_ROWS_PER_STEP = 8


def kernel(x, ln0_g, ln0_b, w_in, b_gate, q_norm_g, k_norm_g, w_branch_fourier, w_branch_attn, w_out,
           ln1_g, ln1_b, peer_w_query, peer_sub_keys, peer_u, peer_v, ln2_g, ln2_b):
    batch, seq, d = x.shape
    n = batch * seq
    depth = w_in.shape[0]
    alpha = (2.0 * depth) ** 0.25
    n1 = seq // DFT_N2

    cos, sin_signed = _rope_tables(seq)
    cq, sq = jnp.tile(cos, (1, N_Q_HEADS)), jnp.tile(sin_signed, (1, N_Q_HEADS))
    ck, sk = jnp.tile(cos, (1, N_KV_HEADS)), jnp.tile(sin_signed, (1, N_KV_HEADS))
    hmq, hmk = _head_mean_matrix(N_Q_HEADS), _head_mean_matrix(N_KV_HEADS)
    cdft, m1, m3, tw_r, tw_i = _dft_constants(seq)
    tw_r = jnp.repeat(tw_r, D_FOURIER, axis=1)
    tw_i = jnp.repeat(tw_i, D_FOURIER, axis=1)

    xs = _ln_call(x.reshape(n, d), ln0_g, ln0_b, TM_LN)
    for l in range(depth):
        qg = jnp.tile(q_norm_g[l], N_Q_HEADS).reshape(1, D_ATTN)
        kg = jnp.tile(k_norm_g[l], N_KV_HEADS).reshape(1, D_KV)
        zr, zi, q, k, vt, gf, ga = _in_proj_call(
            xs, w_in[l].astype(BF16), b_gate[l], qg, kg, cq, sq, ck, sk, cdft, hmq, hmk, TM_IN, seq)
        a = _dft_stage1_call(zr.reshape(batch, n1, DFT_N2 * D_FOURIER), zi.reshape(batch, n1, DFT_N2 * D_FOURIER),
                             m1, tw_r, tw_i, DFT1_COLS)
        yf = _dft_stage2_call(a.reshape(batch, 2, n1, DFT_N2, D_FOURIER), m3, DFT2_K1)
        ya = _attn_call(q, k, vt, batch, seq, ATTN_TQ)
        xs = _mix_call(yf.reshape(n, D_FOURIER), ya, gf, ga, xs,
                       w_branch_fourier[l].astype(BF16), w_branch_attn[l].astype(BF16), w_out[l].astype(BF16),
                       ln1_g[l], ln1_b[l], TM_MIX, alpha)
        g1, g2 = _peer_route_call(xs, peer_w_query[l].astype(BF16), peer_sub_keys[l].astype(BF16), TM_ROUTE)
        xs = _peer_expert_call(xs, g1, g2, peer_u[l].astype(BF16), peer_v[l].astype(BF16).T,
                               ln2_g[l], ln2_b[l], TM_EXPERT, EXPERT_ROWS_PER_STEP, alpha)
    return xs.reshape(batch, seq, d)
```

```python
import functools
import math

import numpy as np
import jax
import jax.numpy as jnp
from jax import lax
from jax.experimental import pallas as pl
from jax.experimental.pallas import tpu as pltpu

F32 = jnp.float32
BF16 = jnp.bfloat16

D_MODEL = 1024
GRID_W = 64
N_FOURIER_GROUPS = 4
FOURIER_GROUP_DIM = 128
D_FOURIER = N_FOURIER_GROUPS * FOURIER_GROUP_DIM
N_Q_HEADS = 8
N_KV_HEADS = 2
Q_PER_KV = N_Q_HEADS // N_KV_HEADS
HEAD_DIM = 64
D_ATTN = N_Q_HEADS * HEAD_DIM
D_KV = N_KV_HEADS * HEAD_DIM
ROPE_AXIS_DIM = HEAD_DIM // 2
ROPE_HALF = ROPE_AXIS_DIM // 2
ROPE_THETA = 10000.0
LOG2_E = math.log2(math.e)
V_ROWS = HEAD_DIM + 16
PEER_HEADS = 8
PEER_N_KEYS = 128
PEER_TOPK = 16
PEER_HALF = 128
LN_EPS = 1e-5
RMS_EPS = 1e-6

_C_XF = 0
_C_Q = D_FOURIER
_C_K = _C_Q + D_ATTN
_C_V = _C_K + D_KV
_C_GF = _C_V + D_KV
_C_GA = _C_GF + D_MODEL
_C_END = _C_GA + D_MODEL

DFT_N2 = 128

VMEM_LIMIT_BYTES = 56 * 1024 * 1024
LANES = 128
def _params(*sem):
    return pltpu.CompilerParams(dimension_semantics=sem, vmem_limit_bytes=VMEM_LIMIT_BYTES)


def _dot(a, b):
    return jnp.dot(a, b, preferred_element_type=F32)


def _dot_nt(a, b):
    return lax.dot_general(a, b, (((1,), (1,)), ((), ())), preferred_element_type=F32)


def _layer_norm(z, g, b):
    mu = jnp.mean(z, axis=-1, keepdims=True)
    zc = z - mu
    var = jnp.mean(zc * zc, axis=-1, keepdims=True)
    return zc * lax.rsqrt(var + LN_EPS) * g + b


def _ln_kernel(x_ref, g_ref, b_ref, o_ref):
    o_ref[...] = _layer_norm(x_ref[...], g_ref[...], b_ref[...])


def _ln_call(x, g, b, tm):
    n, d = x.shape
    return pl.pallas_call(
        _ln_kernel,
        grid=(n // tm,),
        in_specs=[pl.BlockSpec((tm, d), lambda i: (i, 0)),
                  pl.BlockSpec((1, d), lambda i: (0, 0)),
                  pl.BlockSpec((1, d), lambda i: (0, 0))],
        out_specs=pl.BlockSpec((tm, d), lambda i: (i, 0)),
        out_shape=jax.ShapeDtypeStruct((n, d), F32),
        compiler_params=_params("parallel"),
        name="ln0",
    )(x, g.reshape(1, d), b.reshape(1, d))


def _rms_rope(t, gain, head_mean, cos, sin_signed):
    sq = t * t
    hi = sq.astype(BF16)
    lo = (sq - hi.astype(F32)).astype(BF16)
    ms = _dot(hi, head_mean) + _dot(lo, head_mean)
    tn = t * lax.rsqrt(ms + RMS_EPS) * gain
    width = t.shape[-1]
    lane = lax.broadcasted_iota(jnp.int32, tn.shape, 1)
    first_half = (lane % ROPE_AXIS_DIM) < ROPE_HALF
    rot = jnp.where(first_half, pltpu.roll(tn, width - ROPE_HALF, 1), pltpu.roll(tn, ROPE_HALF, 1))
    return tn * cos + rot * sin_signed


def _in_proj_kernel(x_ref, w_ref, bg_ref, qg_ref, kg_ref, cq_ref, sq_ref, ck_ref, sk_ref,
                    cdft_ref, hmq_ref, hmk_ref,
                    zr_ref, zi_ref, q_ref, k_ref, vt_ref, gf_ref, ga_ref):
    xb = x_ref[...].astype(BF16)

    def seg(a, b):
        return _dot(xb, w_ref[:, a:b])

    xf = seg(_C_XF, _C_Q).astype(BF16)
    for g in range(N_FOURIER_GROUPS):
        lo, hi = g * FOURIER_GROUP_DIM, (g + 1) * FOURIER_GROUP_DIM
        z = _dot(xf[:, lo:hi], cdft_ref[...])
        zr_ref[:, lo:hi] = z[:, :FOURIER_GROUP_DIM]
        zi_ref[:, lo:hi] = z[:, FOURIER_GROUP_DIM:]

    q = _rms_rope(seg(_C_Q, _C_K), qg_ref[...], hmq_ref[...], cq_ref[...], sq_ref[...])
    q_ref[...] = (q * (HEAD_DIM ** -0.5 * LOG2_E)).astype(BF16)
    k = _rms_rope(seg(_C_K, _C_V), kg_ref[...], hmk_ref[...], ck_ref[...], sk_ref[...]).astype(BF16)
    vt = seg(_C_V, _C_GF).T.astype(BF16)
    pad_row = lax.broadcasted_iota(jnp.int32, (V_ROWS - HEAD_DIM, vt.shape[1]), 0)
    pad = jnp.where(pad_row == 0, 1.0, 0.0).astype(BF16)
    for j in range(N_KV_HEADS):
        k_ref[j] = k[:, j * HEAD_DIM:(j + 1) * HEAD_DIM]
        vt_ref[j, 0] = jnp.concatenate([vt[j * HEAD_DIM:(j + 1) * HEAD_DIM, :], pad], axis=0)

    gf_ref[...] = jax.nn.sigmoid(seg(_C_GF, _C_GA) + bg_ref[0:1, :])
    ga_ref[...] = jax.nn.sigmoid(seg(_C_GA, _C_END) + bg_ref[1:2, :])


def _in_proj_call(x, w, bg, qg, kg, cq, sq, ck, sk, cdft, hmq, hmk, tm, seq):
    n, d = x.shape
    pos_blocks = seq // tm
    row = lambda i: (i, 0)
    pos = lambda i: (i % pos_blocks, 0)
    const = lambda i: (0, 0)
    full = lambda a: pl.BlockSpec(a.shape, const)
    return pl.pallas_call(
        _in_proj_kernel,
        grid=(n // tm,),
        in_specs=[pl.BlockSpec((tm, d), row), full(w), full(bg), full(qg), full(kg),
                  pl.BlockSpec((tm, D_ATTN), pos), pl.BlockSpec((tm, D_ATTN), pos),
                  pl.BlockSpec((tm, D_KV), pos), pl.BlockSpec((tm, D_KV), pos),
                  full(cdft), full(hmq), full(hmk)],
        out_specs=[pl.BlockSpec((tm, D_FOURIER), row), pl.BlockSpec((tm, D_FOURIER), row),
                   pl.BlockSpec((tm, D_ATTN), row),
                   pl.BlockSpec((N_KV_HEADS, tm, HEAD_DIM), lambda i: (0, i, 0)),
                   pl.BlockSpec((N_KV_HEADS, 1, V_ROWS, tm), lambda i: (0, i, 0, 0)),
                   pl.BlockSpec((tm, d), row), pl.BlockSpec((tm, d), row)],
        out_shape=[jax.ShapeDtypeStruct((n, D_FOURIER), F32), jax.ShapeDtypeStruct((n, D_FOURIER), F32),
                   jax.ShapeDtypeStruct((n, D_ATTN), BF16),
                   jax.ShapeDtypeStruct((N_KV_HEADS, n, HEAD_DIM), BF16),
                   jax.ShapeDtypeStruct((N_KV_HEADS, n // tm, V_ROWS, tm), BF16),
                   jax.ShapeDtypeStruct((n, d), F32), jax.ShapeDtypeStruct((n, d), F32)],
        compiler_params=_params("parallel"),
        name="in_proj",
    )(x, w, bg, qg, kg, cq, sq, ck, sk, cdft, hmq, hmk)


def _dft_stage1_kernel(zr_ref, zi_ref, m1_ref, tr_ref, ti_ref, o_ref):
    n1 = zr_ref.shape[1]
    zz = jnp.concatenate([zr_ref[0], zi_ref[0]], axis=0).astype(BF16)
    a = _dot(m1_ref[...], zz)
    ar, ai = a[:n1], a[n1:]
    tr, ti = tr_ref[...], ti_ref[...]
    o_ref[0, 0] = ar * tr - ai * ti
    o_ref[0, 1] = ar * ti + ai * tr


def _dft_stage1_call(zr, zi, m1, tr, ti, cw):
    b, n1, cols = zr.shape
    blk = pl.BlockSpec((1, n1, cw), lambda i, j: (i, 0, j))
    tw = pl.BlockSpec((n1, cw), lambda i, j: (0, j))
    return pl.pallas_call(
        _dft_stage1_kernel,
        grid=(b, cols // cw),
        in_specs=[blk, blk, pl.BlockSpec(m1.shape, lambda i, j: (0, 0)), tw, tw],
        out_specs=pl.BlockSpec((1, 2, n1, cw), lambda i, j: (i, 0, 0, j)),
        out_shape=jax.ShapeDtypeStruct((b, 2, n1, cols), F32),
        compiler_params=_params("parallel", "parallel"),
        name="dft_stage1",
    )(zr, zi, m1, tr, ti)


def _dft_stage2_kernel(a_ref, m3_ref, o_ref):
    kb, n2, c = a_ref.shape[2], a_ref.shape[3], a_ref.shape[4]
    for j in range(kb):
        rhs = jnp.concatenate([a_ref[0, 0, j], a_ref[0, 1, j]], axis=0).astype(BF16)
        o_ref[0, :, j * c:(j + 1) * c] = _dot(m3_ref[...], rhs)


def _dft_stage2_call(a, m3, kb):
    b, _, n1, n2, c = a.shape
    return pl.pallas_call(
        _dft_stage2_kernel,
        grid=(b, n1 // kb),
        in_specs=[pl.BlockSpec((1, 2, kb, n2, c), lambda i, j: (i, 0, j, 0, 0)),
                  pl.BlockSpec(m3.shape, lambda i, j: (0, 0))],
        out_specs=pl.BlockSpec((1, n2, kb * c), lambda i, j: (i, 0, j)),
        out_shape=jax.ShapeDtypeStruct((b, n2, n1 * c), F32),
        compiler_params=_params("parallel", "parallel"),
        name="dft_stage2",
    )(a, m3)


def _dft_constants(seq):
    c = FOURIER_GROUP_DIM
    n1, n2 = seq // DFT_N2, DFT_N2
    jc = np.arange(c)
    ang = 2.0 * np.pi * np.outer(jc, jc) / c
    cdft = np.concatenate([np.cos(ang), -np.sin(ang)], axis=1) / math.sqrt(c)
    j1 = np.arange(n1)
    a1 = 2.0 * np.pi * np.outer(j1, j1) / n1
    fr, fi = np.cos(a1) / math.sqrt(n1), -np.sin(a1) / math.sqrt(n1)
    m1 = np.block([[fr, -fi], [fi, fr]])
    j2 = np.arange(n2)
    a2 = 2.0 * np.pi * np.outer(j2, j2) / n2
    m3 = np.concatenate([np.cos(a2), np.sin(a2)], axis=1) / math.sqrt(n2)
    at = 2.0 * np.pi * np.outer(j1, j2) / seq
    return (jnp.asarray(cdft, BF16), jnp.asarray(m1, BF16), jnp.asarray(m3, BF16),
            jnp.asarray(np.cos(at), F32), jnp.asarray(-np.sin(at), F32))


def _attn_kernel(q_ref, k_ref, vt_ref, o_ref, qs_ref, st0_ref, st1_ref, p0_ref, p1_ref, acc_ref, *, tq):
    n_chunks, _, tk = vt_ref.shape[1:]
    assert n_chunks % 2 == 0 and n_chunks >= 4
    nq = Q_PER_KV * tq
    for g in range(Q_PER_KV):
        qs_ref[g * tq:(g + 1) * tq, :] = q_ref[:, g * HEAD_DIM:(g + 1) * HEAD_DIM]
    acc_ref[...] = jnp.zeros_like(acc_ref)

    def scores(c, st_ref):
        start = pl.multiple_of(c * tk, tk)
        st_ref[...] = _dot_nt(k_ref[0, pl.ds(start, tk), :], qs_ref[...])

    def softmax(st_ref, p_ref, m):
        st = st_ref[...]
        m_new = jnp.maximum(m, jnp.max(st, axis=0, keepdims=True))
        p_ref[...] = jnp.exp2(st - m_new).astype(BF16)
        return m_new, jnp.exp2(m - m_new)

    def values(c, p_ref, alpha):
        acc_ref[...] = alpha * acc_ref[...] + _dot(vt_ref[0, c], p_ref[...])

    def step(i, carry, st_cur, st_nxt, p_cur, p_prv, last=False):
        m, alpha_prv = carry
        values(i - 1, p_prv, alpha_prv)
        m, alpha = softmax(st_cur, p_cur, m)
        if not last:
            scores(i + 1, st_nxt)
        return m, alpha

    scores(0, st0_ref)
    carry = softmax(st0_ref, p0_ref, jnp.full((1, nq), -jnp.inf, F32))
    scores(1, st1_ref)

    def body(ii, carry):
        i = 2 * ii + 1
        carry = step(i, carry, st1_ref, st0_ref, p1_ref, p0_ref)
        return step(i + 1, carry, st0_ref, st1_ref, p0_ref, p1_ref)

    carry = lax.fori_loop(0, n_chunks // 2 - 1, body, carry)
    _, alpha = step(n_chunks - 1, carry, st1_ref, st0_ref, p1_ref, p0_ref, last=True)
    values(n_chunks - 1, p1_ref, alpha)

    ot = acc_ref[:HEAD_DIM, :] / acc_ref[HEAD_DIM:HEAD_DIM + 1, :]
    o = jnp.concatenate([ot, jnp.zeros_like(ot)], axis=0).T
    o_ref[...] = jnp.concatenate([o[g * tq:(g + 1) * tq, :HEAD_DIM] for g in range(Q_PER_KV)], axis=1)


def _attn_call(q, k, vt, batch, seq, tq):
    n = q.shape[0]
    qblocks = seq // tq
    gw = Q_PER_KV * HEAD_DIM
    tk = vt.shape[-1]
    chunks = seq // tk
    nq = Q_PER_KV * tq
    return pl.pallas_call(
        functools.partial(_attn_kernel, tq=tq),
        grid=(batch, N_KV_HEADS, qblocks),
        in_specs=[pl.BlockSpec((tq, gw), lambda b, j, i: (b * qblocks + i, j)),
                  pl.BlockSpec((1, seq, HEAD_DIM), lambda b, j, i: (j, b, 0)),
                  pl.BlockSpec((1, chunks, V_ROWS, tk), lambda b, j, i: (j, b, 0, 0))],
        out_specs=pl.BlockSpec((tq, gw), lambda b, j, i: (b * qblocks + i, j)),
        out_shape=jax.ShapeDtypeStruct((n, D_ATTN), F32),
        scratch_shapes=[pltpu.VMEM((nq, HEAD_DIM), BF16),
                        pltpu.VMEM((tk, nq), F32), pltpu.VMEM((tk, nq), F32),
                        pltpu.VMEM((tk, nq), BF16), pltpu.VMEM((tk, nq), BF16),
                        pltpu.VMEM((V_ROWS, nq), F32)],
        compiler_params=_params("parallel", "parallel", "parallel"),
        name="attention",
    )(q, k, vt)


def _mix_kernel(yf_ref, ya_ref, gf_ref, ga_ref, x_ref, wf_ref, wa_ref, wo_ref, g_ref, b_ref, o_ref, *, alpha):
    mixed = (gf_ref[...] * _dot(yf_ref[...].astype(BF16), wf_ref[...])
             + ga_ref[...] * _dot(ya_ref[...].astype(BF16), wa_ref[...]))
    z = alpha * x_ref[...] + _dot(mixed.astype(BF16), wo_ref[...])
    o_ref[...] = _layer_norm(z, g_ref[...], b_ref[...])


def _mix_call(yf, ya, gf, ga, x, wf, wa, wo, g, b, tm, alpha):
    n, d = x.shape
    row = lambda i: (i, 0)
    const = lambda i: (0, 0)
    return pl.pallas_call(
        functools.partial(_mix_kernel, alpha=alpha),
        grid=(n // tm,),
        in_specs=[pl.BlockSpec((tm, D_FOURIER), row), pl.BlockSpec((tm, D_ATTN), row),
                  pl.BlockSpec((tm, d), row), pl.BlockSpec((tm, d), row), pl.BlockSpec((tm, d), row),
                  pl.BlockSpec(wf.shape, const), pl.BlockSpec(wa.shape, const), pl.BlockSpec(wo.shape, const),
                  pl.BlockSpec((1, d), const), pl.BlockSpec((1, d), const)],
        out_specs=pl.BlockSpec((tm, d), row),
        out_shape=jax.ShapeDtypeStruct((n, d), F32),
        compiler_params=_params("parallel"),
        name="mix",
    )(yf, ya, gf, ga, x, wf, wa, wo, g.reshape(1, d), b.reshape(1, d))


_NEG_INF = float("-inf")


def _extract_top(s, order, count, exact):
    rank = jnp.full(s.shape, float(count), F32)
    big = float(2 ** 20)
    vals = []
    for a in range(count):
        m = jnp.max(s, axis=0, keepdims=True)
        if exact:
            first = jnp.min(jnp.where(s == m, order, big), axis=0, keepdims=True)
            sel = order == first
        else:
            sel = s == m
        rank = jnp.where(sel, float(a), rank)
        s = jnp.where(sel, _NEG_INF, s)
        vals.append(m)
    extracted = jnp.sum(jnp.where(rank < float(count), 1.0, 0.0), axis=0, keepdims=True)
    return vals, rank, jnp.where(extracted == float(count), 0.0, 1.0)


def _candidate_rows(v1, v2, combine):
    v2_all = jnp.concatenate(v2, axis=0)
    v2_lo = v2_all[:8]
    pieces = [combine(v1[0], v2_all)]
    pieces += [combine(v1[a], v2_lo) for a in range(1, 8)]
    pieces.append(combine(jnp.concatenate(v1[8:], axis=0), v2[0]))
    return jnp.concatenate(pieces, axis=0)


def _candidate_positions(tokens):
    pos = ([b for b in range(16)] + [a * 16 + b for a in range(1, 8) for b in range(8)]
           + [a * 16 for a in range(8, 16)])
    col = jnp.asarray(np.asarray(pos, np.float32).reshape(-1, 1))
    return jnp.broadcast_to(col, (len(pos), tokens))


def _peer_route_kernel(x_ref, wq_ref, keys_ref, pos_ref, g1_ref, g2_ref,
                       s_ref, vals_ref, rank_ref, o1_ref, o2_ref):
    tokens = x_ref.shape[0]
    lane_tiles = tokens // LANES
    xb = x_ref[...].astype(BF16)
    for h in range(PEER_HEADS):
        col = 2 * h * PEER_HALF
        qp = _dot(xb, wq_ref[:, col:col + 2 * PEER_HALF]).astype(BF16)
        for c in range(2):
            s = _dot_nt(keys_ref[h, c], qp[:, c * PEER_HALF:(c + 1) * PEER_HALF])
            for lt in range(lane_tiles):
                s_ref[(2 * h + c) * lane_tiles + lt] = s[:, lt * LANES:(lt + 1) * LANES]

    ambiguous = _route_select(pos_ref, s_ref, vals_ref, rank_ref, o1_ref, o2_ref, lane_tiles, exact=False)

    @pl.when(jnp.max(ambiguous) > 0.0)
    def _():
        _route_select(pos_ref, s_ref, vals_ref, rank_ref, o1_ref, o2_ref, lane_tiles, exact=True)

    for h in range(PEER_HEADS):
        for lt in range(lane_tiles):
            ls = slice(lt * LANES, (lt + 1) * LANES)
            u = h * lane_tiles + lt
            g1_ref[0, h, :, ls] = o1_ref[0, u]
            g1_ref[1, h, :, ls] = o1_ref[1, u]
            g2_ref[0, h, :, ls] = o2_ref[0, u]
            g2_ref[1, h, :, ls] = o2_ref[1, u]


def _route_select(pos_ref, s_ref, vals_ref, rank_ref, o1_ref, o2_ref, lane_tiles, exact):
    key_order = lax.broadcasted_iota(jnp.int32, (PEER_N_KEYS, LANES), 0).astype(F32)

    def top_keys(i, ambiguous):
        for k in range(ROUTE_UNITS_PER_ITER):
            unit = i * ROUTE_UNITS_PER_ITER + k
            vals, rank, amb = _extract_top(s_ref[unit], key_order, PEER_TOPK, exact)
            vals_ref[unit] = jnp.concatenate(vals, axis=0)
            rank_ref[unit] = rank
            ambiguous = jnp.maximum(ambiguous, amb)
        return ambiguous

    ambiguous = lax.fori_loop(0, PEER_HEADS * 2 * lane_tiles // ROUTE_UNITS_PER_ITER, top_keys,
                              jnp.zeros((1, LANES), F32))

    def combine_pair(i, ambiguous):
        return combine(2 * i + 1, combine(2 * i, ambiguous))

    def combine(u, ambiguous):
        h, lt = u // lane_tiles, u % lane_tiles
        u1 = (2 * h) * lane_tiles + lt
        u2 = (2 * h + 1) * lane_tiles + lt
        v1 = [vals_ref[u1, a:a + 1, :] for a in range(PEER_TOPK)]
        v2 = [vals_ref[u2, a:a + 1, :] for a in range(PEER_TOPK)]
        cand = _candidate_rows(v1, v2, lambda a, b: a + b)
        _, crank, amb = _extract_top(cand, pos_ref[...], PEER_TOPK, exact)
        chosen = crank < float(PEER_TOPK)
        e1 = [jnp.exp(v - v1[0]) for v in v1]
        e2 = [jnp.exp(v - v2[0]) for v in v2]
        ecand = _candidate_rows(e1, e2, lambda a, b: a * b)
        z = jnp.sum(jnp.where(chosen, ecand, 0.0), axis=0, keepdims=True)
        cnt = jnp.where(chosen, 1.0, 0.0)
        limit = [jnp.sum(cnt[0:16], axis=0, keepdims=True)]
        limit += [jnp.sum(cnt[8 + 8 * a:16 + 8 * a], axis=0, keepdims=True) for a in range(1, 8)]
        limit += [cnt[72 + a:73 + a] for a in range(8)]
        rank1 = rank_ref[u1]
        lim_dense = jnp.zeros(rank1.shape, F32)
        for a in range(PEER_TOPK):
            lim_dense = jnp.where(rank1 == float(a), limit[a], lim_dense)
        o1_ref[0, u] = jnp.exp(s_ref[u1] - v1[0]) / z
        o1_ref[1, u] = lim_dense
        o2_ref[0, u] = jnp.exp(s_ref[u2] - v2[0]).astype(BF16)
        o2_ref[1, u] = rank_ref[u2].astype(BF16)
        return jnp.maximum(ambiguous, amb)

    return lax.fori_loop(0, PEER_HEADS * lane_tiles // 2, combine_pair, ambiguous)


def _peer_route_call(x, wq, keys, tm):
    n, d = x.shape
    pos = _candidate_positions(LANES)
    units = PEER_HEADS * (tm // LANES)
    return pl.pallas_call(
        _peer_route_kernel,
        grid=(n // tm,),
        in_specs=[pl.BlockSpec((tm, d), lambda i: (i, 0)),
                  pl.BlockSpec(wq.shape, lambda i: (0, 0)),
                  pl.BlockSpec(keys.shape, lambda i: (0, 0, 0, 0)),
                  pl.BlockSpec(pos.shape, lambda i: (0, 0))],
        out_specs=[pl.BlockSpec((2, PEER_HEADS, PEER_N_KEYS, tm), lambda i: (0, 0, 0, i)),
                   pl.BlockSpec((2, PEER_HEADS, PEER_N_KEYS, tm), lambda i: (0, 0, 0, i))],
        out_shape=[jax.ShapeDtypeStruct((2, PEER_HEADS, PEER_N_KEYS, n), F32),
                   jax.ShapeDtypeStruct((2, PEER_HEADS, PEER_N_KEYS, n), BF16)],
        scratch_shapes=[pltpu.VMEM((2 * units, PEER_N_KEYS, LANES), F32),
                        pltpu.VMEM((2 * units, PEER_TOPK, LANES), F32),
                        pltpu.VMEM((2 * units, PEER_N_KEYS, LANES), F32),
                        pltpu.VMEM((2, units, PEER_N_KEYS, LANES), F32),
                        pltpu.VMEM((2, units, PEER_N_KEYS, LANES), BF16)],
        compiler_params=_params("parallel"),
        name="peer_route",
    )(x, wq, keys, pos)


def _peer_expert_kernel(x_ref, g1_ref, g2_ref, u_ref, vt_ref, g_ref, b_ref, o_ref, xb_ref, acc_ref, p_ref, *,
                        alpha, rows_per_step):
    j = pl.program_id(1)

    @pl.when(j == 0)
    def _():
        xb_ref[...] = x_ref[...].astype(BF16)
        acc_ref[...] = jnp.zeros_like(acc_ref)

    act = _dot_nt(u_ref[...], xb_ref[...])
    act = 0.5 * act * (1.0 + lax.erf(act * math.sqrt(0.5)))
    for r in range(rows_per_step):
        i1 = j * rows_per_step + r
        gate = None
        for h in range(PEER_HEADS):
            e1n = g1_ref[0, h, pl.ds(i1, 1), :].astype(BF16)
            lim = g1_ref[1, h, pl.ds(i1, 1), :].astype(BF16)
            term = jnp.where(g2_ref[1, h] < lim, g2_ref[0, h] * e1n, jnp.zeros((), BF16))
            gate = term if gate is None else gate + term
        lo, hi = r * PEER_N_KEYS, (r + 1) * PEER_N_KEYS
        p_ref[lo:hi, :] = gate * act[lo:hi, :].astype(BF16)
    acc_ref[...] += _dot(vt_ref[...], p_ref[...])

    @pl.when(j == pl.num_programs(1) - 1)
    def _():
        z = alpha * x_ref[...] + acc_ref[...].T
        o_ref[...] = _layer_norm(z, g_ref[...], b_ref[...])


def _peer_expert_call(x, g1, g2, u, vt, g, b, tm, rows_per_step, alpha):
    n, d = x.shape
    te = rows_per_step * PEER_N_KEYS
    n_exp = u.shape[0]
    return pl.pallas_call(
        functools.partial(_peer_expert_kernel, alpha=alpha, rows_per_step=rows_per_step),
        grid=(n // tm, n_exp // te),
        in_specs=[pl.BlockSpec((tm, d), lambda i, j: (i, 0)),
                  pl.BlockSpec((2, PEER_HEADS, PEER_N_KEYS, tm), lambda i, j: (0, 0, 0, i)),
                  pl.BlockSpec((2, PEER_HEADS, PEER_N_KEYS, tm), lambda i, j: (0, 0, 0, i)),
                  pl.BlockSpec((te, d), lambda i, j: (j, 0)),
                  pl.BlockSpec((d, te), lambda i, j: (0, j)),
                  pl.BlockSpec((1, d), lambda i, j: (0, 0)),
                  pl.BlockSpec((1, d), lambda i, j: (0, 0))],
        out_specs=pl.BlockSpec((tm, d), lambda i, j: (i, 0)),
        out_shape=jax.ShapeDtypeStruct((n, d), F32),
        scratch_shapes=[pltpu.VMEM((tm, d), BF16), pltpu.VMEM((d, tm), F32), pltpu.VMEM((te, tm), BF16)],
        compiler_params=_params("parallel", "arbitrary"),
        name="peer_experts",
    )(x, g1, g2, u, vt, g.reshape(1, d), b.reshape(1, d))


def _rope_tables(seq):
    rows = seq // GRID_W
    row_id = jnp.repeat(jnp.arange(rows, dtype=F32), GRID_W)
    col_id = jnp.tile(jnp.arange(GRID_W, dtype=F32), rows)
    inv_freq = 1.0 / (ROPE_THETA ** (jnp.arange(0, ROPE_AXIS_DIM, 2, dtype=F32) / ROPE_AXIS_DIM))
    ang_r = row_id[:, None] * inv_freq[None, :]
    ang_c = col_id[:, None] * inv_freq[None, :]
    ang = jnp.concatenate([ang_r, ang_r, ang_c, ang_c], axis=-1)
    sign = jnp.where((jnp.arange(HEAD_DIM) % ROPE_AXIS_DIM) < ROPE_HALF, -1.0, 1.0).astype(F32)
    return jnp.cos(ang), jnp.sin(ang) * sign[None, :]


def _head_mean_matrix(heads):
    m = np.kron(np.eye(heads), np.full((HEAD_DIM, HEAD_DIM), 1.0 / HEAD_DIM))
    return jnp.asarray(m, BF16)


TM_LN = 512
TM_IN = 512
DFT1_COLS = 2048
DFT2_K1 = 8
ATTN_TQ = 128
TM_MIX = 256
TM_ROUTE = 256
ROUTE_UNITS_PER_ITER = 4
TM_EXPERT = 512
EXPERT_ROWS_PER_STEP = 8


def kernel(x, ln0_g, ln0_b, w_in, b_gate, q_norm_g, k_norm_g, w_branch_fourier, w_branch_attn, w_out,
           ln1_g, ln1_b, peer_w_query, peer_sub_keys, peer_u, peer_v, ln2_g, ln2_b):
    batch, seq, d = x.shape
    n = batch * seq
    depth = w_in.shape[0]
    alpha = (2.0 * depth) ** 0.25
    n1 = seq // DFT_N2

    cos, sin_signed = _rope_tables(seq)
    cq, sq = jnp.tile(cos, (1, N_Q_HEADS)), jnp.tile(sin_signed, (1, N_Q_HEADS))
    ck, sk = jnp.tile(cos, (1, N_KV_HEADS)), jnp.tile(sin_signed, (1, N_KV_HEADS))
    hmq, hmk = _head_mean_matrix(N_Q_HEADS), _head_mean_matrix(N_KV_HEADS)
    cdft, m1, m3, tw_r, tw_i = _dft_constants(seq)
    tw_r = jnp.repeat(tw_r, D_FOURIER, axis=1)
    tw_i = jnp.repeat(tw_i, D_FOURIER, axis=1)

    xs = _ln_call(x.reshape(n, d), ln0_g, ln0_b, TM_LN)
    for l in range(depth):
        qg = jnp.tile(q_norm_g[l], N_Q_HEADS).reshape(1, D_ATTN)
        kg = jnp.tile(k_norm_g[l], N_KV_HEADS).reshape(1, D_KV)
        zr, zi, q, k, vt, gf, ga = _in_proj_call(
            xs, w_in[l].astype(BF16), b_gate[l], qg, kg, cq, sq, ck, sk, cdft, hmq, hmk, TM_IN, seq)
        a = _dft_stage1_call(zr.reshape(batch, n1, DFT_N2 * D_FOURIER), zi.reshape(batch, n1, DFT_N2 * D_FOURIER),
                             m1, tw_r, tw_i, DFT1_COLS)
        yf = _dft_stage2_call(a.reshape(batch, 2, n1, DFT_N2, D_FOURIER), m3, DFT2_K1)
        ya = _attn_call(q, k, vt, batch, seq, ATTN_TQ)
        xs = _mix_call(yf.reshape(n, D_FOURIER), ya, gf, ga, xs,
                       w_branch_fourier[l].astype(BF16), w_branch_attn[l].astype(BF16), w_out[l].astype(BF16),
                       ln1_g[l], ln1_b[l], TM_MIX, alpha)
        g1, g2 = _peer_route_call(xs, peer_w_query[l].astype(BF16), peer_sub_keys[l].astype(BF16), TM_ROUTE)
        xs = _peer_expert_call(xs, g1, g2, peer_u[l].astype(BF16), peer_v[l].astype(BF16).T,
                               ln2_g[l], ln2_b[l], TM_EXPERT, EXPERT_ROWS_PER_STEP, alpha)
    return xs.reshape(batch, seq, d)
```

```python
import functools
import math

import numpy as np
import jax
import jax.numpy as jnp
from jax import lax
from jax.experimental import pallas as pl
from jax.experimental.pallas import tpu as pltpu

F32 = jnp.float32
BF16 = jnp.bfloat16

D_MODEL = 1024
GRID_W = 64
N_FOURIER_GROUPS = 4
FOURIER_GROUP_DIM = 128
D_FOURIER = N_FOURIER_GROUPS * FOURIER_GROUP_DIM
N_Q_HEADS = 8
N_KV_HEADS = 2
Q_PER_KV = N_Q_HEADS // N_KV_HEADS
HEAD_DIM = 64
D_ATTN = N_Q_HEADS * HEAD_DIM
D_KV = N_KV_HEADS * HEAD_DIM
ROPE_AXIS_DIM = HEAD_DIM // 2
ROPE_HALF = ROPE_AXIS_DIM // 2
ROPE_THETA = 10000.0
LOG2_E = math.log2(math.e)
V_ROWS = HEAD_DIM + 16
PEER_HEADS = 8
PEER_N_KEYS = 128
PEER_TOPK = 16
PEER_HALF = 128
LN_EPS = 1e-5
RMS_EPS = 1e-6

_C_XF = 0
_C_Q = D_FOURIER
_C_K = _C_Q + D_ATTN
_C_V = _C_K + D_KV
_C_GF = _C_V + D_KV
_C_GA = _C_GF + D_MODEL
_C_END = _C_GA + D_MODEL

DFT_N2 = 128

VMEM_LIMIT_BYTES = 56 * 1024 * 1024


def _params(*sem):
    return pltpu.CompilerParams(dimension_semantics=sem, vmem_limit_bytes=VMEM_LIMIT_BYTES)


def _dot(a, b):
    return jnp.dot(a, b, preferred_element_type=F32)


def _dot_nt(a, b):
    return lax.dot_general(a, b, (((1,), (1,)), ((), ())), preferred_element_type=F32)


def _layer_norm(z, g, b):
    mu = jnp.mean(z, axis=-1, keepdims=True)
    zc = z - mu
    var = jnp.mean(zc * zc, axis=-1, keepdims=True)
    return zc * lax.rsqrt(var + LN_EPS) * g + b


def _ln_kernel(x_ref, g_ref, b_ref, o_ref):
    o_ref[...] = _layer_norm(x_ref[...], g_ref[...], b_ref[...])


def _ln_call(x, g, b, tm):
    n, d = x.shape
    return pl.pallas_call(
        _ln_kernel,
        grid=(n // tm,),
        in_specs=[pl.BlockSpec((tm, d), lambda i: (i, 0)),
                  pl.BlockSpec((1, d), lambda i: (0, 0)),
                  pl.BlockSpec((1, d), lambda i: (0, 0))],
        out_specs=pl.BlockSpec((tm, d), lambda i: (i, 0)),
        out_shape=jax.ShapeDtypeStruct((n, d), F32),
        compiler_params=_params("parallel"),
        name="ln0",
    )(x, g.reshape(1, d), b.reshape(1, d))


def _rms_rope(t, gain, head_mean, cos, sin_signed):
    sq = t * t
    hi = sq.astype(BF16)
    lo = (sq - hi.astype(F32)).astype(BF16)
    ms = _dot(hi, head_mean) + _dot(lo, head_mean)
    tn = t * lax.rsqrt(ms + RMS_EPS) * gain
    width = t.shape[-1]
    lane = lax.broadcasted_iota(jnp.int32, tn.shape, 1)
    first_half = (lane % ROPE_AXIS_DIM) < ROPE_HALF
    rot = jnp.where(first_half, pltpu.roll(tn, width - ROPE_HALF, 1), pltpu.roll(tn, ROPE_HALF, 1))
    return tn * cos + rot * sin_signed


def _in_proj_kernel(x_ref, w_ref, bg_ref, qg_ref, kg_ref, cq_ref, sq_ref, ck_ref, sk_ref,
                    cdft_ref, hmq_ref, hmk_ref,
                    zr_ref, zi_ref, q_ref, k_ref, vt_ref, gf_ref, ga_ref):
    xb = x_ref[...].astype(BF16)

    def seg(a, b):
        return _dot(xb, w_ref[:, a:b])

    xf = seg(_C_XF, _C_Q).astype(BF16)
    for g in range(N_FOURIER_GROUPS):
        lo, hi = g * FOURIER_GROUP_DIM, (g + 1) * FOURIER_GROUP_DIM
        z = _dot(xf[:, lo:hi], cdft_ref[...])
        zr_ref[:, lo:hi] = z[:, :FOURIER_GROUP_DIM]
        zi_ref[:, lo:hi] = z[:, FOURIER_GROUP_DIM:]

    q = _rms_rope(seg(_C_Q, _C_K), qg_ref[...], hmq_ref[...], cq_ref[...], sq_ref[...])
    q_ref[...] = (q * (HEAD_DIM ** -0.5 * LOG2_E)).astype(BF16)
    k = _rms_rope(seg(_C_K, _C_V), kg_ref[...], hmk_ref[...], ck_ref[...], sk_ref[...]).astype(BF16)
    vt = seg(_C_V, _C_GF).T.astype(BF16)
    pad_row = lax.broadcasted_iota(jnp.int32, (V_ROWS - HEAD_DIM, vt.shape[1]), 0)
    pad = jnp.where(pad_row == 0, 1.0, 0.0).astype(BF16)
    for j in range(N_KV_HEADS):
        k_ref[j] = k[:, j * HEAD_DIM:(j + 1) * HEAD_DIM]
        vt_ref[j, 0] = jnp.concatenate([vt[j * HEAD_DIM:(j + 1) * HEAD_DIM, :], pad], axis=0)

    gf_ref[...] = jax.nn.sigmoid(seg(_C_GF, _C_GA) + bg_ref[0:1, :])
    ga_ref[...] = jax.nn.sigmoid(seg(_C_GA, _C_END) + bg_ref[1:2, :])


def _in_proj_call(x, w, bg, qg, kg, cq, sq, ck, sk, cdft, hmq, hmk, tm, seq):
    n, d = x.shape
    pos_blocks = seq // tm
    row = lambda i: (i, 0)
    pos = lambda i: (i % pos_blocks, 0)
    const = lambda i: (0, 0)
    full = lambda a: pl.BlockSpec(a.shape, const)
    return pl.pallas_call(
        _in_proj_kernel,
        grid=(n // tm,),
        in_specs=[pl.BlockSpec((tm, d), row), full(w), full(bg), full(qg), full(kg),
                  pl.BlockSpec((tm, D_ATTN), pos), pl.BlockSpec((tm, D_ATTN), pos),
                  pl.BlockSpec((tm, D_KV), pos), pl.BlockSpec((tm, D_KV), pos),
                  full(cdft), full(hmq), full(hmk)],
        out_specs=[pl.BlockSpec((tm, D_FOURIER), row), pl.BlockSpec((tm, D_FOURIER), row),
                   pl.BlockSpec((tm, D_ATTN), row),
                   pl.BlockSpec((N_KV_HEADS, tm, HEAD_DIM), lambda i: (0, i, 0)),
                   pl.BlockSpec((N_KV_HEADS, 1, V_ROWS, tm), lambda i: (0, i, 0, 0)),
                   pl.BlockSpec((tm, d), row), pl.BlockSpec((tm, d), row)],
        out_shape=[jax.ShapeDtypeStruct((n, D_FOURIER), F32), jax.ShapeDtypeStruct((n, D_FOURIER), F32),
                   jax.ShapeDtypeStruct((n, D_ATTN), BF16),
                   jax.ShapeDtypeStruct((N_KV_HEADS, n, HEAD_DIM), BF16),
                   jax.ShapeDtypeStruct((N_KV_HEADS, n // tm, V_ROWS, tm), BF16),
                   jax.ShapeDtypeStruct((n, d), F32), jax.ShapeDtypeStruct((n, d), F32)],
        compiler_params=_params("parallel"),
        name="in_proj",
    )(x, w, bg, qg, kg, cq, sq, ck, sk, cdft, hmq, hmk)


def _dft_stage1_kernel(zr_ref, zi_ref, m1_ref, tr_ref, ti_ref, o_ref):
    n1 = zr_ref.shape[1]
    zz = jnp.concatenate([zr_ref[0], zi_ref[0]], axis=0).astype(BF16)
    a = _dot(m1_ref[...], zz)
    ar, ai = a[:n1], a[n1:]
    tr, ti = tr_ref[...], ti_ref[...]
    o_ref[0, 0] = ar * tr - ai * ti
    o_ref[0, 1] = ar * ti + ai * tr


def _dft_stage1_call(zr, zi, m1, tr, ti, cw):
    b, n1, cols = zr.shape
    blk = pl.BlockSpec((1, n1, cw), lambda i, j: (i, 0, j))
    tw = pl.BlockSpec((n1, cw), lambda i, j: (0, j))
    return pl.pallas_call(
        _dft_stage1_kernel,
        grid=(b, cols // cw),
        in_specs=[blk, blk, pl.BlockSpec(m1.shape, lambda i, j: (0, 0)), tw, tw],
        out_specs=pl.BlockSpec((1, 2, n1, cw), lambda i, j: (i, 0, 0, j)),
        out_shape=jax.ShapeDtypeStruct((b, 2, n1, cols), F32),
        compiler_params=_params("parallel", "parallel"),
        name="dft_stage1",
    )(zr, zi, m1, tr, ti)


def _dft_stage2_kernel(a_ref, m3_ref, o_ref):
    kb, n2, c = a_ref.shape[2], a_ref.shape[3], a_ref.shape[4]
    for j in range(kb):
        rhs = jnp.concatenate([a_ref[0, 0, j], a_ref[0, 1, j]], axis=0).astype(BF16)
        o_ref[0, :, j * c:(j + 1) * c] = _dot(m3_ref[...], rhs)


def _dft_stage2_call(a, m3, kb):
    b, _, n1, n2, c = a.shape
    return pl.pallas_call(
        _dft_stage2_kernel,
        grid=(b, n1 // kb),
        in_specs=[pl.BlockSpec((1, 2, kb, n2, c), lambda i, j: (i, 0, j, 0, 0)),
                  pl.BlockSpec(m3.shape, lambda i, j: (0, 0))],
        out_specs=pl.BlockSpec((1, n2, kb * c), lambda i, j: (i, 0, j)),
        out_shape=jax.ShapeDtypeStruct((b, n2, n1 * c), F32),
        compiler_params=_params("parallel", "parallel"),
        name="dft_stage2",
    )(a, m3)


def _dft_constants(seq):
    c = FOURIER_GROUP_DIM
    n1, n2 = seq // DFT_N2, DFT_N2
    jc = np.arange(c)
    ang = 2.0 * np.pi * np.outer(jc, jc) / c
    cdft = np.concatenate([np.cos(ang), -np.sin(ang)], axis=1) / math.sqrt(c)
    j1 = np.arange(n1)
    a1 = 2.0 * np.pi * np.outer(j1, j1) / n1
    fr, fi = np.cos(a1) / math.sqrt(n1), -np.sin(a1) / math.sqrt(n1)
    m1 = np.block([[fr, -fi], [fi, fr]])
    j2 = np.arange(n2)
    a2 = 2.0 * np.pi * np.outer(j2, j2) / n2
    m3 = np.concatenate([np.cos(a2), np.sin(a2)], axis=1) / math.sqrt(n2)
    at = 2.0 * np.pi * np.outer(j1, j2) / seq
    return (jnp.asarray(cdft, BF16), jnp.asarray(m1, BF16), jnp.asarray(m3, BF16),
            jnp.asarray(np.cos(at), F32), jnp.asarray(-np.sin(at), F32))


def _attn_kernel(q_ref, k_ref, vt_ref, o_ref, qs_ref, st0_ref, st1_ref, p0_ref, p1_ref, acc_ref, *, tq):
    n_chunks, _, tk = vt_ref.shape[1:]
    assert n_chunks % 2 == 0 and n_chunks >= 4
    nq = Q_PER_KV * tq
    for g in range(Q_PER_KV):
        qs_ref[g * tq:(g + 1) * tq, :] = q_ref[:, g * HEAD_DIM:(g + 1) * HEAD_DIM]
    acc_ref[...] = jnp.zeros_like(acc_ref)

    def scores(c, st_ref):
        start = pl.multiple_of(c * tk, tk)
        st_ref[...] = _dot_nt(k_ref[0, pl.ds(start, tk), :], qs_ref[...])

    def softmax(st_ref, p_ref, m):
        st = st_ref[...]
        m_new = jnp.maximum(m, jnp.max(st, axis=0, keepdims=True))
        p_ref[...] = jnp.exp2(st - m_new).astype(BF16)
        return m_new, jnp.exp2(m - m_new)

    def values(c, p_ref, alpha):
        acc_ref[...] = alpha * acc_ref[...] + _dot(vt_ref[0, c], p_ref[...])

    def step(i, carry, st_cur, st_nxt, p_cur, p_prv, last=False):
        m, alpha_prv = carry
        values(i - 1, p_prv, alpha_prv)
        m, alpha = softmax(st_cur, p_cur, m)
        if not last:
            scores(i + 1, st_nxt)
        return m, alpha

    scores(0, st0_ref)
    carry = softmax(st0_ref, p0_ref, jnp.full((1, nq), -jnp.inf, F32))
    scores(1, st1_ref)

    def body(ii, carry):
        i = 2 * ii + 1
        carry = step(i, carry, st1_ref, st0_ref, p1_ref, p0_ref)
        return step(i + 1, carry, st0_ref, st1_ref, p0_ref, p1_ref)

    carry = lax.fori_loop(0, n_chunks // 2 - 1, body, carry)
    _, alpha = step(n_chunks - 1, carry, st1_ref, st0_ref, p1_ref, p0_ref, last=True)
    values(n_chunks - 1, p1_ref, alpha)

    ot = acc_ref[:HEAD_DIM, :] / acc_ref[HEAD_DIM:HEAD_DIM + 1, :]
    o = jnp.concatenate([ot, jnp.zeros_like(ot)], axis=0).T
    o_ref[...] = jnp.concatenate([o[g * tq:(g + 1) * tq, :HEAD_DIM] for g in range(Q_PER_KV)], axis=1)


def _attn_call(q, k, vt, batch, seq, tq):
    n = q.shape[0]
    qblocks = seq // tq
    gw = Q_PER_KV * HEAD_DIM
    tk = vt.shape[-1]
    chunks = seq // tk
    nq = Q_PER_KV * tq
    return pl.pallas_call(
        functools.partial(_attn_kernel, tq=tq),
        grid=(batch, N_KV_HEADS, qblocks),
        in_specs=[pl.BlockSpec((tq, gw), lambda b, j, i: (b * qblocks + i, j)),
                  pl.BlockSpec((1, seq, HEAD_DIM), lambda b, j, i: (j, b, 0)),
                  pl.BlockSpec((1, chunks, V_ROWS, tk), lambda b, j, i: (j, b, 0, 0))],
        out_specs=pl.BlockSpec((tq, gw), lambda b, j, i: (b * qblocks + i, j)),
        out_shape=jax.ShapeDtypeStruct((n, D_ATTN), F32),
        scratch_shapes=[pltpu.VMEM((nq, HEAD_DIM), BF16),
                        pltpu.VMEM((tk, nq), F32), pltpu.VMEM((tk, nq), F32),
                        pltpu.VMEM((tk, nq), BF16), pltpu.VMEM((tk, nq), BF16),
                        pltpu.VMEM((V_ROWS, nq), F32)],
        compiler_params=_params("parallel", "parallel", "parallel"),
        name="attention",
    )(q, k, vt)


def _mix_kernel(yf_ref, ya_ref, gf_ref, ga_ref, x_ref, wf_ref, wa_ref, wo_ref, g_ref, b_ref, o_ref, *, alpha):
    mixed = (gf_ref[...] * _dot(yf_ref[...].astype(BF16), wf_ref[...])
             + ga_ref[...] * _dot(ya_ref[...].astype(BF16), wa_ref[...]))
    z = alpha * x_ref[...] + _dot(mixed.astype(BF16), wo_ref[...])
    o_ref[...] = _layer_norm(z, g_ref[...], b_ref[...])


def _mix_call(yf, ya, gf, ga, x, wf, wa, wo, g, b, tm, alpha):
    n, d = x.shape
    row = lambda i: (i, 0)
    const = lambda i: (0, 0)
    return pl.pallas_call(
        functools.partial(_mix_kernel, alpha=alpha),
        grid=(n // tm,),
        in_specs=[pl.BlockSpec((tm, D_FOURIER), row), pl.BlockSpec((tm, D_ATTN), row),
                  pl.BlockSpec((tm, d), row), pl.BlockSpec((tm, d), row), pl.BlockSpec((tm, d), row),
                  pl.BlockSpec(wf.shape, const), pl.BlockSpec(wa.shape, const), pl.BlockSpec(wo.shape, const),
                  pl.BlockSpec((1, d), const), pl.BlockSpec((1, d), const)],
        out_specs=pl.BlockSpec((tm, d), row),
        out_shape=jax.ShapeDtypeStruct((n, d), F32),
        compiler_params=_params("parallel"),
        name="mix",
    )(yf, ya, gf, ga, x, wf, wa, wo, g.reshape(1, d), b.reshape(1, d))


_NEG_INF = float("-inf")


def _extract_top(s, order, count, exact):
    rank = jnp.full(s.shape, float(count), F32)
    big = float(2 ** 20)
    vals = []
    for a in range(count):
        m = jnp.max(s, axis=0, keepdims=True)
        if exact:
            first = jnp.min(jnp.where(s == m, order, big), axis=0, keepdims=True)
            sel = order == first
        else:
            sel = s == m
        rank = jnp.where(sel, float(a), rank)
        s = jnp.where(sel, _NEG_INF, s)
        vals.append(m)
    extracted = jnp.sum(jnp.where(rank < float(count), 1.0, 0.0), axis=0, keepdims=True)
    return vals, rank, jnp.where(extracted == float(count), 0.0, 1.0)


def _candidate_rows(v1, v2, combine):
    v2_all = jnp.concatenate(v2, axis=0)
    v2_lo = v2_all[:8]
    pieces = [combine(v1[0], v2_all)]
    pieces += [combine(v1[a], v2_lo) for a in range(1, 8)]
    pieces.append(combine(jnp.concatenate(v1[8:], axis=0), v2[0]))
    return jnp.concatenate(pieces, axis=0)


def _candidate_positions(tokens):
    pos = ([b for b in range(16)] + [a * 16 + b for a in range(1, 8) for b in range(8)]
           + [a * 16 for a in range(8, 16)])
    col = jnp.asarray(np.asarray(pos, np.float32).reshape(-1, 1))
    return jnp.broadcast_to(col, (len(pos), tokens))


def _peer_route_kernel(x_ref, wq_ref, keys_ref, pos_ref, g1_ref, g2_ref):
    ambiguous = _route_tile(x_ref, wq_ref, keys_ref, pos_ref, g1_ref, g2_ref, exact=False)

    @pl.when(jnp.max(ambiguous) > 0.0)
    def _():
        _route_tile(x_ref, wq_ref, keys_ref, pos_ref, g1_ref, g2_ref, exact=True)


def _route_tile(x_ref, wq_ref, keys_ref, pos_ref, g1_ref, g2_ref, exact):
    tokens = x_ref.shape[0]
    xb = x_ref[...].astype(BF16)
    key_order = lax.broadcasted_iota(jnp.int32, (PEER_N_KEYS, tokens), 0).astype(F32)
    cand_pos = pos_ref[...]
    ambiguous = jnp.zeros((1, tokens), F32)
    for h in range(PEER_HEADS):
        col = 2 * h * PEER_HALF
        qp = _dot(xb, wq_ref[:, col:col + 2 * PEER_HALF]).astype(BF16)
        scores, vals, ranks = [], [], []
        for c in range(2):
            s = _dot_nt(keys_ref[h, c], qp[:, c * PEER_HALF:(c + 1) * PEER_HALF])
            v, r, amb = _extract_top(s, key_order, PEER_TOPK, exact)
            ambiguous = jnp.maximum(ambiguous, amb)
            scores.append(s)
            vals.append(v)
            ranks.append(r)
        v1, v2 = vals
        cand = _candidate_rows(v1, v2, lambda a, b: a + b)
        _, crank, amb = _extract_top(cand, cand_pos, PEER_TOPK, exact)
        ambiguous = jnp.maximum(ambiguous, amb)
        chosen = crank < float(PEER_TOPK)
        e1 = [jnp.exp(v - v1[0]) for v in v1]
        e2 = [jnp.exp(v - v2[0]) for v in v2]
        ecand = _candidate_rows(e1, e2, lambda a, b: a * b)
        z = jnp.sum(jnp.where(chosen, ecand, 0.0), axis=0, keepdims=True)
        cnt = jnp.where(chosen, 1.0, 0.0)
        limit = [jnp.sum(cnt[0:16], axis=0, keepdims=True)]
        limit += [jnp.sum(cnt[8 + 8 * a:16 + 8 * a], axis=0, keepdims=True) for a in range(1, 8)]
        limit += [cnt[72 + a:73 + a] for a in range(8)]
        lim_dense = jnp.zeros((PEER_N_KEYS, tokens), F32)
        for a in range(PEER_TOPK):
            lim_dense = jnp.where(ranks[0] == float(a), limit[a], lim_dense)
        g1_ref[0, h] = jnp.exp(scores[0] - v1[0]) / z
        g1_ref[1, h] = lim_dense
        g2_ref[0, h] = jnp.exp(scores[1] - v2[0]).astype(BF16)
        g2_ref[1, h] = ranks[1].astype(BF16)
    return ambiguous


def _peer_route_call(x, wq, keys, tm):
    n, d = x.shape
    pos = _candidate_positions(tm)
    return pl.pallas_call(
        _peer_route_kernel,
        grid=(n // tm,),
        in_specs=[pl.BlockSpec((tm, d), lambda i: (i, 0)),
                  pl.BlockSpec(wq.shape, lambda i: (0, 0)),
                  pl.BlockSpec(keys.shape, lambda i: (0, 0, 0, 0)),
                  pl.BlockSpec(pos.shape, lambda i: (0, 0))],
        out_specs=[pl.BlockSpec((2, PEER_HEADS, PEER_N_KEYS, tm), lambda i: (0, 0, 0, i)),
                   pl.BlockSpec((2, PEER_HEADS, PEER_N_KEYS, tm), lambda i: (0, 0, 0, i))],
        out_shape=[jax.ShapeDtypeStruct((2, PEER_HEADS, PEER_N_KEYS, n), F32),
                   jax.ShapeDtypeStruct((2, PEER_HEADS, PEER_N_KEYS, n), BF16)],
        compiler_params=_params("parallel"),
        name="peer_route",
    )(x, wq, keys, pos)


def _peer_expert_kernel(x_ref, g1_ref, g2_ref, u_ref, vt_ref, g_ref, b_ref, o_ref, xb_ref, acc_ref, p_ref, *,
                        alpha, rows_per_step):
    j = pl.program_id(1)

    @pl.when(j == 0)
    def _():
        xb_ref[...] = x_ref[...].astype(BF16)
        acc_ref[...] = jnp.zeros_like(acc_ref)

    act = _dot_nt(u_ref[...], xb_ref[...])
    act = 0.5 * act * (1.0 + lax.erf(act * math.sqrt(0.5)))
    for r in range(rows_per_step):
        i1 = j * rows_per_step + r
        gate = None
        for h in range(PEER_HEADS):
            e1n = g1_ref[0, h, pl.ds(i1, 1), :].astype(BF16)
            lim = g1_ref[1, h, pl.ds(i1, 1), :].astype(BF16)
            term = jnp.where(g2_ref[1, h] < lim, g2_ref[0, h] * e1n, jnp.zeros((), BF16))
            gate = term if gate is None else gate + term
        lo, hi = r * PEER_N_KEYS, (r + 1) * PEER_N_KEYS
        p_ref[lo:hi, :] = gate * act[lo:hi, :].astype(BF16)
    acc_ref[...] += _dot(vt_ref[...], p_ref[...])

    @pl.when(j == pl.num_programs(1) - 1)
    def _():
        z = alpha * x_ref[...] + acc_ref[...].T
        o_ref[...] = _layer_norm(z, g_ref[...], b_ref[...])


def _peer_expert_call(x, g1, g2, u, vt, g, b, tm, rows_per_step, alpha):
    n, d = x.shape
    te = rows_per_step * PEER_N_KEYS
    n_exp = u.shape[0]
    return pl.pallas_call(
        functools.partial(_peer_expert_kernel, alpha=alpha, rows_per_step=rows_per_step),
        grid=(n // tm, n_exp // te),
        in_specs=[pl.BlockSpec((tm, d), lambda i, j: (i, 0)),
                  pl.BlockSpec((2, PEER_HEADS, PEER_N_KEYS, tm), lambda i, j: (0, 0, 0, i)),
                  pl.BlockSpec((2, PEER_HEADS, PEER_N_KEYS, tm), lambda i, j: (0, 0, 0, i)),
                  pl.BlockSpec((te, d), lambda i, j: (j, 0)),
                  pl.BlockSpec((d, te), lambda i, j: (0, j)),
                  pl.BlockSpec((1, d), lambda i, j: (0, 0)),
                  pl.BlockSpec((1, d), lambda i, j: (0, 0))],
        out_specs=pl.BlockSpec((tm, d), lambda i, j: (i, 0)),
        out_shape=jax.ShapeDtypeStruct((n, d), F32),
        scratch_shapes=[pltpu.VMEM((tm, d), BF16), pltpu.VMEM((d, tm), F32), pltpu.VMEM((te, tm), BF16)],
        compiler_params=_params("parallel", "arbitrary"),
        name="peer_experts",
    )(x, g1, g2, u, vt, g.reshape(1, d), b.reshape(1, d))


def _rope_tables(seq):
    rows = seq // GRID_W
    row_id = jnp.repeat(jnp.arange(rows, dtype=F32), GRID_W)
    col_id = jnp.tile(jnp.arange(GRID_W, dtype=F32), rows)
    inv_freq = 1.0 / (ROPE_THETA ** (jnp.arange(0, ROPE_AXIS_DIM, 2, dtype=F32) / ROPE_AXIS_DIM))
    ang_r = row_id[:, None] * inv_freq[None, :]
    ang_c = col_id[:, None] * inv_freq[None, :]
    ang = jnp.concatenate([ang_r, ang_r, ang_c, ang_c], axis=-1)
    sign = jnp.where((jnp.arange(HEAD_DIM) % ROPE_AXIS_DIM) < ROPE_HALF, -1.0, 1.0).astype(F32)
    return jnp.cos(ang), jnp.sin(ang) * sign[None, :]


def _head_mean_matrix(heads):
    m = np.kron(np.eye(heads), np.full((HEAD_DIM, HEAD_DIM), 1.0 / HEAD_DIM))
    return jnp.asarray(m, BF16)


TM_LN = 512
TM_IN = 512
DFT1_COLS = 2048
DFT2_K1 = 8
ATTN_TQ = 128
TM_MIX = 256
TM_ROUTE = 256
TM_EXPERT = 512
EXPERT_ROWS_PER_STEP = 16


def kernel(x, ln0_g, ln0_b, w_in, b_gate, q_norm_g, k_norm_g, w_branch_fourier, w_branch_attn, w_out,
           ln1_g, ln1_b, peer_w_query, peer_sub_keys, peer_u, peer_v, ln2_g, ln2_b):
    batch, seq, d = x.shape
    n = batch * seq
    depth = w_in.shape[0]
    alpha = (2.0 * depth) ** 0.25
    n1 = seq // DFT_N2

    cos, sin_signed = _rope_tables(seq)
    cq, sq = jnp.tile(cos, (1, N_Q_HEADS)), jnp.tile(sin_signed, (1, N_Q_HEADS))
    ck, sk = jnp.tile(cos, (1, N_KV_HEADS)), jnp.tile(sin_signed, (1, N_KV_HEADS))
    hmq, hmk = _head_mean_matrix(N_Q_HEADS), _head_mean_matrix(N_KV_HEADS)
    cdft, m1, m3, tw_r, tw_i = _dft_constants(seq)
    tw_r = jnp.repeat(tw_r, D_FOURIER, axis=1)
    tw_i = jnp.repeat(tw_i, D_FOURIER, axis=1)

    xs = _ln_call(x.reshape(n, d), ln0_g, ln0_b, TM_LN)
    for l in range(depth):
        qg = jnp.tile(q_norm_g[l], N_Q_HEADS).reshape(1, D_ATTN)
        kg = jnp.tile(k_norm_g[l], N_KV_HEADS).reshape(1, D_KV)
        zr, zi, q, k, vt, gf, ga = _in_proj_call(
            xs, w_in[l].astype(BF16), b_gate[l], qg, kg, cq, sq, ck, sk, cdft, hmq, hmk, TM_IN, seq)
        a = _dft_stage1_call(zr.reshape(batch, n1, DFT_N2 * D_FOURIER), zi.reshape(batch, n1, DFT_N2 * D_FOURIER),
                             m1, tw_r, tw_i, DFT1_COLS)
        yf = _dft_stage2_call(a.reshape(batch, 2, n1, DFT_N2, D_FOURIER), m3, DFT2_K1)
        ya = _attn_call(q, k, vt, batch, seq, ATTN_TQ)
        xs = _mix_call(yf.reshape(n, D_FOURIER), ya, gf, ga, xs,
                       w_branch_fourier[l].astype(BF16), w_branch_attn[l].astype(BF16), w_out[l].astype(BF16),
                       ln1_g[l], ln1_b[l], TM_MIX, alpha)
        g1, g2 = _peer_route_call(xs, peer_w_query[l].astype(BF16), peer_sub_keys[l].astype(BF16), TM_ROUTE)
        xs = _peer_expert_call(xs, g1, g2, peer_u[l].astype(BF16), peer_v[l].astype(BF16).T,
                               ln2_g[l], ln2_b[l], TM_EXPERT, EXPERT_ROWS_PER_STEP, alpha)
    return xs.reshape(batch, seq, d)
```

```python
import functools
import math

import numpy as np
import jax
import jax.numpy as jnp
from jax import lax
from jax.experimental import pallas as pl
from jax.experimental.pallas import tpu as pltpu

F32 = jnp.float32
BF16 = jnp.bfloat16

D_MODEL = 1024
GRID_W = 64
N_FOURIER_GROUPS = 4
FOURIER_GROUP_DIM = 128
D_FOURIER = N_FOURIER_GROUPS * FOURIER_GROUP_DIM
N_Q_HEADS = 8
N_KV_HEADS = 2
Q_PER_KV = N_Q_HEADS // N_KV_HEADS
HEAD_DIM = 64
D_ATTN = N_Q_HEADS * HEAD_DIM
D_KV = N_KV_HEADS * HEAD_DIM
ROPE_AXIS_DIM = HEAD_DIM // 2
ROPE_HALF = ROPE_AXIS_DIM // 2
ROPE_THETA = 10000.0
LOG2_E = math.log2(math.e)
V_ROWS = HEAD_DIM + 16
PEER_HEADS = 8
PEER_N_KEYS = 128
PEER_TOPK = 16
PEER_HALF = 128
LN_EPS = 1e-5
RMS_EPS = 1e-6

_C_XF = 0
_C_Q = D_FOURIER
_C_K = _C_Q + D_ATTN
_C_V = _C_K + D_KV
_C_GF = _C_V + D_KV
_C_GA = _C_GF + D_MODEL
_C_END = _C_GA + D_MODEL

DFT_N2 = 128

VMEM_LIMIT_BYTES = 56 * 1024 * 1024


def _params(*sem):
    return pltpu.CompilerParams(dimension_semantics=sem, vmem_limit_bytes=VMEM_LIMIT_BYTES)


def _dot(a, b):
    return jnp.dot(a, b, preferred_element_type=F32)


def _dot_nt(a, b):
    return lax.dot_general(a, b, (((1,), (1,)), ((), ())), preferred_element_type=F32)


def _layer_norm(z, g, b):
    mu = jnp.mean(z, axis=-1, keepdims=True)
    zc = z - mu
    var = jnp.mean(zc * zc, axis=-1, keepdims=True)
    return zc * lax.rsqrt(var + LN_EPS) * g + b


def _ln_kernel(x_ref, g_ref, b_ref, o_ref):
    o_ref[...] = _layer_norm(x_ref[...], g_ref[...], b_ref[...])


def _ln_call(x, g, b, tm):
    n, d = x.shape
    return pl.pallas_call(
        _ln_kernel,
        grid=(n // tm,),
        in_specs=[pl.BlockSpec((tm, d), lambda i: (i, 0)),
                  pl.BlockSpec((1, d), lambda i: (0, 0)),
                  pl.BlockSpec((1, d), lambda i: (0, 0))],
        out_specs=pl.BlockSpec((tm, d), lambda i: (i, 0)),
        out_shape=jax.ShapeDtypeStruct((n, d), F32),
        compiler_params=_params("parallel"),
        name="ln0",
    )(x, g.reshape(1, d), b.reshape(1, d))


def _rms_rope(t, gain, head_mean, cos, sin_signed):
    sq = t * t
    hi = sq.astype(BF16)
    lo = (sq - hi.astype(F32)).astype(BF16)
    ms = _dot(hi, head_mean) + _dot(lo, head_mean)
    tn = t * lax.rsqrt(ms + RMS_EPS) * gain
    width = t.shape[-1]
    lane = lax.broadcasted_iota(jnp.int32, tn.shape, 1)
    first_half = (lane % ROPE_AXIS_DIM) < ROPE_HALF
    rot = jnp.where(first_half, pltpu.roll(tn, width - ROPE_HALF, 1), pltpu.roll(tn, ROPE_HALF, 1))
    return tn * cos + rot * sin_signed


def _in_proj_kernel(x_ref, w_ref, bg_ref, qg_ref, kg_ref, cq_ref, sq_ref, ck_ref, sk_ref,
                    cdft_ref, hmq_ref, hmk_ref,
                    zr_ref, zi_ref, q_ref, k_ref, vt_ref, gf_ref, ga_ref):
    xb = x_ref[...].astype(BF16)

    def seg(a, b):
        return _dot(xb, w_ref[:, a:b])

    xf = seg(_C_XF, _C_Q).astype(BF16)
    for g in range(N_FOURIER_GROUPS):
        lo, hi = g * FOURIER_GROUP_DIM, (g + 1) * FOURIER_GROUP_DIM
        z = _dot(xf[:, lo:hi], cdft_ref[...])
        zr_ref[:, lo:hi] = z[:, :FOURIER_GROUP_DIM]
        zi_ref[:, lo:hi] = z[:, FOURIER_GROUP_DIM:]

    q = _rms_rope(seg(_C_Q, _C_K), qg_ref[...], hmq_ref[...], cq_ref[...], sq_ref[...])
    q_ref[...] = (q * (HEAD_DIM ** -0.5 * LOG2_E)).astype(BF16)
    k = _rms_rope(seg(_C_K, _C_V), kg_ref[...], hmk_ref[...], ck_ref[...], sk_ref[...]).astype(BF16)
    vt = seg(_C_V, _C_GF).T.astype(BF16)
    pad_row = lax.broadcasted_iota(jnp.int32, (V_ROWS - HEAD_DIM, vt.shape[1]), 0)
    pad = jnp.where(pad_row == 0, 1.0, 0.0).astype(BF16)
    for j in range(N_KV_HEADS):
        k_ref[j] = k[:, j * HEAD_DIM:(j + 1) * HEAD_DIM]
        vt_ref[j, 0] = jnp.concatenate([vt[j * HEAD_DIM:(j + 1) * HEAD_DIM, :], pad], axis=0)

    gf_ref[...] = jax.nn.sigmoid(seg(_C_GF, _C_GA) + bg_ref[0:1, :])
    ga_ref[...] = jax.nn.sigmoid(seg(_C_GA, _C_END) + bg_ref[1:2, :])


def _in_proj_call(x, w, bg, qg, kg, cq, sq, ck, sk, cdft, hmq, hmk, tm, seq):
    n, d = x.shape
    pos_blocks = seq // tm
    row = lambda i: (i, 0)
    pos = lambda i: (i % pos_blocks, 0)
    const = lambda i: (0, 0)
    full = lambda a: pl.BlockSpec(a.shape, const)
    return pl.pallas_call(
        _in_proj_kernel,
        grid=(n // tm,),
        in_specs=[pl.BlockSpec((tm, d), row), full(w), full(bg), full(qg), full(kg),
                  pl.BlockSpec((tm, D_ATTN), pos), pl.BlockSpec((tm, D_ATTN), pos),
                  pl.BlockSpec((tm, D_KV), pos), pl.BlockSpec((tm, D_KV), pos),
                  full(cdft), full(hmq), full(hmk)],
        out_specs=[pl.BlockSpec((tm, D_FOURIER), row), pl.BlockSpec((tm, D_FOURIER), row),
                   pl.BlockSpec((tm, D_ATTN), row),
                   pl.BlockSpec((N_KV_HEADS, tm, HEAD_DIM), lambda i: (0, i, 0)),
                   pl.BlockSpec((N_KV_HEADS, 1, V_ROWS, tm), lambda i: (0, i, 0, 0)),
                   pl.BlockSpec((tm, d), row), pl.BlockSpec((tm, d), row)],
        out_shape=[jax.ShapeDtypeStruct((n, D_FOURIER), F32), jax.ShapeDtypeStruct((n, D_FOURIER), F32),
                   jax.ShapeDtypeStruct((n, D_ATTN), BF16),
                   jax.ShapeDtypeStruct((N_KV_HEADS, n, HEAD_DIM), BF16),
                   jax.ShapeDtypeStruct((N_KV_HEADS, n // tm, V_ROWS, tm), BF16),
                   jax.ShapeDtypeStruct((n, d), F32), jax.ShapeDtypeStruct((n, d), F32)],
        compiler_params=_params("parallel"),
        name="in_proj",
    )(x, w, bg, qg, kg, cq, sq, ck, sk, cdft, hmq, hmk)


def _dft_stage1_kernel(zr_ref, zi_ref, m1_ref, tr_ref, ti_ref, o_ref):
    n1 = zr_ref.shape[1]
    zz = jnp.concatenate([zr_ref[0], zi_ref[0]], axis=0).astype(BF16)
    a = _dot(m1_ref[...], zz)
    ar, ai = a[:n1], a[n1:]
    tr, ti = tr_ref[...], ti_ref[...]
    o_ref[0, 0] = ar * tr - ai * ti
    o_ref[0, 1] = ar * ti + ai * tr


def _dft_stage1_call(zr, zi, m1, tr, ti, cw):
    b, n1, cols = zr.shape
    blk = pl.BlockSpec((1, n1, cw), lambda i, j: (i, 0, j))
    tw = pl.BlockSpec((n1, cw), lambda i, j: (0, j))
    return pl.pallas_call(
        _dft_stage1_kernel,
        grid=(b, cols // cw),
        in_specs=[blk, blk, pl.BlockSpec(m1.shape, lambda i, j: (0, 0)), tw, tw],
        out_specs=pl.BlockSpec((1, 2, n1, cw), lambda i, j: (i, 0, 0, j)),
        out_shape=jax.ShapeDtypeStruct((b, 2, n1, cols), F32),
        compiler_params=_params("parallel", "parallel"),
        name="dft_stage1",
    )(zr, zi, m1, tr, ti)


def _dft_stage2_kernel(a_ref, m3_ref, o_ref):
    kb, n2, c = a_ref.shape[2], a_ref.shape[3], a_ref.shape[4]
    for j in range(kb):
        rhs = jnp.concatenate([a_ref[0, 0, j], a_ref[0, 1, j]], axis=0).astype(BF16)
        o_ref[0, :, j * c:(j + 1) * c] = _dot(m3_ref[...], rhs)


def _dft_stage2_call(a, m3, kb):
    b, _, n1, n2, c = a.shape
    return pl.pallas_call(
        _dft_stage2_kernel,
        grid=(b, n1 // kb),
        in_specs=[pl.BlockSpec((1, 2, kb, n2, c), lambda i, j: (i, 0, j, 0, 0)),
                  pl.BlockSpec(m3.shape, lambda i, j: (0, 0))],
        out_specs=pl.BlockSpec((1, n2, kb * c), lambda i, j: (i, 0, j)),
        out_shape=jax.ShapeDtypeStruct((b, n2, n1 * c), F32),
        compiler_params=_params("parallel", "parallel"),
        name="dft_stage2",
    )(a, m3)


def _dft_constants(seq):
    c = FOURIER_GROUP_DIM
    n1, n2 = seq // DFT_N2, DFT_N2
    jc = np.arange(c)
    ang = 2.0 * np.pi * np.outer(jc, jc) / c
    cdft = np.concatenate([np.cos(ang), -np.sin(ang)], axis=1) / math.sqrt(c)
    j1 = np.arange(n1)
    a1 = 2.0 * np.pi * np.outer(j1, j1) / n1
    fr, fi = np.cos(a1) / math.sqrt(n1), -np.sin(a1) / math.sqrt(n1)
    m1 = np.block([[fr, -fi], [fi, fr]])
    j2 = np.arange(n2)
    a2 = 2.0 * np.pi * np.outer(j2, j2) / n2
    m3 = np.concatenate([np.cos(a2), np.sin(a2)], axis=1) / math.sqrt(n2)
    at = 2.0 * np.pi * np.outer(j1, j2) / seq
    return (jnp.asarray(cdft, BF16), jnp.asarray(m1, BF16), jnp.asarray(m3, BF16),
            jnp.asarray(np.cos(at), F32), jnp.asarray(-np.sin(at), F32))


def _attn_kernel(q_ref, k_ref, vt_ref, o_ref, qs_ref, st0_ref, st1_ref, p0_ref, p1_ref, acc_ref, *, tq):
    n_chunks, _, tk = vt_ref.shape[1:]
    assert n_chunks % 2 == 0 and n_chunks >= 4
    nq = Q_PER_KV * tq
    for g in range(Q_PER_KV):
        qs_ref[g * tq:(g + 1) * tq, :] = q_ref[:, g * HEAD_DIM:(g + 1) * HEAD_DIM]
    acc_ref[...] = jnp.zeros_like(acc_ref)

    def scores(c, st_ref):
        start = pl.multiple_of(c * tk, tk)
        st_ref[...] = _dot_nt(k_ref[0, pl.ds(start, tk), :], qs_ref[...])

    def softmax(st_ref, p_ref, m):
        st = st_ref[...]
        m_new = jnp.maximum(m, jnp.max(st, axis=0, keepdims=True))
        p_ref[...] = jnp.exp2(st - m_new).astype(BF16)
        return m_new, jnp.exp2(m - m_new)

    def values(c, p_ref, alpha):
        acc_ref[...] = alpha * acc_ref[...] + _dot(vt_ref[0, c], p_ref[...])

    def step(i, carry, st_cur, st_nxt, p_cur, p_prv, last=False):
        m, alpha_prv = carry
        values(i - 1, p_prv, alpha_prv)
        m, alpha = softmax(st_cur, p_cur, m)
        if not last:
            scores(i + 1, st_nxt)
        return m, alpha

    scores(0, st0_ref)
    carry = softmax(st0_ref, p0_ref, jnp.full((1, nq), -jnp.inf, F32))
    scores(1, st1_ref)

    def body(ii, carry):
        i = 2 * ii + 1
        carry = step(i, carry, st1_ref, st0_ref, p1_ref, p0_ref)
        return step(i + 1, carry, st0_ref, st1_ref, p0_ref, p1_ref)

    carry = lax.fori_loop(0, n_chunks // 2 - 1, body, carry)
    _, alpha = step(n_chunks - 1, carry, st1_ref, st0_ref, p1_ref, p0_ref, last=True)
    values(n_chunks - 1, p1_ref, alpha)

    ot = acc_ref[:HEAD_DIM, :] / acc_ref[HEAD_DIM:HEAD_DIM + 1, :]
    o = jnp.concatenate([ot, jnp.zeros_like(ot)], axis=0).T
    o_ref[...] = jnp.concatenate([o[g * tq:(g + 1) * tq, :HEAD_DIM] for g in range(Q_PER_KV)], axis=1)


def _attn_call(q, k, vt, batch, seq, tq):
    n = q.shape[0]
    qblocks = seq // tq
    gw = Q_PER_KV * HEAD_DIM
    tk = vt.shape[-1]
    chunks = seq // tk
    nq = Q_PER_KV * tq
    return pl.pallas_call(
        functools.partial(_attn_kernel, tq=tq),
        grid=(batch, N_KV_HEADS, qblocks),
        in_specs=[pl.BlockSpec((tq, gw), lambda b, j, i: (b * qblocks + i, j)),
                  pl.BlockSpec((1, seq, HEAD_DIM), lambda b, j, i: (j, b, 0)),
                  pl.BlockSpec((1, chunks, V_ROWS, tk), lambda b, j, i: (j, b, 0, 0))],
        out_specs=pl.BlockSpec((tq, gw), lambda b, j, i: (b * qblocks + i, j)),
        out_shape=jax.ShapeDtypeStruct((n, D_ATTN), F32),
        scratch_shapes=[pltpu.VMEM((nq, HEAD_DIM), BF16),
                        pltpu.VMEM((tk, nq), F32), pltpu.VMEM((tk, nq), F32),
                        pltpu.VMEM((tk, nq), BF16), pltpu.VMEM((tk, nq), BF16),
                        pltpu.VMEM((V_ROWS, nq), F32)],
        compiler_params=_params("parallel", "parallel", "parallel"),
        name="attention",
    )(q, k, vt)


def _mix_kernel(yf_ref, ya_ref, gf_ref, ga_ref, x_ref, wf_ref, wa_ref, wo_ref, g_ref, b_ref, o_ref, *, alpha):
    mixed = (gf_ref[...] * _dot(yf_ref[...].astype(BF16), wf_ref[...])
             + ga_ref[...] * _dot(ya_ref[...].astype(BF16), wa_ref[...]))
    z = alpha * x_ref[...] + _dot(mixed.astype(BF16), wo_ref[...])
    o_ref[...] = _layer_norm(z, g_ref[...], b_ref[...])


def _mix_call(yf, ya, gf, ga, x, wf, wa, wo, g, b, tm, alpha):
    n, d = x.shape
    row = lambda i: (i, 0)
    const = lambda i: (0, 0)
    return pl.pallas_call(
        functools.partial(_mix_kernel, alpha=alpha),
        grid=(n // tm,),
        in_specs=[pl.BlockSpec((tm, D_FOURIER), row), pl.BlockSpec((tm, D_ATTN), row),
                  pl.BlockSpec((tm, d), row), pl.BlockSpec((tm, d), row), pl.BlockSpec((tm, d), row),
                  pl.BlockSpec(wf.shape, const), pl.BlockSpec(wa.shape, const), pl.BlockSpec(wo.shape, const),
                  pl.BlockSpec((1, d), const), pl.BlockSpec((1, d), const)],
        out_specs=pl.BlockSpec((tm, d), row),
        out_shape=jax.ShapeDtypeStruct((n, d), F32),
        compiler_params=_params("parallel"),
        name="mix",
    )(yf, ya, gf, ga, x, wf, wa, wo, g.reshape(1, d), b.reshape(1, d))


_NEG_INF = float("-inf")


def _extract_top(s, order, count, exact):
    rank = jnp.full(s.shape, float(count), F32)
    big = float(2 ** 20)
    vals = []
    for a in range(count):
        m = jnp.max(s, axis=0, keepdims=True)
        if exact:
            first = jnp.min(jnp.where(s == m, order, big), axis=0, keepdims=True)
            sel = order == first
        else:
            sel = s == m
        rank = jnp.where(sel, float(a), rank)
        s = jnp.where(sel, _NEG_INF, s)
        vals.append(m)
    extracted = jnp.sum(jnp.where(rank < float(count), 1.0, 0.0), axis=0, keepdims=True)
    return vals, rank, jnp.where(extracted == float(count), 0.0, 1.0)


def _candidate_rows(v1, v2, combine):
    v2_all = jnp.concatenate(v2, axis=0)
    v2_lo = v2_all[:8]
    pieces = [combine(v1[0], v2_all)]
    pieces += [combine(v1[a], v2_lo) for a in range(1, 8)]
    pieces.append(combine(jnp.concatenate(v1[8:], axis=0), v2[0]))
    return jnp.concatenate(pieces, axis=0)


def _candidate_positions(tokens):
    pos = ([b for b in range(16)] + [a * 16 + b for a in range(1, 8) for b in range(8)]
           + [a * 16 for a in range(8, 16)])
    col = jnp.asarray(np.asarray(pos, np.float32).reshape(-1, 1))
    return jnp.broadcast_to(col, (len(pos), tokens))


def _peer_route_kernel(x_ref, wq_ref, keys_ref, pos_ref, g1_ref, g2_ref):
    ambiguous = _route_tile(x_ref, wq_ref, keys_ref, pos_ref, g1_ref, g2_ref, exact=False)

    @pl.when(jnp.max(ambiguous) > 0.0)
    def _():
        _route_tile(x_ref, wq_ref, keys_ref, pos_ref, g1_ref, g2_ref, exact=True)


def _route_tile(x_ref, wq_ref, keys_ref, pos_ref, g1_ref, g2_ref, exact):
    tokens = x_ref.shape[0]
    xb = x_ref[...].astype(BF16)
    key_order = lax.broadcasted_iota(jnp.int32, (PEER_N_KEYS, tokens), 0).astype(F32)
    cand_pos = pos_ref[...]
    ambiguous = jnp.zeros((1, tokens), F32)
    for h in range(PEER_HEADS):
        col = 2 * h * PEER_HALF
        qp = _dot(xb, wq_ref[:, col:col + 2 * PEER_HALF]).astype(BF16)
        scores, vals, ranks = [], [], []
        for c in range(2):
            s = _dot_nt(keys_ref[h, c], qp[:, c * PEER_HALF:(c + 1) * PEER_HALF])
            v, r, amb = _extract_top(s, key_order, PEER_TOPK, exact)
            ambiguous = jnp.maximum(ambiguous, amb)
            scores.append(s)
            vals.append(v)
            ranks.append(r)
        v1, v2 = vals
        cand = _candidate_rows(v1, v2, lambda a, b: a + b)
        _, crank, amb = _extract_top(cand, cand_pos, PEER_TOPK, exact)
        ambiguous = jnp.maximum(ambiguous, amb)
        chosen = crank < float(PEER_TOPK)
        e1 = [jnp.exp(v - v1[0]) for v in v1]
        e2 = [jnp.exp(v - v2[0]) for v in v2]
        ecand = _candidate_rows(e1, e2, lambda a, b: a * b)
        z = jnp.sum(jnp.where(chosen, ecand, 0.0), axis=0, keepdims=True)
        cnt = jnp.where(chosen, 1.0, 0.0)
        limit = [jnp.sum(cnt[0:16], axis=0, keepdims=True)]
        limit += [jnp.sum(cnt[8 + 8 * a:16 + 8 * a], axis=0, keepdims=True) for a in range(1, 8)]
        limit += [cnt[72 + a:73 + a] for a in range(8)]
        lim_dense = jnp.zeros((PEER_N_KEYS, tokens), F32)
        for a in range(PEER_TOPK):
            lim_dense = jnp.where(ranks[0] == float(a), limit[a], lim_dense)
        g1_ref[0, h] = jnp.exp(scores[0] - v1[0]) / z
        g1_ref[1, h] = lim_dense
        g2_ref[0, h] = jnp.exp(scores[1] - v2[0]).astype(BF16)
        g2_ref[1, h] = ranks[1].astype(BF16)
    return ambiguous


def _peer_route_call(x, wq, keys, tm):
    n, d = x.shape
    pos = _candidate_positions(tm)
    return pl.pallas_call(
        _peer_route_kernel,
        grid=(n // tm,),
        in_specs=[pl.BlockSpec((tm, d), lambda i: (i, 0)),
                  pl.BlockSpec(wq.shape, lambda i: (0, 0)),
                  pl.BlockSpec(keys.shape, lambda i: (0, 0, 0, 0)),
                  pl.BlockSpec(pos.shape, lambda i: (0, 0))],
        out_specs=[pl.BlockSpec((2, PEER_HEADS, PEER_N_KEYS, tm), lambda i: (0, 0, 0, i)),
                   pl.BlockSpec((2, PEER_HEADS, PEER_N_KEYS, tm), lambda i: (0, 0, 0, i))],
        out_shape=[jax.ShapeDtypeStruct((2, PEER_HEADS, PEER_N_KEYS, n), F32),
                   jax.ShapeDtypeStruct((2, PEER_HEADS, PEER_N_KEYS, n), BF16)],
        compiler_params=_params("parallel"),
        name="peer_route",
    )(x, wq, keys, pos)


def _peer_expert_kernel(x_ref, g1_ref, g2_ref, u_ref, vt_ref, g_ref, b_ref, o_ref, xb_ref, acc_ref, p_ref, *,
                        alpha, rows_per_step):
    j = pl.program_id(1)

    @pl.when(j == 0)
    def _():
        xb_ref[...] = x_ref[...].astype(BF16)
        acc_ref[...] = jnp.zeros_like(acc_ref)

    act = _dot_nt(u_ref[...], xb_ref[...]).astype(BF16)
    act = 0.5 * act * (1.0 + lax.erf(act * math.sqrt(0.5)))
    for r in range(rows_per_step):
        i1 = j * rows_per_step + r
        gate = None
        for h in range(PEER_HEADS):
            e1n = g1_ref[0, h, pl.ds(i1, 1), :].astype(BF16)
            lim = g1_ref[1, h, pl.ds(i1, 1), :].astype(BF16)
            term = jnp.where(g2_ref[1, h] < lim, g2_ref[0, h] * e1n, jnp.zeros((), BF16))
            gate = term if gate is None else gate + term
        lo, hi = r * PEER_N_KEYS, (r + 1) * PEER_N_KEYS
        p_ref[lo:hi, :] = gate * act[lo:hi, :]
    acc_ref[...] += _dot(vt_ref[...], p_ref[...])

    @pl.when(j == pl.num_programs(1) - 1)
    def _():
        z = alpha * x_ref[...] + acc_ref[...].T
        o_ref[...] = _layer_norm(z, g_ref[...], b_ref[...])


def _peer_expert_call(x, g1, g2, u, vt, g, b, tm, rows_per_step, alpha):
    n, d = x.shape
    te = rows_per_step * PEER_N_KEYS
    n_exp = u.shape[0]
    return pl.pallas_call(
        functools.partial(_peer_expert_kernel, alpha=alpha, rows_per_step=rows_per_step),
        grid=(n // tm, n_exp // te),
        in_specs=[pl.BlockSpec((tm, d), lambda i, j: (i, 0)),
                  pl.BlockSpec((2, PEER_HEADS, PEER_N_KEYS, tm), lambda i, j: (0, 0, 0, i)),
                  pl.BlockSpec((2, PEER_HEADS, PEER_N_KEYS, tm), lambda i, j: (0, 0, 0, i)),
                  pl.BlockSpec((te, d), lambda i, j: (j, 0)),
                  pl.BlockSpec((d, te), lambda i, j: (0, j)),
                  pl.BlockSpec((1, d), lambda i, j: (0, 0)),
                  pl.BlockSpec((1, d), lambda i, j: (0, 0))],
        out_specs=pl.BlockSpec((tm, d), lambda i, j: (i, 0)),
        out_shape=jax.ShapeDtypeStruct((n, d), F32),
        scratch_shapes=[pltpu.VMEM((tm, d), BF16), pltpu.VMEM((d, tm), F32), pltpu.VMEM((te, tm), BF16)],
        compiler_params=_params("parallel", "arbitrary"),
        name="peer_experts",
    )(x, g1, g2, u, vt, g.reshape(1, d), b.reshape(1, d))


def _rope_tables(seq):
    rows = seq // GRID_W
    row_id = jnp.repeat(jnp.arange(rows, dtype=F32), GRID_W)
    col_id = jnp.tile(jnp.arange(GRID_W, dtype=F32), rows)
    inv_freq = 1.0 / (ROPE_THETA ** (jnp.arange(0, ROPE_AXIS_DIM, 2, dtype=F32) / ROPE_AXIS_DIM))
    ang_r = row_id[:, None] * inv_freq[None, :]
    ang_c = col_id[:, None] * inv_freq[None, :]
    ang = jnp.concatenate([ang_r, ang_r, ang_c, ang_c], axis=-1)
    sign = jnp.where((jnp.arange(HEAD_DIM) % ROPE_AXIS_DIM) < ROPE_HALF, -1.0, 1.0).astype(F32)
    return jnp.cos(ang), jnp.sin(ang) * sign[None, :]


def _head_mean_matrix(heads):
    m = np.kron(np.eye(heads), np.full((HEAD_DIM, HEAD_DIM), 1.0 / HEAD_DIM))
    return jnp.asarray(m, BF16)


TM_LN = 512
TM_IN = 512
DFT1_COLS = 2048
DFT2_K1 = 8
ATTN_TQ = 128
TM_MIX = 256
TM_ROUTE = 256
TM_EXPERT = 512
EXPERT_ROWS_PER_STEP = 16


def kernel(x, ln0_g, ln0_b, w_in, b_gate, q_norm_g, k_norm_g, w_branch_fourier, w_branch_attn, w_out,
           ln1_g, ln1_b, peer_w_query, peer_sub_keys, peer_u, peer_v, ln2_g, ln2_b):
    batch, seq, d = x.shape
    n = batch * seq
    depth = w_in.shape[0]
    alpha = (2.0 * depth) ** 0.25
    n1 = seq // DFT_N2

    cos, sin_signed = _rope_tables(seq)
    cq, sq = jnp.tile(cos, (1, N_Q_HEADS)), jnp.tile(sin_signed, (1, N_Q_HEADS))
    ck, sk = jnp.tile(cos, (1, N_KV_HEADS)), jnp.tile(sin_signed, (1, N_KV_HEADS))
    hmq, hmk = _head_mean_matrix(N_Q_HEADS), _head_mean_matrix(N_KV_HEADS)
    cdft, m1, m3, tw_r, tw_i = _dft_constants(seq)
    tw_r = jnp.repeat(tw_r, D_FOURIER, axis=1)
    tw_i = jnp.repeat(tw_i, D_FOURIER, axis=1)

    xs = _ln_call(x.reshape(n, d), ln0_g, ln0_b, TM_LN)
    for l in range(depth):
        qg = jnp.tile(q_norm_g[l], N_Q_HEADS).reshape(1, D_ATTN)
        kg = jnp.tile(k_norm_g[l], N_KV_HEADS).reshape(1, D_KV)
        zr, zi, q, k, vt, gf, ga = _in_proj_call(
            xs, w_in[l].astype(BF16), b_gate[l], qg, kg, cq, sq, ck, sk, cdft, hmq, hmk, TM_IN, seq)
        a = _dft_stage1_call(zr.reshape(batch, n1, DFT_N2 * D_FOURIER), zi.reshape(batch, n1, DFT_N2 * D_FOURIER),
                             m1, tw_r, tw_i, DFT1_COLS)
        yf = _dft_stage2_call(a.reshape(batch, 2, n1, DFT_N2, D_FOURIER), m3, DFT2_K1)
        ya = _attn_call(q, k, vt, batch, seq, ATTN_TQ)
        xs = _mix_call(yf.reshape(n, D_FOURIER), ya, gf, ga, xs,
                       w_branch_fourier[l].astype(BF16), w_branch_attn[l].astype(BF16), w_out[l].astype(BF16),
                       ln1_g[l], ln1_b[l], TM_MIX, alpha)
        g1, g2 = _peer_route_call(xs, peer_w_query[l].astype(BF16), peer_sub_keys[l].astype(BF16), TM_ROUTE)
        xs = _peer_expert_call(xs, g1, g2, peer_u[l].astype(BF16), peer_v[l].astype(BF16).T,
                               ln2_g[l], ln2_b[l], TM_EXPERT, EXPERT_ROWS_PER_STEP, alpha)
    return xs.reshape(batch, seq, d)
```

```python
import functools
import math

import numpy as np
import jax
import jax.numpy as jnp
from jax import lax
from jax.experimental import pallas as pl
from jax.experimental.pallas import tpu as pltpu

F32 = jnp.float32
BF16 = jnp.bfloat16

D_MODEL = 1024
GRID_W = 64
N_FOURIER_GROUPS = 4
FOURIER_GROUP_DIM = 128
D_FOURIER = N_FOURIER_GROUPS * FOURIER_GROUP_DIM
N_Q_HEADS = 8
N_KV_HEADS = 2
Q_PER_KV = N_Q_HEADS // N_KV_HEADS
HEAD_DIM = 64
D_ATTN = N_Q_HEADS * HEAD_DIM
D_KV = N_KV_HEADS * HEAD_DIM
ROPE_AXIS_DIM = HEAD_DIM // 2
ROPE_HALF = ROPE_AXIS_DIM // 2
ROPE_THETA = 10000.0
LOG2_E = math.log2(math.e)
V_ROWS = HEAD_DIM + 16
PEER_HEADS = 8
PEER_N_KEYS = 128
PEER_TOPK = 16
PEER_HALF = 128
LN_EPS = 1e-5
RMS_EPS = 1e-6

_C_XF = 0
_C_Q = D_FOURIER
_C_K = _C_Q + D_ATTN
_C_V = _C_K + D_KV
_C_GF = _C_V + D_KV
_C_GA = _C_GF + D_MODEL
_C_END = _C_GA + D_MODEL

DFT_N2 = 128

VMEM_LIMIT_BYTES = 56 * 1024 * 1024


def _params(*sem):
    return pltpu.CompilerParams(dimension_semantics=sem, vmem_limit_bytes=VMEM_LIMIT_BYTES)


def _dot(a, b):
    return jnp.dot(a, b, preferred_element_type=F32)


def _dot_nt(a, b):
    return lax.dot_general(a, b, (((1,), (1,)), ((), ())), preferred_element_type=F32)


def _layer_norm(z, g, b):
    mu = jnp.mean(z, axis=-1, keepdims=True)
    zc = z - mu
    var = jnp.mean(zc * zc, axis=-1, keepdims=True)
    return zc * lax.rsqrt(var + LN_EPS) * g + b


def _ln_kernel(x_ref, g_ref, b_ref, o_ref):
    o_ref[...] = _layer_norm(x_ref[...], g_ref[...], b_ref[...])


def _ln_call(x, g, b, tm):
    n, d = x.shape
    return pl.pallas_call(
        _ln_kernel,
        grid=(n // tm,),
        in_specs=[pl.BlockSpec((tm, d), lambda i: (i, 0)),
                  pl.BlockSpec((1, d), lambda i: (0, 0)),
                  pl.BlockSpec((1, d), lambda i: (0, 0))],
        out_specs=pl.BlockSpec((tm, d), lambda i: (i, 0)),
        out_shape=jax.ShapeDtypeStruct((n, d), F32),
        compiler_params=_params("parallel"),
        name="ln0",
    )(x, g.reshape(1, d), b.reshape(1, d))


def _rms_rope(t, gain, head_mean, cos, sin_signed):
    sq = t * t
    hi = sq.astype(BF16)
    lo = (sq - hi.astype(F32)).astype(BF16)
    ms = _dot(hi, head_mean) + _dot(lo, head_mean)
    tn = t * lax.rsqrt(ms + RMS_EPS) * gain
    width = t.shape[-1]
    lane = lax.broadcasted_iota(jnp.int32, tn.shape, 1)
    first_half = (lane % ROPE_AXIS_DIM) < ROPE_HALF
    rot = jnp.where(first_half, pltpu.roll(tn, width - ROPE_HALF, 1), pltpu.roll(tn, ROPE_HALF, 1))
    return tn * cos + rot * sin_signed


def _in_proj_kernel(x_ref, w_ref, bg_ref, qg_ref, kg_ref, cq_ref, sq_ref, ck_ref, sk_ref,
                    cdft_ref, hmq_ref, hmk_ref,
                    zr_ref, zi_ref, q_ref, k_ref, vt_ref, gf_ref, ga_ref):
    xb = x_ref[...].astype(BF16)

    def seg(a, b):
        return _dot(xb, w_ref[:, a:b])

    xf = seg(_C_XF, _C_Q).astype(BF16)
    for g in range(N_FOURIER_GROUPS):
        lo, hi = g * FOURIER_GROUP_DIM, (g + 1) * FOURIER_GROUP_DIM
        z = _dot(xf[:, lo:hi], cdft_ref[...])
        zr_ref[:, lo:hi] = z[:, :FOURIER_GROUP_DIM]
        zi_ref[:, lo:hi] = z[:, FOURIER_GROUP_DIM:]

    q = _rms_rope(seg(_C_Q, _C_K), qg_ref[...], hmq_ref[...], cq_ref[...], sq_ref[...])
    q_ref[...] = (q * (HEAD_DIM ** -0.5 * LOG2_E)).astype(BF16)
    k = _rms_rope(seg(_C_K, _C_V), kg_ref[...], hmk_ref[...], ck_ref[...], sk_ref[...]).astype(BF16)
    vt = seg(_C_V, _C_GF).T.astype(BF16)
    pad_row = lax.broadcasted_iota(jnp.int32, (V_ROWS - HEAD_DIM, vt.shape[1]), 0)
    pad = jnp.where(pad_row == 0, 1.0, 0.0).astype(BF16)
    for j in range(N_KV_HEADS):
        k_ref[j] = k[:, j * HEAD_DIM:(j + 1) * HEAD_DIM]
        vt_ref[j, 0] = jnp.concatenate([vt[j * HEAD_DIM:(j + 1) * HEAD_DIM, :], pad], axis=0)

    gf_ref[...] = jax.nn.sigmoid(seg(_C_GF, _C_GA) + bg_ref[0:1, :])
    ga_ref[...] = jax.nn.sigmoid(seg(_C_GA, _C_END) + bg_ref[1:2, :])


def _in_proj_call(x, w, bg, qg, kg, cq, sq, ck, sk, cdft, hmq, hmk, tm, seq):
    n, d = x.shape
    pos_blocks = seq // tm
    row = lambda i: (i, 0)
    pos = lambda i: (i % pos_blocks, 0)
    const = lambda i: (0, 0)
    full = lambda a: pl.BlockSpec(a.shape, const)
    return pl.pallas_call(
        _in_proj_kernel,
        grid=(n // tm,),
        in_specs=[pl.BlockSpec((tm, d), row), full(w), full(bg), full(qg), full(kg),
                  pl.BlockSpec((tm, D_ATTN), pos), pl.BlockSpec((tm, D_ATTN), pos),
                  pl.BlockSpec((tm, D_KV), pos), pl.BlockSpec((tm, D_KV), pos),
                  full(cdft), full(hmq), full(hmk)],
        out_specs=[pl.BlockSpec((tm, D_FOURIER), row), pl.BlockSpec((tm, D_FOURIER), row),
                   pl.BlockSpec((tm, D_ATTN), row),
                   pl.BlockSpec((N_KV_HEADS, tm, HEAD_DIM), lambda i: (0, i, 0)),
                   pl.BlockSpec((N_KV_HEADS, 1, V_ROWS, tm), lambda i: (0, i, 0, 0)),
                   pl.BlockSpec((tm, d), row), pl.BlockSpec((tm, d), row)],
        out_shape=[jax.ShapeDtypeStruct((n, D_FOURIER), F32), jax.ShapeDtypeStruct((n, D_FOURIER), F32),
                   jax.ShapeDtypeStruct((n, D_ATTN), BF16),
                   jax.ShapeDtypeStruct((N_KV_HEADS, n, HEAD_DIM), BF16),
                   jax.ShapeDtypeStruct((N_KV_HEADS, n // tm, V_ROWS, tm), BF16),
                   jax.ShapeDtypeStruct((n, d), F32), jax.ShapeDtypeStruct((n, d), F32)],
        compiler_params=_params("parallel"),
        name="in_proj",
    )(x, w, bg, qg, kg, cq, sq, ck, sk, cdft, hmq, hmk)


def _dft_stage1_kernel(zr_ref, zi_ref, m1_ref, tr_ref, ti_ref, o_ref):
    n1 = zr_ref.shape[1]
    zz = jnp.concatenate([zr_ref[0], zi_ref[0]], axis=0).astype(BF16)
    a = _dot(m1_ref[...], zz)
    ar, ai = a[:n1], a[n1:]
    tr, ti = tr_ref[...], ti_ref[...]
    o_ref[0, 0] = ar * tr - ai * ti
    o_ref[0, 1] = ar * ti + ai * tr


def _dft_stage1_call(zr, zi, m1, tr, ti, cw):
    b, n1, cols = zr.shape
    blk = pl.BlockSpec((1, n1, cw), lambda i, j: (i, 0, j))
    tw = pl.BlockSpec((n1, cw), lambda i, j: (0, j))
    return pl.pallas_call(
        _dft_stage1_kernel,
        grid=(b, cols // cw),
        in_specs=[blk, blk, pl.BlockSpec(m1.shape, lambda i, j: (0, 0)), tw, tw],
        out_specs=pl.BlockSpec((1, 2, n1, cw), lambda i, j: (i, 0, 0, j)),
        out_shape=jax.ShapeDtypeStruct((b, 2, n1, cols), F32),
        compiler_params=_params("parallel", "parallel"),
        name="dft_stage1",
    )(zr, zi, m1, tr, ti)


def _dft_stage2_kernel(a_ref, m3_ref, o_ref):
    kb, n2, c = a_ref.shape[2], a_ref.shape[3], a_ref.shape[4]
    for j in range(kb):
        rhs = jnp.concatenate([a_ref[0, 0, j], a_ref[0, 1, j]], axis=0).astype(BF16)
        o_ref[0, :, j * c:(j + 1) * c] = _dot(m3_ref[...], rhs)


def _dft_stage2_call(a, m3, kb):
    b, _, n1, n2, c = a.shape
    return pl.pallas_call(
        _dft_stage2_kernel,
        grid=(b, n1 // kb),
        in_specs=[pl.BlockSpec((1, 2, kb, n2, c), lambda i, j: (i, 0, j, 0, 0)),
                  pl.BlockSpec(m3.shape, lambda i, j: (0, 0))],
        out_specs=pl.BlockSpec((1, n2, kb * c), lambda i, j: (i, 0, j)),
        out_shape=jax.ShapeDtypeStruct((b, n2, n1 * c), F32),
        compiler_params=_params("parallel", "parallel"),
        name="dft_stage2",
    )(a, m3)


def _dft_constants(seq):
    c = FOURIER_GROUP_DIM
    n1, n2 = seq // DFT_N2, DFT_N2
    jc = np.arange(c)
    ang = 2.0 * np.pi * np.outer(jc, jc) / c
    cdft = np.concatenate([np.cos(ang), -np.sin(ang)], axis=1) / math.sqrt(c)
    j1 = np.arange(n1)
    a1 = 2.0 * np.pi * np.outer(j1, j1) / n1
    fr, fi = np.cos(a1) / math.sqrt(n1), -np.sin(a1) / math.sqrt(n1)
    m1 = np.block([[fr, -fi], [fi, fr]])
    j2 = np.arange(n2)
    a2 = 2.0 * np.pi * np.outer(j2, j2) / n2
    m3 = np.concatenate([np.cos(a2), np.sin(a2)], axis=1) / math.sqrt(n2)
    at = 2.0 * np.pi * np.outer(j1, j2) / seq
    return (jnp.asarray(cdft, BF16), jnp.asarray(m1, BF16), jnp.asarray(m3, BF16),
            jnp.asarray(np.cos(at), F32), jnp.asarray(-np.sin(at), F32))


def _attn_kernel(q_ref, k_ref, vt_ref, o_ref, qs_ref, st_ref, p_ref, acc_ref, *, tq):
    n_chunks, _, tk = vt_ref.shape[1:]
    slots = st_ref.shape[0]
    assert slots == 3 and n_chunks >= 4 and (n_chunks - 4) % slots == 0
    nq = Q_PER_KV * tq
    heads = range(N_KV_HEADS)
    for j in heads:
        for g in range(Q_PER_KV):
            h = j * Q_PER_KV + g
            qs_ref[j, g * tq:(g + 1) * tq, :] = q_ref[:, h * HEAD_DIM:(h + 1) * HEAD_DIM]
    acc_ref[...] = jnp.zeros_like(acc_ref)

    def scores(j, c, slot):
        start = pl.multiple_of(c * tk, tk)
        st_ref[slot, j] = _dot_nt(k_ref[j, pl.ds(start, tk), :], qs_ref[j])

    def softmax(j, slot, m):
        st = st_ref[slot, j]
        m_new = jnp.maximum(m, jnp.max(st, axis=0, keepdims=True))
        p_ref[slot, j] = jnp.exp2(st - m_new).astype(BF16)
        return m_new, jnp.exp2(m - m_new)

    def values(j, c, slot, alpha):
        acc_ref[j] = alpha * acc_ref[j] + _dot(vt_ref[j, c], p_ref[slot, j])

    def step(i, slot, carry, do_values=True, do_scores=True):
        if do_values:
            for j in heads:
                values(j, i - 2, (slot + 1) % slots, carry[j][2])
        new = []
        for j in heads:
            m, alpha = softmax(j, slot, carry[j][0])
            new.append((m, alpha, carry[j][1]))
        if do_scores:
            for j in heads:
                scores(j, i + 2, (slot + 2) % slots)
        return tuple(new)

    for j in heads:
        scores(j, 0, 0)
        scores(j, 1, 1)
    zero = jnp.zeros((1, nq), F32)
    carry = tuple((jnp.full((1, nq), -jnp.inf, F32), zero, zero) for _ in heads)
    carry = step(0, 0, carry, do_values=False)
    carry = step(1, 1, carry, do_values=False)

    def body(g, carry):
        i = 2 + slots * g
        for k in range(slots):
            carry = step(i + k, (2 + k) % slots, carry)
        return carry

    carry = lax.fori_loop(0, (n_chunks - 4) // slots, body, carry)
    for i in (n_chunks - 2, n_chunks - 1):
        carry = step(i, i % slots, carry, do_scores=False)
    pieces = []
    for j in heads:
        values(j, n_chunks - 2, (n_chunks - 2) % slots, carry[j][2])
        values(j, n_chunks - 1, (n_chunks - 1) % slots, carry[j][1])
        ot = acc_ref[j, :HEAD_DIM, :] / acc_ref[j, HEAD_DIM:HEAD_DIM + 1, :]
        o = jnp.concatenate([ot, jnp.zeros_like(ot)], axis=0).T
        pieces += [o[g * tq:(g + 1) * tq, :HEAD_DIM] for g in range(Q_PER_KV)]
    o_ref[...] = jnp.concatenate(pieces, axis=1)


def _attn_call(q, k, vt, batch, seq, tq):
    n = q.shape[0]
    qblocks = seq // tq
    tk = vt.shape[-1]
    chunks = seq // tk
    nq = Q_PER_KV * tq
    return pl.pallas_call(
        functools.partial(_attn_kernel, tq=tq),
        grid=(batch, qblocks),
        in_specs=[pl.BlockSpec((tq, D_ATTN), lambda b, i: (b * qblocks + i, 0)),
                  pl.BlockSpec((N_KV_HEADS, seq, HEAD_DIM), lambda b, i: (0, b, 0)),
                  pl.BlockSpec((N_KV_HEADS, chunks, V_ROWS, tk), lambda b, i: (0, b, 0, 0))],
        out_specs=pl.BlockSpec((tq, D_ATTN), lambda b, i: (b * qblocks + i, 0)),
        out_shape=jax.ShapeDtypeStruct((n, D_ATTN), F32),
        scratch_shapes=[pltpu.VMEM((N_KV_HEADS, nq, HEAD_DIM), BF16),
                        pltpu.VMEM((3, N_KV_HEADS, tk, nq), F32),
                        pltpu.VMEM((3, N_KV_HEADS, tk, nq), BF16),
                        pltpu.VMEM((N_KV_HEADS, V_ROWS, nq), F32)],
        compiler_params=_params("parallel", "parallel"),
        name="attention",
    )(q, k, vt)


def _mix_kernel(yf_ref, ya_ref, gf_ref, ga_ref, x_ref, wf_ref, wa_ref, wo_ref, g_ref, b_ref, o_ref, *, alpha):
    mixed = (gf_ref[...] * _dot(yf_ref[...].astype(BF16), wf_ref[...])
             + ga_ref[...] * _dot(ya_ref[...].astype(BF16), wa_ref[...]))
    z = alpha * x_ref[...] + _dot(mixed.astype(BF16), wo_ref[...])
    o_ref[...] = _layer_norm(z, g_ref[...], b_ref[...])


def _mix_call(yf, ya, gf, ga, x, wf, wa, wo, g, b, tm, alpha):
    n, d = x.shape
    row = lambda i: (i, 0)
    const = lambda i: (0, 0)
    return pl.pallas_call(
        functools.partial(_mix_kernel, alpha=alpha),
        grid=(n // tm,),
        in_specs=[pl.BlockSpec((tm, D_FOURIER), row), pl.BlockSpec((tm, D_ATTN), row),
                  pl.BlockSpec((tm, d), row), pl.BlockSpec((tm, d), row), pl.BlockSpec((tm, d), row),
                  pl.BlockSpec(wf.shape, const), pl.BlockSpec(wa.shape, const), pl.BlockSpec(wo.shape, const),
                  pl.BlockSpec((1, d), const), pl.BlockSpec((1, d), const)],
        out_specs=pl.BlockSpec((tm, d), row),
        out_shape=jax.ShapeDtypeStruct((n, d), F32),
        compiler_params=_params("parallel"),
        name="mix",
    )(yf, ya, gf, ga, x, wf, wa, wo, g.reshape(1, d), b.reshape(1, d))


_NEG_INF = float("-inf")


def _extract_top(s, order, count, exact):
    rank = jnp.full(s.shape, float(count), F32)
    big = float(2 ** 20)
    vals = []
    for a in range(count):
        m = jnp.max(s, axis=0, keepdims=True)
        if exact:
            first = jnp.min(jnp.where(s == m, order, big), axis=0, keepdims=True)
            sel = order == first
        else:
            sel = s == m
        rank = jnp.where(sel, float(a), rank)
        s = jnp.where(sel, _NEG_INF, s)
        vals.append(m)
    extracted = jnp.sum(jnp.where(rank < float(count), 1.0, 0.0), axis=0, keepdims=True)
    return vals, rank, jnp.where(extracted == float(count), 0.0, 1.0)


def _candidate_rows(v1, v2, combine):
    v2_all = jnp.concatenate(v2, axis=0)
    v2_lo = v2_all[:8]
    pieces = [combine(v1[0], v2_all)]
    pieces += [combine(v1[a], v2_lo) for a in range(1, 8)]
    pieces.append(combine(jnp.concatenate(v1[8:], axis=0), v2[0]))
    return jnp.concatenate(pieces, axis=0)


def _candidate_positions(tokens):
    pos = ([b for b in range(16)] + [a * 16 + b for a in range(1, 8) for b in range(8)]
           + [a * 16 for a in range(8, 16)])
    col = jnp.asarray(np.asarray(pos, np.float32).reshape(-1, 1))
    return jnp.broadcast_to(col, (len(pos), tokens))


def _peer_route_kernel(x_ref, wq_ref, keys_ref, pos_ref, g1_ref, g2_ref):
    ambiguous = _route_tile(x_ref, wq_ref, keys_ref, pos_ref, g1_ref, g2_ref, exact=False)

    @pl.when(jnp.max(ambiguous) > 0.0)
    def _():
        _route_tile(x_ref, wq_ref, keys_ref, pos_ref, g1_ref, g2_ref, exact=True)


def _route_tile(x_ref, wq_ref, keys_ref, pos_ref, g1_ref, g2_ref, exact):
    tokens = x_ref.shape[0]
    xb = x_ref[...].astype(BF16)
    key_order = lax.broadcasted_iota(jnp.int32, (PEER_N_KEYS, tokens), 0).astype(F32)
    cand_pos = pos_ref[...]
    ambiguous = jnp.zeros((1, tokens), F32)
    for h in range(PEER_HEADS):
        col = 2 * h * PEER_HALF
        qp = _dot(xb, wq_ref[:, col:col + 2 * PEER_HALF]).astype(BF16)
        scores, vals, ranks = [], [], []
        for c in range(2):
            s = _dot_nt(keys_ref[h, c], qp[:, c * PEER_HALF:(c + 1) * PEER_HALF])
            v, r, amb = _extract_top(s, key_order, PEER_TOPK, exact)
            ambiguous = jnp.maximum(ambiguous, amb)
            scores.append(s)
            vals.append(v)
            ranks.append(r)
        v1, v2 = vals
        cand = _candidate_rows(v1, v2, lambda a, b: a + b)
        _, crank, amb = _extract_top(cand, cand_pos, PEER_TOPK, exact)
        ambiguous = jnp.maximum(ambiguous, amb)
        chosen = crank < float(PEER_TOPK)
        e1 = [jnp.exp(v - v1[0]) for v in v1]
        e2 = [jnp.exp(v - v2[0]) for v in v2]
        ecand = _candidate_rows(e1, e2, lambda a, b: a * b)
        z = jnp.sum(jnp.where(chosen, ecand, 0.0), axis=0, keepdims=True)
        cnt = jnp.where(chosen, 1.0, 0.0)
        limit = [jnp.sum(cnt[0:16], axis=0, keepdims=True)]
        limit += [jnp.sum(cnt[8 + 8 * a:16 + 8 * a], axis=0, keepdims=True) for a in range(1, 8)]
        limit += [cnt[72 + a:73 + a] for a in range(8)]
        lim_dense = jnp.zeros((PEER_N_KEYS, tokens), F32)
        for a in range(PEER_TOPK):
            lim_dense = jnp.where(ranks[0] == float(a), limit[a], lim_dense)
        g1_ref[0, h] = jnp.exp(scores[0] - v1[0]) / z
        g1_ref[1, h] = lim_dense
        g2_ref[0, h] = jnp.exp(scores[1] - v2[0]).astype(BF16)
        g2_ref[1, h] = ranks[1].astype(BF16)
    return ambiguous


def _peer_route_call(x, wq, keys, tm):
    n, d = x.shape
    pos = _candidate_positions(tm)
    return pl.pallas_call(
        _peer_route_kernel,
        grid=(n // tm,),
        in_specs=[pl.BlockSpec((tm, d), lambda i: (i, 0)),
                  pl.BlockSpec(wq.shape, lambda i: (0, 0)),
                  pl.BlockSpec(keys.shape, lambda i: (0, 0, 0, 0)),
                  pl.BlockSpec(pos.shape, lambda i: (0, 0))],
        out_specs=[pl.BlockSpec((2, PEER_HEADS, PEER_N_KEYS, tm), lambda i: (0, 0, 0, i)),
                   pl.BlockSpec((2, PEER_HEADS, PEER_N_KEYS, tm), lambda i: (0, 0, 0, i))],
        out_shape=[jax.ShapeDtypeStruct((2, PEER_HEADS, PEER_N_KEYS, n), F32),
                   jax.ShapeDtypeStruct((2, PEER_HEADS, PEER_N_KEYS, n), BF16)],
        compiler_params=_params("parallel"),
        name="peer_route",
    )(x, wq, keys, pos)


def _peer_expert_kernel(x_ref, g1_ref, g2_ref, u_ref, vt_ref, g_ref, b_ref, o_ref, xb_ref, acc_ref, p_ref, *,
                        alpha, rows_per_step):
    j = pl.program_id(1)

    @pl.when(j == 0)
    def _():
        xb_ref[...] = x_ref[...].astype(BF16)
        acc_ref[...] = jnp.zeros_like(acc_ref)

    act = _dot_nt(u_ref[...], xb_ref[...])
    act = 0.5 * act * (1.0 + lax.erf(act * math.sqrt(0.5)))
    for r in range(rows_per_step):
        i1 = j * rows_per_step + r
        gate = None
        for h in range(PEER_HEADS):
            e1n = g1_ref[0, h, pl.ds(i1, 1), :].astype(BF16)
            lim = g1_ref[1, h, pl.ds(i1, 1), :].astype(BF16)
            term = jnp.where(g2_ref[1, h] < lim, g2_ref[0, h] * e1n, jnp.zeros((), BF16))
            gate = term if gate is None else gate + term
        lo, hi = r * PEER_N_KEYS, (r + 1) * PEER_N_KEYS
        p_ref[lo:hi, :] = gate * act[lo:hi, :].astype(BF16)
    acc_ref[...] += _dot(vt_ref[...], p_ref[...])

    @pl.when(j == pl.num_programs(1) - 1)
    def _():
        z = alpha * x_ref[...] + acc_ref[...].T
        o_ref[...] = _layer_norm(z, g_ref[...], b_ref[...])


def _peer_expert_call(x, g1, g2, u, vt, g, b, tm, rows_per_step, alpha):
    n, d = x.shape
    te = rows_per_step * PEER_N_KEYS
    n_exp = u.shape[0]
    return pl.pallas_call(
        functools.partial(_peer_expert_kernel, alpha=alpha, rows_per_step=rows_per_step),
        grid=(n // tm, n_exp // te),
        in_specs=[pl.BlockSpec((tm, d), lambda i, j: (i, 0)),
                  pl.BlockSpec((2, PEER_HEADS, PEER_N_KEYS, tm), lambda i, j: (0, 0, 0, i)),
                  pl.BlockSpec((2, PEER_HEADS, PEER_N_KEYS, tm), lambda i, j: (0, 0, 0, i)),
                  pl.BlockSpec((te, d), lambda i, j: (j, 0)),
                  pl.BlockSpec((d, te), lambda i, j: (0, j)),
                  pl.BlockSpec((1, d), lambda i, j: (0, 0)),
                  pl.BlockSpec((1, d), lambda i, j: (0, 0))],
        out_specs=pl.BlockSpec((tm, d), lambda i, j: (i, 0)),
        out_shape=jax.ShapeDtypeStruct((n, d), F32),
        scratch_shapes=[pltpu.VMEM((tm, d), BF16), pltpu.VMEM((d, tm), F32), pltpu.VMEM((te, tm), BF16)],
        compiler_params=_params("parallel", "arbitrary"),
        name="peer_experts",
    )(x, g1, g2, u, vt, g.reshape(1, d), b.reshape(1, d))


def _rope_tables(seq):
    rows = seq // GRID_W
    row_id = jnp.repeat(jnp.arange(rows, dtype=F32), GRID_W)
    col_id = jnp.tile(jnp.arange(GRID_W, dtype=F32), rows)
    inv_freq = 1.0 / (ROPE_THETA ** (jnp.arange(0, ROPE_AXIS_DIM, 2, dtype=F32) / ROPE_AXIS_DIM))
    ang_r = row_id[:, None] * inv_freq[None, :]
    ang_c = col_id[:, None] * inv_freq[None, :]
    ang = jnp.concatenate([ang_r, ang_r, ang_c, ang_c], axis=-1)
    sign = jnp.where((jnp.arange(HEAD_DIM) % ROPE_AXIS_DIM) < ROPE_HALF, -1.0, 1.0).astype(F32)
    return jnp.cos(ang), jnp.sin(ang) * sign[None, :]


def _head_mean_matrix(heads):
    m = np.kron(np.eye(heads), np.full((HEAD_DIM, HEAD_DIM), 1.0 / HEAD_DIM))
    return jnp.asarray(m, BF16)


TM_LN = 512
TM_IN = 512
DFT1_COLS = 2048
DFT2_K1 = 8
ATTN_TQ = 128
TM_MIX = 256
TM_ROUTE = 256
TM_EXPERT = 512
EXPERT_ROWS_PER_STEP = 16


def kernel(x, ln0_g, ln0_b, w_in, b_gate, q_norm_g, k_norm_g, w_branch_fourier, w_branch_attn, w_out,
           ln1_g, ln1_b, peer_w_query, peer_sub_keys, peer_u, peer_v, ln2_g, ln2_b):
    batch, seq, d = x.shape
    n = batch * seq
    depth = w_in.shape[0]
    alpha = (2.0 * depth) ** 0.25
    n1 = seq // DFT_N2

    cos, sin_signed = _rope_tables(seq)
    cq, sq = jnp.tile(cos, (1, N_Q_HEADS)), jnp.tile(sin_signed, (1, N_Q_HEADS))
    ck, sk = jnp.tile(cos, (1, N_KV_HEADS)), jnp.tile(sin_signed, (1, N_KV_HEADS))
    hmq, hmk = _head_mean_matrix(N_Q_HEADS), _head_mean_matrix(N_KV_HEADS)
    cdft, m1, m3, tw_r, tw_i = _dft_constants(seq)
    tw_r = jnp.repeat(tw_r, D_FOURIER, axis=1)
    tw_i = jnp.repeat(tw_i, D_FOURIER, axis=1)

    xs = _ln_call(x.reshape(n, d), ln0_g, ln0_b, TM_LN)
    for l in range(depth):
        qg = jnp.tile(q_norm_g[l], N_Q_HEADS).reshape(1, D_ATTN)
        kg = jnp.tile(k_norm_g[l], N_KV_HEADS).reshape(1, D_KV)
        zr, zi, q, k, vt, gf, ga = _in_proj_call(
            xs, w_in[l].astype(BF16), b_gate[l], qg, kg, cq, sq, ck, sk, cdft, hmq, hmk, TM_IN, seq)
        a = _dft_stage1_call(zr.reshape(batch, n1, DFT_N2 * D_FOURIER), zi.reshape(batch, n1, DFT_N2 * D_FOURIER),
                             m1, tw_r, tw_i, DFT1_COLS)
        yf = _dft_stage2_call(a.reshape(batch, 2, n1, DFT_N2, D_FOURIER), m3, DFT2_K1)
        ya = _attn_call(q, k, vt, batch, seq, ATTN_TQ)
        xs = _mix_call(yf.reshape(n, D_FOURIER), ya, gf, ga, xs,
                       w_branch_fourier[l].astype(BF16), w_branch_attn[l].astype(BF16), w_out[l].astype(BF16),
                       ln1_g[l], ln1_b[l], TM_MIX, alpha)
        g1, g2 = _peer_route_call(xs, peer_w_query[l].astype(BF16), peer_sub_keys[l].astype(BF16), TM_ROUTE)
        xs = _peer_expert_call(xs, g1, g2, peer_u[l].astype(BF16), peer_v[l].astype(BF16).T,
                               ln2_g[l], ln2_b[l], TM_EXPERT, EXPERT_ROWS_PER_STEP, alpha)
    return xs.reshape(batch, seq, d)
```

```python
import functools
import math

import numpy as np
import jax
import jax.numpy as jnp
from jax import lax
from jax.experimental import pallas as pl
from jax.experimental.pallas import tpu as pltpu

F32 = jnp.float32
BF16 = jnp.bfloat16

D_MODEL = 1024
GRID_W = 64
N_FOURIER_GROUPS = 4
FOURIER_GROUP_DIM = 128
D_FOURIER = N_FOURIER_GROUPS * FOURIER_GROUP_DIM
N_Q_HEADS = 8
N_KV_HEADS = 2
Q_PER_KV = N_Q_HEADS // N_KV_HEADS
HEAD_DIM = 64
D_ATTN = N_Q_HEADS * HEAD_DIM
D_KV = N_KV_HEADS * HEAD_DIM
ROPE_AXIS_DIM = HEAD_DIM // 2
ROPE_HALF = ROPE_AXIS_DIM // 2
ROPE_THETA = 10000.0
LOG2_E = math.log2(math.e)
V_ROWS = HEAD_DIM + 16
PEER_HEADS = 8
PEER_N_KEYS = 128
PEER_TOPK = 16
PEER_HALF = 128
LN_EPS = 1e-5
RMS_EPS = 1e-6

_C_XF = 0
_C_Q = D_FOURIER
_C_K = _C_Q + D_ATTN
_C_V = _C_K + D_KV
_C_GF = _C_V + D_KV
_C_GA = _C_GF + D_MODEL
_C_END = _C_GA + D_MODEL

DFT_N2 = 128

VMEM_LIMIT_BYTES = 56 * 1024 * 1024


def _params(*sem):
    return pltpu.CompilerParams(dimension_semantics=sem, vmem_limit_bytes=VMEM_LIMIT_BYTES)


def _dot(a, b):
    return jnp.dot(a, b, preferred_element_type=F32)


def _dot_nt(a, b):
    return lax.dot_general(a, b, (((1,), (1,)), ((), ())), preferred_element_type=F32)


def _layer_norm(z, g, b):
    mu = jnp.mean(z, axis=-1, keepdims=True)
    zc = z - mu
    var = jnp.mean(zc * zc, axis=-1, keepdims=True)
    return zc * lax.rsqrt(var + LN_EPS) * g + b


def _ln_kernel(x_ref, g_ref, b_ref, o_ref):
    o_ref[...] = _layer_norm(x_ref[...], g_ref[...], b_ref[...])


def _ln_call(x, g, b, tm):
    n, d = x.shape
    return pl.pallas_call(
        _ln_kernel,
        grid=(n // tm,),
        in_specs=[pl.BlockSpec((tm, d), lambda i: (i, 0)),
                  pl.BlockSpec((1, d), lambda i: (0, 0)),
                  pl.BlockSpec((1, d), lambda i: (0, 0))],
        out_specs=pl.BlockSpec((tm, d), lambda i: (i, 0)),
        out_shape=jax.ShapeDtypeStruct((n, d), F32),
        compiler_params=_params("parallel"),
        name="ln0",
    )(x, g.reshape(1, d), b.reshape(1, d))


def _rms_rope(t, gain, head_mean, cos, sin_signed):
    sq = t * t
    hi = sq.astype(BF16)
    lo = (sq - hi.astype(F32)).astype(BF16)
    ms = _dot(hi, head_mean) + _dot(lo, head_mean)
    tn = t * lax.rsqrt(ms + RMS_EPS) * gain
    width = t.shape[-1]
    lane = lax.broadcasted_iota(jnp.int32, tn.shape, 1)
    first_half = (lane % ROPE_AXIS_DIM) < ROPE_HALF
    rot = jnp.where(first_half, pltpu.roll(tn, width - ROPE_HALF, 1), pltpu.roll(tn, ROPE_HALF, 1))
    return tn * cos + rot * sin_signed


def _in_proj_kernel(x_ref, w_ref, bg_ref, qg_ref, kg_ref, cq_ref, sq_ref, ck_ref, sk_ref,
                    cdft_ref, hmq_ref, hmk_ref,
                    zr_ref, zi_ref, q_ref, k_ref, vt_ref, gf_ref, ga_ref):
    xb = x_ref[...].astype(BF16)

    def seg(a, b):
        return _dot(xb, w_ref[:, a:b])

    xf = seg(_C_XF, _C_Q).astype(BF16)
    for g in range(N_FOURIER_GROUPS):
        lo, hi = g * FOURIER_GROUP_DIM, (g + 1) * FOURIER_GROUP_DIM
        z = _dot(xf[:, lo:hi], cdft_ref[...])
        zr_ref[:, lo:hi] = z[:, :FOURIER_GROUP_DIM]
        zi_ref[:, lo:hi] = z[:, FOURIER_GROUP_DIM:]

    q = _rms_rope(seg(_C_Q, _C_K), qg_ref[...], hmq_ref[...], cq_ref[...], sq_ref[...])
    q_ref[...] = (q * (HEAD_DIM ** -0.5 * LOG2_E)).astype(BF16)
    k = _rms_rope(seg(_C_K, _C_V), kg_ref[...], hmk_ref[...], ck_ref[...], sk_ref[...]).astype(BF16)
    vt = seg(_C_V, _C_GF).T.astype(BF16)
    pad_row = lax.broadcasted_iota(jnp.int32, (V_ROWS - HEAD_DIM, vt.shape[1]), 0)
    pad = jnp.where(pad_row == 0, 1.0, 0.0).astype(BF16)
    for j in range(N_KV_HEADS):
        k_ref[j] = k[:, j * HEAD_DIM:(j + 1) * HEAD_DIM]
        vt_ref[j, 0] = jnp.concatenate([vt[j * HEAD_DIM:(j + 1) * HEAD_DIM, :], pad], axis=0)

    gf_ref[...] = jax.nn.sigmoid(seg(_C_GF, _C_GA) + bg_ref[0:1, :])
    ga_ref[...] = jax.nn.sigmoid(seg(_C_GA, _C_END) + bg_ref[1:2, :])


def _in_proj_call(x, w, bg, qg, kg, cq, sq, ck, sk, cdft, hmq, hmk, tm, seq):
    n, d = x.shape
    pos_blocks = seq // tm
    row = lambda i: (i, 0)
    pos = lambda i: (i % pos_blocks, 0)
    const = lambda i: (0, 0)
    full = lambda a: pl.BlockSpec(a.shape, const)
    return pl.pallas_call(
        _in_proj_kernel,
        grid=(n // tm,),
        in_specs=[pl.BlockSpec((tm, d), row), full(w), full(bg), full(qg), full(kg),
                  pl.BlockSpec((tm, D_ATTN), pos), pl.BlockSpec((tm, D_ATTN), pos),
                  pl.BlockSpec((tm, D_KV), pos), pl.BlockSpec((tm, D_KV), pos),
                  full(cdft), full(hmq), full(hmk)],
        out_specs=[pl.BlockSpec((tm, D_FOURIER), row), pl.BlockSpec((tm, D_FOURIER), row),
                   pl.BlockSpec((tm, D_ATTN), row),
                   pl.BlockSpec((N_KV_HEADS, tm, HEAD_DIM), lambda i: (0, i, 0)),
                   pl.BlockSpec((N_KV_HEADS, 1, V_ROWS, tm), lambda i: (0, i, 0, 0)),
                   pl.BlockSpec((tm, d), row), pl.BlockSpec((tm, d), row)],
        out_shape=[jax.ShapeDtypeStruct((n, D_FOURIER), F32), jax.ShapeDtypeStruct((n, D_FOURIER), F32),
                   jax.ShapeDtypeStruct((n, D_ATTN), BF16),
                   jax.ShapeDtypeStruct((N_KV_HEADS, n, HEAD_DIM), BF16),
                   jax.ShapeDtypeStruct((N_KV_HEADS, n // tm, V_ROWS, tm), BF16),
                   jax.ShapeDtypeStruct((n, d), F32), jax.ShapeDtypeStruct((n, d), F32)],
        compiler_params=_params("parallel"),
        name="in_proj",
    )(x, w, bg, qg, kg, cq, sq, ck, sk, cdft, hmq, hmk)


def _dft_stage1_kernel(zr_ref, zi_ref, m1_ref, tr_ref, ti_ref, o_ref):
    n1 = zr_ref.shape[1]
    zz = jnp.concatenate([zr_ref[0], zi_ref[0]], axis=0).astype(BF16)
    a = _dot(m1_ref[...], zz)
    ar, ai = a[:n1], a[n1:]
    tr, ti = tr_ref[...], ti_ref[...]
    o_ref[0, 0] = ar * tr - ai * ti
    o_ref[0, 1] = ar * ti + ai * tr


def _dft_stage1_call(zr, zi, m1, tr, ti, cw):
    b, n1, cols = zr.shape
    blk = pl.BlockSpec((1, n1, cw), lambda i, j: (i, 0, j))
    tw = pl.BlockSpec((n1, cw), lambda i, j: (0, j))
    return pl.pallas_call(
        _dft_stage1_kernel,
        grid=(b, cols // cw),
        in_specs=[blk, blk, pl.BlockSpec(m1.shape, lambda i, j: (0, 0)), tw, tw],
        out_specs=pl.BlockSpec((1, 2, n1, cw), lambda i, j: (i, 0, 0, j)),
        out_shape=jax.ShapeDtypeStruct((b, 2, n1, cols), F32),
        compiler_params=_params("parallel", "parallel"),
        name="dft_stage1",
    )(zr, zi, m1, tr, ti)


def _dft_stage2_kernel(a_ref, m3_ref, o_ref):
    kb, n2, c = a_ref.shape[2], a_ref.shape[3], a_ref.shape[4]
    for j in range(kb):
        rhs = jnp.concatenate([a_ref[0, 0, j], a_ref[0, 1, j]], axis=0).astype(BF16)
        o_ref[0, :, j * c:(j + 1) * c] = _dot(m3_ref[...], rhs)


def _dft_stage2_call(a, m3, kb):
    b, _, n1, n2, c = a.shape
    return pl.pallas_call(
        _dft_stage2_kernel,
        grid=(b, n1 // kb),
        in_specs=[pl.BlockSpec((1, 2, kb, n2, c), lambda i, j: (i, 0, j, 0, 0)),
                  pl.BlockSpec(m3.shape, lambda i, j: (0, 0))],
        out_specs=pl.BlockSpec((1, n2, kb * c), lambda i, j: (i, 0, j)),
        out_shape=jax.ShapeDtypeStruct((b, n2, n1 * c), F32),
        compiler_params=_params("parallel", "parallel"),
        name="dft_stage2",
    )(a, m3)


def _dft_constants(seq):
    c = FOURIER_GROUP_DIM
    n1, n2 = seq // DFT_N2, DFT_N2
    jc = np.arange(c)
    ang = 2.0 * np.pi * np.outer(jc, jc) / c
    cdft = np.concatenate([np.cos(ang), -np.sin(ang)], axis=1) / math.sqrt(c)
    j1 = np.arange(n1)
    a1 = 2.0 * np.pi * np.outer(j1, j1) / n1
    fr, fi = np.cos(a1) / math.sqrt(n1), -np.sin(a1) / math.sqrt(n1)
    m1 = np.block([[fr, -fi], [fi, fr]])
    j2 = np.arange(n2)
    a2 = 2.0 * np.pi * np.outer(j2, j2) / n2
    m3 = np.concatenate([np.cos(a2), np.sin(a2)], axis=1) / math.sqrt(n2)
    at = 2.0 * np.pi * np.outer(j1, j2) / seq
    return (jnp.asarray(cdft, BF16), jnp.asarray(m1, BF16), jnp.asarray(m3, BF16),
            jnp.asarray(np.cos(at), F32), jnp.asarray(-np.sin(at), F32))


def _attn_kernel(q_ref, k_ref, vt_ref, o_ref, qs_ref, st_ref, p_ref, acc_ref, *, tq):
    n_chunks, _, tk = vt_ref.shape[1:]
    slots = st_ref.shape[0]
    assert slots == 3 and n_chunks >= 4 and (n_chunks - 4) % slots == 0
    nq = Q_PER_KV * tq
    heads = range(N_KV_HEADS)
    for j in heads:
        for g in range(Q_PER_KV):
            h = j * Q_PER_KV + g
            qs_ref[j, g * tq:(g + 1) * tq, :] = q_ref[:, h * HEAD_DIM:(h + 1) * HEAD_DIM]
    acc_ref[...] = jnp.zeros_like(acc_ref)

    def scores(j, c, slot):
        start = pl.multiple_of(c * tk, tk)
        st_ref[slot, j] = _dot_nt(k_ref[j, pl.ds(start, tk), :], qs_ref[j])

    def softmax(j, slot, m):
        st = st_ref[slot, j]
        m_new = jnp.maximum(m, jnp.max(st, axis=0, keepdims=True))
        p_ref[slot, j] = jnp.exp2(st - m_new).astype(BF16)
        return m_new, jnp.exp2(m - m_new)

    def values(j, c, slot, alpha):
        acc_ref[j] = alpha * acc_ref[j] + _dot(vt_ref[j, c], p_ref[slot, j])

    def step(i, slot, carry, do_values=True, do_scores=True):
        if do_values:
            for j in heads:
                values(j, i - 2, (slot + 1) % slots, carry[j][2])
        new = []
        for j in heads:
            m, alpha = softmax(j, slot, carry[j][0])
            new.append((m, alpha, carry[j][1]))
        if do_scores:
            for j in heads:
                scores(j, i + 2, (slot + 2) % slots)
        return tuple(new)

    for j in heads:
        scores(j, 0, 0)
        scores(j, 1, 1)
    zero = jnp.zeros((1, nq), F32)
    carry = tuple((jnp.full((1, nq), -jnp.inf, F32), zero, zero) for _ in heads)
    carry = step(0, 0, carry, do_values=False)
    carry = step(1, 1, carry, do_values=False)

    def body(g, carry):
        i = 2 + slots * g
        for k in range(slots):
            carry = step(i + k, (2 + k) % slots, carry)
        return carry

    carry = lax.fori_loop(0, (n_chunks - 4) // slots, body, carry)
    for i in (n_chunks - 2, n_chunks - 1):
        carry = step(i, i % slots, carry, do_scores=False)
    pieces = []
    for j in heads:
        values(j, n_chunks - 2, (n_chunks - 2) % slots, carry[j][2])
        values(j, n_chunks - 1, (n_chunks - 1) % slots, carry[j][1])
        ot = acc_ref[j, :HEAD_DIM, :] / acc_ref[j, HEAD_DIM:HEAD_DIM + 1, :]
        o = jnp.concatenate([ot, jnp.zeros_like(ot)], axis=0).T
        pieces += [o[g * tq:(g + 1) * tq, :HEAD_DIM] for g in range(Q_PER_KV)]
    o_ref[...] = jnp.concatenate(pieces, axis=1)


def _attn_call(q, k, vt, batch, seq, tq):
    n = q.shape[0]
    qblocks = seq // tq
    tk = vt.shape[-1]
    chunks = seq // tk
    nq = Q_PER_KV * tq
    return pl.pallas_call(
        functools.partial(_attn_kernel, tq=tq),
        grid=(batch, qblocks),
        in_specs=[pl.BlockSpec((tq, D_ATTN), lambda b, i: (b * qblocks + i, 0)),
                  pl.BlockSpec((N_KV_HEADS, seq, HEAD_DIM), lambda b, i: (0, b, 0)),
                  pl.BlockSpec((N_KV_HEADS, chunks, V_ROWS, tk), lambda b, i: (0, b, 0, 0))],
        out_specs=pl.BlockSpec((tq, D_ATTN), lambda b, i: (b * qblocks + i, 0)),
        out_shape=jax.ShapeDtypeStruct((n, D_ATTN), F32),
        scratch_shapes=[pltpu.VMEM((N_KV_HEADS, nq, HEAD_DIM), BF16),
                        pltpu.VMEM((3, N_KV_HEADS, tk, nq), F32),
                        pltpu.VMEM((3, N_KV_HEADS, tk, nq), BF16),
                        pltpu.VMEM((N_KV_HEADS, V_ROWS, nq), F32)],
        compiler_params=_params("parallel", "parallel"),
        name="attention",
    )(q, k, vt)


def _mix_kernel(yf_ref, ya_ref, gf_ref, ga_ref, x_ref, wf_ref, wa_ref, wo_ref, g_ref, b_ref, o_ref, *, alpha):
    mixed = (gf_ref[...] * _dot(yf_ref[...].astype(BF16), wf_ref[...])
             + ga_ref[...] * _dot(ya_ref[...].astype(BF16), wa_ref[...]))
    z = alpha * x_ref[...] + _dot(mixed.astype(BF16), wo_ref[...])
    o_ref[...] = _layer_norm(z, g_ref[...], b_ref[...])


def _mix_call(yf, ya, gf, ga, x, wf, wa, wo, g, b, tm, alpha):
    n, d = x.shape
    row = lambda i: (i, 0)
    const = lambda i: (0, 0)
    return pl.pallas_call(
        functools.partial(_mix_kernel, alpha=alpha),
        grid=(n // tm,),
        in_specs=[pl.BlockSpec((tm, D_FOURIER), row), pl.BlockSpec((tm, D_ATTN), row),
                  pl.BlockSpec((tm, d), row), pl.BlockSpec((tm, d), row), pl.BlockSpec((tm, d), row),
                  pl.BlockSpec(wf.shape, const), pl.BlockSpec(wa.shape, const), pl.BlockSpec(wo.shape, const),
                  pl.BlockSpec((1, d), const), pl.BlockSpec((1, d), const)],
        out_specs=pl.BlockSpec((tm, d), row),
        out_shape=jax.ShapeDtypeStruct((n, d), F32),
        compiler_params=_params("parallel"),
        name="mix",
    )(yf, ya, gf, ga, x, wf, wa, wo, g.reshape(1, d), b.reshape(1, d))


_NEG_INF = float("-inf")


def _extract_top(s, order, count, exact):
    rank = jnp.full(s.shape, float(count), F32)
    big = float(2 ** 20)
    vals = []
    for a in range(count):
        m = jnp.max(s, axis=0, keepdims=True)
        if exact:
            first = jnp.min(jnp.where(s == m, order, big), axis=0, keepdims=True)
            sel = order == first
        else:
            sel = s == m
        rank = jnp.where(sel, float(a), rank)
        s = jnp.where(sel, _NEG_INF, s)
        vals.append(m)
    extracted = jnp.sum(jnp.where(rank < float(count), 1.0, 0.0), axis=0, keepdims=True)
    return vals, rank, jnp.where(extracted == float(count), 0.0, 1.0)


def _candidate_rows(v1, v2, combine):
    v2_all = jnp.concatenate(v2, axis=0)
    v2_lo = v2_all[:8]
    pieces = [combine(v1[0], v2_all)]
    pieces += [combine(v1[a], v2_lo) for a in range(1, 8)]
    pieces.append(combine(jnp.concatenate(v1[8:], axis=0), v2[0]))
    return jnp.concatenate(pieces, axis=0)


def _candidate_positions(tokens):
    pos = ([b for b in range(16)] + [a * 16 + b for a in range(1, 8) for b in range(8)]
           + [a * 16 for a in range(8, 16)])
    col = jnp.asarray(np.asarray(pos, np.float32).reshape(-1, 1))
    return jnp.broadcast_to(col, (len(pos), tokens))


def _peer_route_kernel(x_ref, wq_ref, keys_ref, pos_ref, g1_ref, g2_ref):
    ambiguous = _route_tile(x_ref, wq_ref, keys_ref, pos_ref, g1_ref, g2_ref, exact=False)

    @pl.when(jnp.max(ambiguous) > 0.0)
    def _():
        _route_tile(x_ref, wq_ref, keys_ref, pos_ref, g1_ref, g2_ref, exact=True)


def _route_tile(x_ref, wq_ref, keys_ref, pos_ref, g1_ref, g2_ref, exact):
    tokens = x_ref.shape[0]
    xb = x_ref[...].astype(BF16)
    key_order = lax.broadcasted_iota(jnp.int32, (PEER_N_KEYS, tokens), 0).astype(F32)
    cand_pos = pos_ref[...]
    ambiguous = jnp.zeros((1, tokens), F32)
    for h in range(PEER_HEADS):
        col = 2 * h * PEER_HALF
        qp = _dot(xb, wq_ref[:, col:col + 2 * PEER_HALF]).astype(BF16)
        scores, vals, ranks = [], [], []
        for c in range(2):
            s = _dot_nt(keys_ref[h, c], qp[:, c * PEER_HALF:(c + 1) * PEER_HALF])
            v, r, amb = _extract_top(s, key_order, PEER_TOPK, exact)
            ambiguous = jnp.maximum(ambiguous, amb)
            scores.append(s)
            vals.append(v)
            ranks.append(r)
        v1, v2 = vals
        cand = _candidate_rows(v1, v2, lambda a, b: a + b)
        _, crank, amb = _extract_top(cand, cand_pos, PEER_TOPK, exact)
        ambiguous = jnp.maximum(ambiguous, amb)
        chosen = crank < float(PEER_TOPK)
        e1 = [jnp.exp(v - v1[0]) for v in v1]
        e2 = [jnp.exp(v - v2[0]) for v in v2]
        ecand = _candidate_rows(e1, e2, lambda a, b: a * b)
        z = jnp.sum(jnp.where(chosen, ecand, 0.0), axis=0, keepdims=True)
        cnt = jnp.where(chosen, 1.0, 0.0)
        limit = [jnp.sum(cnt[0:16], axis=0, keepdims=True)]
        limit += [jnp.sum(cnt[8 + 8 * a:16 + 8 * a], axis=0, keepdims=True) for a in range(1, 8)]
        limit += [cnt[72 + a:73 + a] for a in range(8)]
        lim_dense = jnp.zeros((PEER_N_KEYS, tokens), F32)
        for a in range(PEER_TOPK):
            lim_dense = jnp.where(ranks[0] == float(a), limit[a], lim_dense)
        g1_ref[0, h] = jnp.exp(scores[0] - v1[0]) / z
        g1_ref[1, h] = lim_dense
        g2_ref[0, h] = jnp.exp(scores[1] - v2[0]).astype(BF16)
        g2_ref[1, h] = ranks[1].astype(BF16)
    return ambiguous


def _peer_route_call(x, wq, keys, tm):
    n, d = x.shape
    pos = _candidate_positions(tm)
    return pl.pallas_call(
        _peer_route_kernel,
        grid=(n // tm,),
        in_specs=[pl.BlockSpec((tm, d), lambda i: (i, 0)),
                  pl.BlockSpec(wq.shape, lambda i: (0, 0)),
                  pl.BlockSpec(keys.shape, lambda i: (0, 0, 0, 0)),
                  pl.BlockSpec(pos.shape, lambda i: (0, 0))],
        out_specs=[pl.BlockSpec((2, PEER_HEADS, PEER_N_KEYS, tm), lambda i: (0, 0, 0, i)),
                   pl.BlockSpec((2, PEER_HEADS, PEER_N_KEYS, tm), lambda i: (0, 0, 0, i))],
        out_shape=[jax.ShapeDtypeStruct((2, PEER_HEADS, PEER_N_KEYS, n), F32),
                   jax.ShapeDtypeStruct((2, PEER_HEADS, PEER_N_KEYS, n), BF16)],
        compiler_params=_params("parallel"),
        name="peer_route",
    )(x, wq, keys, pos)


def _peer_expert_kernel(x_ref, g1_ref, g2_ref, u_ref, vt_ref, g_ref, b_ref, o_ref, xb_ref, acc_ref, p_ref, *,
                        alpha, rows_per_step):
    j = pl.program_id(1)

    @pl.when(j == 0)
    def _():
        xb_ref[...] = x_ref[...].astype(BF16)
        acc_ref[...] = jnp.zeros_like(acc_ref)

    act = _dot_nt(u_ref[...], xb_ref[...])
    act = 0.5 * act * (1.0 + lax.erf(act * math.sqrt(0.5)))
    for r in range(rows_per_step):
        i1 = j * rows_per_step + r
        gate = None
        for h in range(PEER_HEADS):
            e1n = g1_ref[0, h, pl.ds(i1, 1), :].astype(BF16)
            lim = g1_ref[1, h, pl.ds(i1, 1), :].astype(BF16)
            term = jnp.where(g2_ref[1, h] < lim, g2_ref[0, h] * e1n, jnp.zeros((), BF16))
            gate = term if gate is None else gate + term
        lo, hi = r * PEER_N_KEYS, (r + 1) * PEER_N_KEYS
        p_ref[lo:hi, :] = gate * act[lo:hi, :].astype(BF16)
    acc_ref[...] += _dot(vt_ref[...], p_ref[...])

    @pl.when(j == pl.num_programs(1) - 1)
    def _():
        z = alpha * x_ref[...] + acc_ref[...].T
        o_ref[...] = _layer_norm(z, g_ref[...], b_ref[...])


def _peer_expert_call(x, g1, g2, u, vt, g, b, tm, rows_per_step, alpha):
    n, d = x.shape
    te = rows_per_step * PEER_N_KEYS
    n_exp = u.shape[0]
    return pl.pallas_call(
        functools.partial(_peer_expert_kernel, alpha=alpha, rows_per_step=rows_per_step),
        grid=(n // tm, n_exp // te),
        in_specs=[pl.BlockSpec((tm, d), lambda i, j: (i, 0)),
                  pl.BlockSpec((2, PEER_HEADS, PEER_N_KEYS, tm), lambda i, j: (0, 0, 0, i)),
                  pl.BlockSpec((2, PEER_HEADS, PEER_N_KEYS, tm), lambda i, j: (0, 0, 0, i)),
                  pl.BlockSpec((te, d), lambda i, j: (j, 0)),
                  pl.BlockSpec((d, te), lambda i, j: (0, j)),
                  pl.BlockSpec((1, d), lambda i, j: (0, 0)),
                  pl.BlockSpec((1, d), lambda i, j: (0, 0))],
        out_specs=pl.BlockSpec((tm, d), lambda i, j: (i, 0)),
        out_shape=jax.ShapeDtypeStruct((n, d), F32),
        scratch_shapes=[pltpu.VMEM((tm, d), BF16), pltpu.VMEM((d, tm), F32), pltpu.VMEM((te, tm), BF16)],
        compiler_params=_params("parallel", "arbitrary"),
        name="peer_experts",
    )(x, g1, g2, u, vt, g.reshape(1, d), b.reshape(1, d))


def _rope_tables(seq):
    rows = seq // GRID_W
    row_id = jnp.repeat(jnp.arange(rows, dtype=F32), GRID_W)
    col_id = jnp.tile(jnp.arange(GRID_W, dtype=F32), rows)
    inv_freq = 1.0 / (ROPE_THETA ** (jnp.arange(0, ROPE_AXIS_DIM, 2, dtype=F32) / ROPE_AXIS_DIM))
    ang_r = row_id[:, None] * inv_freq[None, :]
    ang_c = col_id[:, None] * inv_freq[None, :]
    ang = jnp.concatenate([ang_r, ang_r, ang_c, ang_c], axis=-1)
    sign = jnp.where((jnp.arange(HEAD_DIM) % ROPE_AXIS_DIM) < ROPE_HALF, -1.0, 1.0).astype(F32)
    return jnp.cos(ang), jnp.sin(ang) * sign[None, :]


def _head_mean_matrix(heads):
    m = np.kron(np.eye(heads), np.full((HEAD_DIM, HEAD_DIM), 1.0 / HEAD_DIM))
    return jnp.asarray(m, BF16)


TM_LN = 512
TM_IN = 512
DFT1_COLS = 2048
DFT2_K1 = 8
ATTN_TQ = 256
TM_MIX = 256
TM_ROUTE = 256
TM_EXPERT = 512
EXPERT_ROWS_PER_STEP = 16


def kernel(x, ln0_g, ln0_b, w_in, b_gate, q_norm_g, k_norm_g, w_branch_fourier, w_branch_attn, w_out,
           ln1_g, ln1_b, peer_w_query, peer_sub_keys, peer_u, peer_v, ln2_g, ln2_b):
    batch, seq, d = x.shape
    n = batch * seq
    depth = w_in.shape[0]
    alpha = (2.0 * depth) ** 0.25
    n1 = seq // DFT_N2

    cos, sin_signed = _rope_tables(seq)
    cq, sq = jnp.tile(cos, (1, N_Q_HEADS)), jnp.tile(sin_signed, (1, N_Q_HEADS))
    ck, sk = jnp.tile(cos, (1, N_KV_HEADS)), jnp.tile(sin_signed, (1, N_KV_HEADS))
    hmq, hmk = _head_mean_matrix(N_Q_HEADS), _head_mean_matrix(N_KV_HEADS)
    cdft, m1, m3, tw_r, tw_i = _dft_constants(seq)
    tw_r = jnp.repeat(tw_r, D_FOURIER, axis=1)
    tw_i = jnp.repeat(tw_i, D_FOURIER, axis=1)

    xs = _ln_call(x.reshape(n, d), ln0_g, ln0_b, TM_LN)
    for l in range(depth):
        qg = jnp.tile(q_norm_g[l], N_Q_HEADS).reshape(1, D_ATTN)
        kg = jnp.tile(k_norm_g[l], N_KV_HEADS).reshape(1, D_KV)
        zr, zi, q, k, vt, gf, ga = _in_proj_call(
            xs, w_in[l].astype(BF16), b_gate[l], qg, kg, cq, sq, ck, sk, cdft, hmq, hmk, TM_IN, seq)
        a = _dft_stage1_call(zr.reshape(batch, n1, DFT_N2 * D_FOURIER), zi.reshape(batch, n1, DFT_N2 * D_FOURIER),
                             m1, tw_r, tw_i, DFT1_COLS)
        yf = _dft_stage2_call(a.reshape(batch, 2, n1, DFT_N2, D_FOURIER), m3, DFT2_K1)
        ya = _attn_call(q, k, vt, batch, seq, ATTN_TQ)
        xs = _mix_call(yf.reshape(n, D_FOURIER), ya, gf, ga, xs,
                       w_branch_fourier[l].astype(BF16), w_branch_attn[l].astype(BF16), w_out[l].astype(BF16),
                       ln1_g[l], ln1_b[l], TM_MIX, alpha)
        g1, g2 = _peer_route_call(xs, peer_w_query[l].astype(BF16), peer_sub_keys[l].astype(BF16), TM_ROUTE)
        xs = _peer_expert_call(xs, g1, g2, peer_u[l].astype(BF16), peer_v[l].astype(BF16).T,
                               ln2_g[l], ln2_b[l], TM_EXPERT, EXPERT_ROWS_PER_STEP, alpha)
    return xs.reshape(batch, seq, d)
```

```python
import functools
import math

import numpy as np
import jax
import jax.numpy as jnp
from jax import lax
from jax.experimental import pallas as pl
from jax.experimental.pallas import tpu as pltpu

F32 = jnp.float32
BF16 = jnp.bfloat16

D_MODEL = 1024
GRID_W = 64
N_FOURIER_GROUPS = 4
FOURIER_GROUP_DIM = 128
D_FOURIER = N_FOURIER_GROUPS * FOURIER_GROUP_DIM
N_Q_HEADS = 8
N_KV_HEADS = 2
Q_PER_KV = N_Q_HEADS // N_KV_HEADS
HEAD_DIM = 64
D_ATTN = N_Q_HEADS * HEAD_DIM
D_KV = N_KV_HEADS * HEAD_DIM
ROPE_AXIS_DIM = HEAD_DIM // 2
ROPE_HALF = ROPE_AXIS_DIM // 2
ROPE_THETA = 10000.0
LOG2_E = math.log2(math.e)
V_ROWS = HEAD_DIM + 16
PEER_HEADS = 8
PEER_N_KEYS = 128
PEER_TOPK = 16
PEER_HALF = 128
LN_EPS = 1e-5
RMS_EPS = 1e-6

_C_XF = 0
_C_Q = D_FOURIER
_C_K = _C_Q + D_ATTN
_C_V = _C_K + D_KV
_C_GF = _C_V + D_KV
_C_GA = _C_GF + D_MODEL
_C_END = _C_GA + D_MODEL

DFT_N2 = 128

VMEM_LIMIT_BYTES = 56 * 1024 * 1024


def _params(*sem):
    return pltpu.CompilerParams(dimension_semantics=sem, vmem_limit_bytes=VMEM_LIMIT_BYTES)


def _dot(a, b):
    return jnp.dot(a, b, preferred_element_type=F32)


def _dot_nt(a, b):
    return lax.dot_general(a, b, (((1,), (1,)), ((), ())), preferred_element_type=F32)


def _layer_norm(z, g, b):
    mu = jnp.mean(z, axis=-1, keepdims=True)
    zc = z - mu
    var = jnp.mean(zc * zc, axis=-1, keepdims=True)
    return zc * lax.rsqrt(var + LN_EPS) * g + b


def _ln_kernel(x_ref, g_ref, b_ref, o_ref):
    o_ref[...] = _layer_norm(x_ref[...], g_ref[...], b_ref[...])


def _ln_call(x, g, b, tm):
    n, d = x.shape
    return pl.pallas_call(
        _ln_kernel,
        grid=(n // tm,),
        in_specs=[pl.BlockSpec((tm, d), lambda i: (i, 0)),
                  pl.BlockSpec((1, d), lambda i: (0, 0)),
                  pl.BlockSpec((1, d), lambda i: (0, 0))],
        out_specs=pl.BlockSpec((tm, d), lambda i: (i, 0)),
        out_shape=jax.ShapeDtypeStruct((n, d), F32),
        compiler_params=_params("parallel"),
        name="ln0",
    )(x, g.reshape(1, d), b.reshape(1, d))


def _rms_rope(t, gain, head_mean, cos, sin_signed):
    sq = t * t
    hi = sq.astype(BF16)
    lo = (sq - hi.astype(F32)).astype(BF16)
    ms = _dot(hi, head_mean) + _dot(lo, head_mean)
    tn = t * lax.rsqrt(ms + RMS_EPS) * gain
    width = t.shape[-1]
    lane = lax.broadcasted_iota(jnp.int32, tn.shape, 1)
    first_half = (lane % ROPE_AXIS_DIM) < ROPE_HALF
    rot = jnp.where(first_half, pltpu.roll(tn, width - ROPE_HALF, 1), pltpu.roll(tn, ROPE_HALF, 1))
    return tn * cos + rot * sin_signed


def _in_proj_kernel(x_ref, w_ref, bg_ref, qg_ref, kg_ref, cq_ref, sq_ref, ck_ref, sk_ref,
                    cdft_ref, hmq_ref, hmk_ref,
                    zr_ref, zi_ref, q_ref, k_ref, vt_ref, gf_ref, ga_ref):
    xb = x_ref[...].astype(BF16)

    def seg(a, b):
        return _dot(xb, w_ref[:, a:b])

    xf = seg(_C_XF, _C_Q).astype(BF16)
    for g in range(N_FOURIER_GROUPS):
        lo, hi = g * FOURIER_GROUP_DIM, (g + 1) * FOURIER_GROUP_DIM
        z = _dot(xf[:, lo:hi], cdft_ref[...])
        zr_ref[:, lo:hi] = z[:, :FOURIER_GROUP_DIM]
        zi_ref[:, lo:hi] = z[:, FOURIER_GROUP_DIM:]

    q = _rms_rope(seg(_C_Q, _C_K), qg_ref[...], hmq_ref[...], cq_ref[...], sq_ref[...])
    q_ref[...] = (q * (HEAD_DIM ** -0.5 * LOG2_E)).astype(BF16)
    k = _rms_rope(seg(_C_K, _C_V), kg_ref[...], hmk_ref[...], ck_ref[...], sk_ref[...]).astype(BF16)
    vt = seg(_C_V, _C_GF).T.astype(BF16)
    pad_row = lax.broadcasted_iota(jnp.int32, (V_ROWS - HEAD_DIM, vt.shape[1]), 0)
    pad = jnp.where(pad_row == 0, 1.0, 0.0).astype(BF16)
    for j in range(N_KV_HEADS):
        k_ref[j] = k[:, j * HEAD_DIM:(j + 1) * HEAD_DIM]
        vt_ref[j, 0] = jnp.concatenate([vt[j * HEAD_DIM:(j + 1) * HEAD_DIM, :], pad], axis=0)

    gf_ref[...] = jax.nn.sigmoid(seg(_C_GF, _C_GA) + bg_ref[0:1, :])
    ga_ref[...] = jax.nn.sigmoid(seg(_C_GA, _C_END) + bg_ref[1:2, :])


def _in_proj_call(x, w, bg, qg, kg, cq, sq, ck, sk, cdft, hmq, hmk, tm, seq):
    n, d = x.shape
    pos_blocks = seq // tm
    row = lambda i: (i, 0)
    pos = lambda i: (i % pos_blocks, 0)
    const = lambda i: (0, 0)
    full = lambda a: pl.BlockSpec(a.shape, const)
    return pl.pallas_call(
        _in_proj_kernel,
        grid=(n // tm,),
        in_specs=[pl.BlockSpec((tm, d), row), full(w), full(bg), full(qg), full(kg),
                  pl.BlockSpec((tm, D_ATTN), pos), pl.BlockSpec((tm, D_ATTN), pos),
                  pl.BlockSpec((tm, D_KV), pos), pl.BlockSpec((tm, D_KV), pos),
                  full(cdft), full(hmq), full(hmk)],
        out_specs=[pl.BlockSpec((tm, D_FOURIER), row), pl.BlockSpec((tm, D_FOURIER), row),
                   pl.BlockSpec((tm, D_ATTN), row),
                   pl.BlockSpec((N_KV_HEADS, tm, HEAD_DIM), lambda i: (0, i, 0)),
                   pl.BlockSpec((N_KV_HEADS, 1, V_ROWS, tm), lambda i: (0, i, 0, 0)),
                   pl.BlockSpec((tm, d), row), pl.BlockSpec((tm, d), row)],
        out_shape=[jax.ShapeDtypeStruct((n, D_FOURIER), F32), jax.ShapeDtypeStruct((n, D_FOURIER), F32),
                   jax.ShapeDtypeStruct((n, D_ATTN), BF16),
                   jax.ShapeDtypeStruct((N_KV_HEADS, n, HEAD_DIM), BF16),
                   jax.ShapeDtypeStruct((N_KV_HEADS, n // tm, V_ROWS, tm), BF16),
                   jax.ShapeDtypeStruct((n, d), F32), jax.ShapeDtypeStruct((n, d), F32)],
        compiler_params=_params("parallel"),
        name="in_proj",
    )(x, w, bg, qg, kg, cq, sq, ck, sk, cdft, hmq, hmk)


def _dft_stage1_kernel(zr_ref, zi_ref, m1_ref, tr_ref, ti_ref, o_ref):
    n2, jb, c = zr_ref.shape[1:]
    for j in range(jb):
        zz = jnp.concatenate([zr_ref[0, :, j, :], zi_ref[0, :, j, :]], axis=0).astype(BF16)
        b = _dot(m1_ref[...], zz)
        br, bi = b[:n2], b[n2:]
        reps = c // tr_ref.shape[-1]
        tr, ti = jnp.tile(tr_ref[j], (1, reps)), jnp.tile(ti_ref[j], (1, reps))
        o_ref[0, 0, :, j, :] = br * tr - bi * ti
        o_ref[0, 1, :, j, :] = br * ti + bi * tr


def _dft_stage1_call(z_r, z_i, m1, tr, ti, jb):
    b, n2, n1, c = z_r.shape
    blk = pl.BlockSpec((1, n2, jb, c), lambda i, j: (i, 0, j, 0))
    tw = pl.BlockSpec((jb,) + tr.shape[1:], lambda i, j: (j, 0, 0))
    return pl.pallas_call(
        _dft_stage1_kernel,
        grid=(b, n1 // jb),
        in_specs=[blk, blk, pl.BlockSpec(m1.shape, lambda i, j: (0, 0)), tw, tw],
        out_specs=pl.BlockSpec((1, 2, n2, jb, c), lambda i, j: (i, 0, 0, j, 0)),
        out_shape=jax.ShapeDtypeStruct((b, 2, n2, n1, c), F32),
        compiler_params=_params("parallel", "parallel"),
        name="dft_stage1",
    )(z_r, z_i, m1, tr, ti)


def _dft_stage2_kernel(a_ref, m3_ref, o_ref):
    kb = a_ref.shape[2]
    for j in range(kb):
        rhs = jnp.concatenate([a_ref[0, 0, j], a_ref[0, 1, j]], axis=0).astype(BF16)
        o_ref[0, :, j, :] = _dot(m3_ref[...], rhs)


def _dft_stage2_call(a, m3, kb):
    b, _, n2, n1, c = a.shape
    return pl.pallas_call(
        _dft_stage2_kernel,
        grid=(b, n2 // kb),
        in_specs=[pl.BlockSpec((1, 2, kb, n1, c), lambda i, j: (i, 0, j, 0, 0)),
                  pl.BlockSpec(m3.shape, lambda i, j: (0, 0))],
        out_specs=pl.BlockSpec((1, n1, kb, c), lambda i, j: (i, 0, j, 0)),
        out_shape=jax.ShapeDtypeStruct((b, n1, n2, c), F32),
        compiler_params=_params("parallel", "parallel"),
        name="dft_stage2",
    )(a, m3)


def _dft_constants(seq):
    c = FOURIER_GROUP_DIM
    n1, n2 = seq // DFT_N2, DFT_N2
    jc = np.arange(c)
    ang = 2.0 * np.pi * np.outer(jc, jc) / c
    cdft = np.concatenate([np.cos(ang), -np.sin(ang)], axis=1) / math.sqrt(c)
    j2 = np.arange(n2)
    a2 = 2.0 * np.pi * np.outer(j2, j2) / n2
    fr, fi = np.cos(a2) / math.sqrt(n2), -np.sin(a2) / math.sqrt(n2)
    m1 = np.block([[fr, -fi], [fi, fr]])
    j1 = np.arange(n1)
    a1 = 2.0 * np.pi * np.outer(j1, j1) / n1
    m3 = np.concatenate([np.cos(a1), np.sin(a1)], axis=1) / math.sqrt(n1)
    at = 2.0 * np.pi * np.outer(j1, j2) / seq
    lanes = np.ones((1, 1, 128))
    return (jnp.asarray(cdft, BF16), jnp.asarray(m1, BF16), jnp.asarray(m3, BF16),
            jnp.asarray(np.cos(at)[:, :, None] * lanes, F32), jnp.asarray(-np.sin(at)[:, :, None] * lanes, F32))


def _attn_kernel(q_ref, k_ref, vt_ref, o_ref, qs_ref, st_ref, p_ref, acc_ref, *, tq):
    n_chunks, _, tk = vt_ref.shape[1:]
    slots = st_ref.shape[0]
    assert slots == 3 and n_chunks >= 4 and (n_chunks - 4) % slots == 0
    nq = Q_PER_KV * tq
    heads = range(N_KV_HEADS)
    for j in heads:
        for g in range(Q_PER_KV):
            h = j * Q_PER_KV + g
            qs_ref[j, g * tq:(g + 1) * tq, :] = q_ref[:, h * HEAD_DIM:(h + 1) * HEAD_DIM]
    acc_ref[...] = jnp.zeros_like(acc_ref)

    def scores(j, c, slot):
        start = pl.multiple_of(c * tk, tk)
        st_ref[slot, j] = _dot_nt(k_ref[j, pl.ds(start, tk), :], qs_ref[j])

    def softmax(j, slot, m):
        st = st_ref[slot, j]
        m_new = jnp.maximum(m, jnp.max(st, axis=0, keepdims=True))
        p_ref[slot, j] = jnp.exp2(st - m_new).astype(BF16)
        return m_new, jnp.exp2(m - m_new)

    def values(j, c, slot, alpha):
        acc_ref[j] = alpha * acc_ref[j] + _dot(vt_ref[j, c], p_ref[slot, j])

    def step(i, slot, carry, do_values=True, do_scores=True):
        if do_values:
            for j in heads:
                values(j, i - 2, (slot + 1) % slots, carry[j][2])
        new = []
        for j in heads:
            m, alpha = softmax(j, slot, carry[j][0])
            new.append((m, alpha, carry[j][1]))
        if do_scores:
            for j in heads:
                scores(j, i + 2, (slot + 2) % slots)
        return tuple(new)

    for j in heads:
        scores(j, 0, 0)
        scores(j, 1, 1)
    zero = jnp.zeros((1, nq), F32)
    carry = tuple((jnp.full((1, nq), -jnp.inf, F32), zero, zero) for _ in heads)
    carry = step(0, 0, carry, do_values=False)
    carry = step(1, 1, carry, do_values=False)

    def body(g, carry):
        i = 2 + slots * g
        for k in range(slots):
            carry = step(i + k, (2 + k) % slots, carry)
        return carry

    carry = lax.fori_loop(0, (n_chunks - 4) // slots, body, carry)
    for i in (n_chunks - 2, n_chunks - 1):
        carry = step(i, i % slots, carry, do_scores=False)
    pieces = []
    for j in heads:
        values(j, n_chunks - 2, (n_chunks - 2) % slots, carry[j][2])
        values(j, n_chunks - 1, (n_chunks - 1) % slots, carry[j][1])
        ot = acc_ref[j, :HEAD_DIM, :] / acc_ref[j, HEAD_DIM:HEAD_DIM + 1, :]
        o = jnp.concatenate([ot, jnp.zeros_like(ot)], axis=0).T
        pieces += [o[g * tq:(g + 1) * tq, :HEAD_DIM] for g in range(Q_PER_KV)]
    o_ref[...] = jnp.concatenate(pieces, axis=1)


def _attn_call(q, k, vt, batch, seq, tq):
    n = q.shape[0]
    qblocks = seq // tq
    tk = vt.shape[-1]
    chunks = seq // tk
    nq = Q_PER_KV * tq
    return pl.pallas_call(
        functools.partial(_attn_kernel, tq=tq),
        grid=(batch, qblocks),
        in_specs=[pl.BlockSpec((tq, D_ATTN), lambda b, i: (b * qblocks + i, 0)),
                  pl.BlockSpec((N_KV_HEADS, seq, HEAD_DIM), lambda b, i: (0, b, 0)),
                  pl.BlockSpec((N_KV_HEADS, chunks, V_ROWS, tk), lambda b, i: (0, b, 0, 0))],
        out_specs=pl.BlockSpec((tq, D_ATTN), lambda b, i: (b * qblocks + i, 0)),
        out_shape=jax.ShapeDtypeStruct((n, D_ATTN), F32),
        scratch_shapes=[pltpu.VMEM((N_KV_HEADS, nq, HEAD_DIM), BF16),
                        pltpu.VMEM((3, N_KV_HEADS, tk, nq), F32),
                        pltpu.VMEM((3, N_KV_HEADS, tk, nq), BF16),
                        pltpu.VMEM((N_KV_HEADS, V_ROWS, nq), F32)],
        compiler_params=_params("parallel", "parallel"),
        name="attention",
    )(q, k, vt)


def _mix_kernel(yf_ref, ya_ref, gf_ref, ga_ref, x_ref, wf_ref, wa_ref, wo_ref, g_ref, b_ref, o_ref, *, alpha):
    mixed = (gf_ref[...] * _dot(yf_ref[...].astype(BF16), wf_ref[...])
             + ga_ref[...] * _dot(ya_ref[...].astype(BF16), wa_ref[...]))
    z = alpha * x_ref[...] + _dot(mixed.astype(BF16), wo_ref[...])
    o_ref[...] = _layer_norm(z, g_ref[...], b_ref[...])


def _mix_call(yf, ya, gf, ga, x, wf, wa, wo, g, b, tm, alpha):
    n, d = x.shape
    row = lambda i: (i, 0)
    const = lambda i: (0, 0)
    return pl.pallas_call(
        functools.partial(_mix_kernel, alpha=alpha),
        grid=(n // tm,),
        in_specs=[pl.BlockSpec((tm, D_FOURIER), row), pl.BlockSpec((tm, D_ATTN), row),
                  pl.BlockSpec((tm, d), row), pl.BlockSpec((tm, d), row), pl.BlockSpec((tm, d), row),
                  pl.BlockSpec(wf.shape, const), pl.BlockSpec(wa.shape, const), pl.BlockSpec(wo.shape, const),
                  pl.BlockSpec((1, d), const), pl.BlockSpec((1, d), const)],
        out_specs=pl.BlockSpec((tm, d), row),
        out_shape=jax.ShapeDtypeStruct((n, d), F32),
        compiler_params=_params("parallel"),
        name="mix",
    )(yf, ya, gf, ga, x, wf, wa, wo, g.reshape(1, d), b.reshape(1, d))


_NEG_INF = float("-inf")


def _extract_top(s, order, count, exact):
    rank = jnp.full(s.shape, float(count), F32)
    big = float(2 ** 20)
    vals = []
    for a in range(count):
        m = jnp.max(s, axis=0, keepdims=True)
        if exact:
            first = jnp.min(jnp.where(s == m, order, big), axis=0, keepdims=True)
            sel = order == first
        else:
            sel = s == m
        rank = jnp.where(sel, float(a), rank)
        s = jnp.where(sel, _NEG_INF, s)
        vals.append(m)
    extracted = jnp.sum(jnp.where(rank < float(count), 1.0, 0.0), axis=0, keepdims=True)
    return vals, rank, jnp.where(extracted == float(count), 0.0, 1.0)


def _candidate_rows(v1, v2, combine):
    v2_all = jnp.concatenate(v2, axis=0)
    v2_lo = v2_all[:8]
    pieces = [combine(v1[0], v2_all)]
    pieces += [combine(v1[a], v2_lo) for a in range(1, 8)]
    pieces.append(combine(jnp.concatenate(v1[8:], axis=0), v2[0]))
    return jnp.concatenate(pieces, axis=0)


def _candidate_positions(tokens):
    pos = ([b for b in range(16)] + [a * 16 + b for a in range(1, 8) for b in range(8)]
           + [a * 16 for a in range(8, 16)])
    col = jnp.asarray(np.asarray(pos, np.float32).reshape(-1, 1))
    return jnp.broadcast_to(col, (len(pos), tokens))


def _peer_route_kernel(x_ref, wq_ref, keys_ref, pos_ref, g1_ref, g2_ref):
    ambiguous = _route_tile(x_ref, wq_ref, keys_ref, pos_ref, g1_ref, g2_ref, exact=False)

    @pl.when(jnp.max(ambiguous) > 0.0)
    def _():
        _route_tile(x_ref, wq_ref, keys_ref, pos_ref, g1_ref, g2_ref, exact=True)


def _route_tile(x_ref, wq_ref, keys_ref, pos_ref, g1_ref, g2_ref, exact):
    tokens = x_ref.shape[0]
    xb = x_ref[...].astype(BF16)
    key_order = lax.broadcasted_iota(jnp.int32, (PEER_N_KEYS, tokens), 0).astype(F32)
    cand_pos = pos_ref[...]
    ambiguous = jnp.zeros((1, tokens), F32)
    for h in range(PEER_HEADS):
        col = 2 * h * PEER_HALF
        qp = _dot(xb, wq_ref[:, col:col + 2 * PEER_HALF]).astype(BF16)
        scores, vals, ranks = [], [], []
        for c in range(2):
            s = _dot_nt(keys_ref[h, c], qp[:, c * PEER_HALF:(c + 1) * PEER_HALF])
            v, r, amb = _extract_top(s, key_order, PEER_TOPK, exact)
            ambiguous = jnp.maximum(ambiguous, amb)
            scores.append(s)
            vals.append(v)
            ranks.append(r)
        v1, v2 = vals
        cand = _candidate_rows(v1, v2, lambda a, b: a + b)
        _, crank, amb = _extract_top(cand, cand_pos, PEER_TOPK, exact)
        ambiguous = jnp.maximum(ambiguous, amb)
        chosen = crank < float(PEER_TOPK)
        e1 = [jnp.exp(v - v1[0]) for v in v1]
        e2 = [jnp.exp(v - v2[0]) for v in v2]
        ecand = _candidate_rows(e1, e2, lambda a, b: a * b)
        z = jnp.sum(jnp.where(chosen, ecand, 0.0), axis=0, keepdims=True)
        cnt = jnp.where(chosen, 1.0, 0.0)
        limit = [jnp.sum(cnt[0:16], axis=0, keepdims=True)]
        limit += [jnp.sum(cnt[8 + 8 * a:16 + 8 * a], axis=0, keepdims=True) for a in range(1, 8)]
        limit += [cnt[72 + a:73 + a] for a in range(8)]
        lim_dense = jnp.zeros((PEER_N_KEYS, tokens), F32)
        for a in range(PEER_TOPK):
            lim_dense = jnp.where(ranks[0] == float(a), limit[a], lim_dense)
        g1_ref[0, h] = jnp.exp(scores[0] - v1[0]) / z
        g1_ref[1, h] = lim_dense
        g2_ref[0, h] = jnp.exp(scores[1] - v2[0]).astype(BF16)
        g2_ref[1, h] = ranks[1].astype(BF16)
    return ambiguous


def _peer_route_call(x, wq, keys, tm):
    n, d = x.shape
    pos = _candidate_positions(tm)
    return pl.pallas_call(
        _peer_route_kernel,
        grid=(n // tm,),
        in_specs=[pl.BlockSpec((tm, d), lambda i: (i, 0)),
                  pl.BlockSpec(wq.shape, lambda i: (0, 0)),
                  pl.BlockSpec(keys.shape, lambda i: (0, 0, 0, 0)),
                  pl.BlockSpec(pos.shape, lambda i: (0, 0))],
        out_specs=[pl.BlockSpec((2, PEER_HEADS, PEER_N_KEYS, tm), lambda i: (0, 0, 0, i)),
                   pl.BlockSpec((2, PEER_HEADS, PEER_N_KEYS, tm), lambda i: (0, 0, 0, i))],
        out_shape=[jax.ShapeDtypeStruct((2, PEER_HEADS, PEER_N_KEYS, n), F32),
                   jax.ShapeDtypeStruct((2, PEER_HEADS, PEER_N_KEYS, n), BF16)],
        compiler_params=_params("parallel"),
        name="peer_route",
    )(x, wq, keys, pos)


def _peer_expert_kernel(x_ref, g1_ref, g2_ref, u_ref, vt_ref, g_ref, b_ref, o_ref, xb_ref, acc_ref, p_ref, *,
                        alpha, rows_per_step):
    j = pl.program_id(1)

    @pl.when(j == 0)
    def _():
        xb_ref[...] = x_ref[...].astype(BF16)
        acc_ref[...] = jnp.zeros_like(acc_ref)

    act = _dot_nt(u_ref[...], xb_ref[...])
    act = 0.5 * act * (1.0 + lax.erf(act * math.sqrt(0.5)))
    for r in range(rows_per_step):
        i1 = j * rows_per_step + r
        gate = None
        for h in range(PEER_HEADS):
            e1n = g1_ref[0, h, pl.ds(i1, 1), :].astype(BF16)
            lim = g1_ref[1, h, pl.ds(i1, 1), :].astype(BF16)
            term = jnp.where(g2_ref[1, h] < lim, g2_ref[0, h] * e1n, jnp.zeros((), BF16))
            gate = term if gate is None else gate + term
        lo, hi = r * PEER_N_KEYS, (r + 1) * PEER_N_KEYS
        p_ref[lo:hi, :] = gate * act[lo:hi, :].astype(BF16)
    acc_ref[...] += _dot(vt_ref[...], p_ref[...])

    @pl.when(j == pl.num_programs(1) - 1)
    def _():
        z = alpha * x_ref[...] + acc_ref[...].T
        o_ref[...] = _layer_norm(z, g_ref[...], b_ref[...])


def _peer_expert_call(x, g1, g2, u, vt, g, b, tm, rows_per_step, alpha):
    n, d = x.shape
    te = rows_per_step * PEER_N_KEYS
    n_exp = u.shape[0]
    return pl.pallas_call(
        functools.partial(_peer_expert_kernel, alpha=alpha, rows_per_step=rows_per_step),
        grid=(n // tm, n_exp // te),
        in_specs=[pl.BlockSpec((tm, d), lambda i, j: (i, 0)),
                  pl.BlockSpec((2, PEER_HEADS, PEER_N_KEYS, tm), lambda i, j: (0, 0, 0, i)),
                  pl.BlockSpec((2, PEER_HEADS, PEER_N_KEYS, tm), lambda i, j: (0, 0, 0, i)),
                  pl.BlockSpec((te, d), lambda i, j: (j, 0)),
                  pl.BlockSpec((d, te), lambda i, j: (0, j)),
                  pl.BlockSpec((1, d), lambda i, j: (0, 0)),
                  pl.BlockSpec((1, d), lambda i, j: (0, 0))],
        out_specs=pl.BlockSpec((tm, d), lambda i, j: (i, 0)),
        out_shape=jax.ShapeDtypeStruct((n, d), F32),
        scratch_shapes=[pltpu.VMEM((tm, d), BF16), pltpu.VMEM((d, tm), F32), pltpu.VMEM((te, tm), BF16)],
        compiler_params=_params("parallel", "arbitrary"),
        name="peer_experts",
    )(x, g1, g2, u, vt, g.reshape(1, d), b.reshape(1, d))


def _rope_tables(seq):
    rows = seq // GRID_W
    row_id = jnp.repeat(jnp.arange(rows, dtype=F32), GRID_W)
    col_id = jnp.tile(jnp.arange(GRID_W, dtype=F32), rows)
    inv_freq = 1.0 / (ROPE_THETA ** (jnp.arange(0, ROPE_AXIS_DIM, 2, dtype=F32) / ROPE_AXIS_DIM))
    ang_r = row_id[:, None] * inv_freq[None, :]
    ang_c = col_id[:, None] * inv_freq[None, :]
    ang = jnp.concatenate([ang_r, ang_r, ang_c, ang_c], axis=-1)
    sign = jnp.where((jnp.arange(HEAD_DIM) % ROPE_AXIS_DIM) < ROPE_HALF, -1.0, 1.0).astype(F32)
    return jnp.cos(ang), jnp.sin(ang) * sign[None, :]


def _head_mean_matrix(heads):
    m = np.kron(np.eye(heads), np.full((HEAD_DIM, HEAD_DIM), 1.0 / HEAD_DIM))
    return jnp.asarray(m, BF16)


TM_LN = 512
TM_IN = 512
DFT_BLOCK = 8
ATTN_TQ = 128
TM_MIX = 256
TM_ROUTE = 256
TM_EXPERT = 512
EXPERT_ROWS_PER_STEP = 16


def kernel(x, ln0_g, ln0_b, w_in, b_gate, q_norm_g, k_norm_g, w_branch_fourier, w_branch_attn, w_out,
           ln1_g, ln1_b, peer_w_query, peer_sub_keys, peer_u, peer_v, ln2_g, ln2_b):
    batch, seq, d = x.shape
    n = batch * seq
    depth = w_in.shape[0]
    alpha = (2.0 * depth) ** 0.25
    n1 = seq // DFT_N2

    cos, sin_signed = _rope_tables(seq)
    cq, sq = jnp.tile(cos, (1, N_Q_HEADS)), jnp.tile(sin_signed, (1, N_Q_HEADS))
    ck, sk = jnp.tile(cos, (1, N_KV_HEADS)), jnp.tile(sin_signed, (1, N_KV_HEADS))
    hmq, hmk = _head_mean_matrix(N_Q_HEADS), _head_mean_matrix(N_KV_HEADS)
    cdft, m1, m3, tw_r, tw_i = _dft_constants(seq)

    xs = _ln_call(x.reshape(n, d), ln0_g, ln0_b, TM_LN)
    for l in range(depth):
        qg = jnp.tile(q_norm_g[l], N_Q_HEADS).reshape(1, D_ATTN)
        kg = jnp.tile(k_norm_g[l], N_KV_HEADS).reshape(1, D_KV)
        zr, zi, q, k, vt, gf, ga = _in_proj_call(
            xs, w_in[l].astype(BF16), b_gate[l], qg, kg, cq, sq, ck, sk, cdft, hmq, hmk, TM_IN, seq)
        a = _dft_stage1_call(zr.reshape(batch, DFT_N2, n1, D_FOURIER), zi.reshape(batch, DFT_N2, n1, D_FOURIER),
                             m1, tw_r, tw_i, DFT_BLOCK)
        yf = _dft_stage2_call(a, m3, DFT_BLOCK)
        ya = _attn_call(q, k, vt, batch, seq, ATTN_TQ)
        xs = _mix_call(yf.reshape(n, D_FOURIER), ya, gf, ga, xs,
                       w_branch_fourier[l].astype(BF16), w_branch_attn[l].astype(BF16), w_out[l].astype(BF16),
                       ln1_g[l], ln1_b[l], TM_MIX, alpha)
        g1, g2 = _peer_route_call(xs, peer_w_query[l].astype(BF16), peer_sub_keys[l].astype(BF16), TM_ROUTE)
        xs = _peer_expert_call(xs, g1, g2, peer_u[l].astype(BF16), peer_v[l].astype(BF16).T,
                               ln2_g[l], ln2_b[l], TM_EXPERT, EXPERT_ROWS_PER_STEP, alpha)
    return xs.reshape(batch, seq, d)
```

```python
import functools
import math

import numpy as np
import jax
import jax.numpy as jnp
from jax import lax
from jax.experimental import pallas as pl
from jax.experimental.pallas import tpu as pltpu

F32 = jnp.float32
BF16 = jnp.bfloat16

D_MODEL = 1024
GRID_W = 64
N_FOURIER_GROUPS = 4
FOURIER_GROUP_DIM = 128
D_FOURIER = N_FOURIER_GROUPS * FOURIER_GROUP_DIM
N_Q_HEADS = 8
N_KV_HEADS = 2
Q_PER_KV = N_Q_HEADS // N_KV_HEADS
HEAD_DIM = 64
D_ATTN = N_Q_HEADS * HEAD_DIM
D_KV = N_KV_HEADS * HEAD_DIM
ROPE_AXIS_DIM = HEAD_DIM // 2
ROPE_HALF = ROPE_AXIS_DIM // 2
ROPE_THETA = 10000.0
LOG2_E = math.log2(math.e)
V_ROWS = HEAD_DIM + 16
PEER_HEADS = 8
PEER_N_KEYS = 128
PEER_TOPK = 16
PEER_HALF = 128
LN_EPS = 1e-5
RMS_EPS = 1e-6

_C_XF = 0
_C_Q = D_FOURIER
_C_K = _C_Q + D_ATTN
_C_V = _C_K + D_KV
_C_GF = _C_V + D_KV
_C_GA = _C_GF + D_MODEL
_C_END = _C_GA + D_MODEL

DFT_N2 = 128

VMEM_LIMIT_BYTES = 56 * 1024 * 1024


def _params(*sem):
    return pltpu.CompilerParams(dimension_semantics=sem, vmem_limit_bytes=VMEM_LIMIT_BYTES)


def _dot(a, b):
    return jnp.dot(a, b, preferred_element_type=F32)


def _dot_nt(a, b):
    return lax.dot_general(a, b, (((1,), (1,)), ((), ())), preferred_element_type=F32)


def _layer_norm(z, g, b):
    mu = jnp.mean(z, axis=-1, keepdims=True)
    zc = z - mu
    var = jnp.mean(zc * zc, axis=-1, keepdims=True)
    return zc * lax.rsqrt(var + LN_EPS) * g + b


def _ln_kernel(x_ref, g_ref, b_ref, o_ref):
    o_ref[...] = _layer_norm(x_ref[...], g_ref[...], b_ref[...])


def _ln_call(x, g, b, tm):
    n, d = x.shape
    return pl.pallas_call(
        _ln_kernel,
        grid=(n // tm,),
        in_specs=[pl.BlockSpec((tm, d), lambda i: (i, 0)),
                  pl.BlockSpec((1, d), lambda i: (0, 0)),
                  pl.BlockSpec((1, d), lambda i: (0, 0))],
        out_specs=pl.BlockSpec((tm, d), lambda i: (i, 0)),
        out_shape=jax.ShapeDtypeStruct((n, d), F32),
        compiler_params=_params("parallel"),
        name="ln0",
    )(x, g.reshape(1, d), b.reshape(1, d))


def _rms_rope(t, gain, head_mean, cos, sin_signed):
    sq = t * t
    hi = sq.astype(BF16)
    lo = (sq - hi.astype(F32)).astype(BF16)
    ms = _dot(hi, head_mean) + _dot(lo, head_mean)
    tn = t * lax.rsqrt(ms + RMS_EPS) * gain
    width = t.shape[-1]
    lane = lax.broadcasted_iota(jnp.int32, tn.shape, 1)
    first_half = (lane % ROPE_AXIS_DIM) < ROPE_HALF
    rot = jnp.where(first_half, pltpu.roll(tn, width - ROPE_HALF, 1), pltpu.roll(tn, ROPE_HALF, 1))
    return tn * cos + rot * sin_signed


def _in_proj_kernel(x_ref, w_ref, bg_ref, qg_ref, kg_ref, cq_ref, sq_ref, ck_ref, sk_ref,
                    cdft_ref, hmq_ref, hmk_ref,
                    zr_ref, zi_ref, q_ref, k_ref, vt_ref, gf_ref, ga_ref):
    xb = x_ref[...].astype(BF16)

    def seg(a, b):
        return _dot(xb, w_ref[:, a:b])

    xf = seg(_C_XF, _C_Q).astype(BF16)
    for g in range(N_FOURIER_GROUPS):
        lo, hi = g * FOURIER_GROUP_DIM, (g + 1) * FOURIER_GROUP_DIM
        z = _dot(xf[:, lo:hi], cdft_ref[...])
        zr_ref[:, lo:hi] = z[:, :FOURIER_GROUP_DIM]
        zi_ref[:, lo:hi] = z[:, FOURIER_GROUP_DIM:]

    q = _rms_rope(seg(_C_Q, _C_K), qg_ref[...], hmq_ref[...], cq_ref[...], sq_ref[...])
    q_ref[...] = (q * (HEAD_DIM ** -0.5 * LOG2_E)).astype(BF16)
    k = _rms_rope(seg(_C_K, _C_V), kg_ref[...], hmk_ref[...], ck_ref[...], sk_ref[...]).astype(BF16)
    vt = seg(_C_V, _C_GF).T.astype(BF16)
    pad_row = lax.broadcasted_iota(jnp.int32, (V_ROWS - HEAD_DIM, vt.shape[1]), 0)
    pad = jnp.where(pad_row == 0, 1.0, 0.0).astype(BF16)
    for j in range(N_KV_HEADS):
        k_ref[j] = k[:, j * HEAD_DIM:(j + 1) * HEAD_DIM]
        vt_ref[j, 0] = jnp.concatenate([vt[j * HEAD_DIM:(j + 1) * HEAD_DIM, :], pad], axis=0)

    gf_ref[...] = jax.nn.sigmoid(seg(_C_GF, _C_GA) + bg_ref[0:1, :])
    ga_ref[...] = jax.nn.sigmoid(seg(_C_GA, _C_END) + bg_ref[1:2, :])


def _in_proj_call(x, w, bg, qg, kg, cq, sq, ck, sk, cdft, hmq, hmk, tm, seq):
    n, d = x.shape
    pos_blocks = seq // tm
    row = lambda i: (i, 0)
    pos = lambda i: (i % pos_blocks, 0)
    const = lambda i: (0, 0)
    full = lambda a: pl.BlockSpec(a.shape, const)
    return pl.pallas_call(
        _in_proj_kernel,
        grid=(n // tm,),
        in_specs=[pl.BlockSpec((tm, d), row), full(w), full(bg), full(qg), full(kg),
                  pl.BlockSpec((tm, D_ATTN), pos), pl.BlockSpec((tm, D_ATTN), pos),
                  pl.BlockSpec((tm, D_KV), pos), pl.BlockSpec((tm, D_KV), pos),
                  full(cdft), full(hmq), full(hmk)],
        out_specs=[pl.BlockSpec((tm, D_FOURIER), row), pl.BlockSpec((tm, D_FOURIER), row),
                   pl.BlockSpec((tm, D_ATTN), row),
                   pl.BlockSpec((N_KV_HEADS, tm, HEAD_DIM), lambda i: (0, i, 0)),
                   pl.BlockSpec((N_KV_HEADS, 1, V_ROWS, tm), lambda i: (0, i, 0, 0)),
                   pl.BlockSpec((tm, d), row), pl.BlockSpec((tm, d), row)],
        out_shape=[jax.ShapeDtypeStruct((n, D_FOURIER), F32), jax.ShapeDtypeStruct((n, D_FOURIER), F32),
                   jax.ShapeDtypeStruct((n, D_ATTN), BF16),
                   jax.ShapeDtypeStruct((N_KV_HEADS, n, HEAD_DIM), BF16),
                   jax.ShapeDtypeStruct((N_KV_HEADS, n // tm, V_ROWS, tm), BF16),
                   jax.ShapeDtypeStruct((n, d), F32), jax.ShapeDtypeStruct((n, d), F32)],
        compiler_params=_params("parallel"),
        name="in_proj",
    )(x, w, bg, qg, kg, cq, sq, ck, sk, cdft, hmq, hmk)


def _dft_stage1_kernel(zr_ref, zi_ref, m1_ref, tr_ref, ti_ref, o_ref):
    n2, jb, c = zr_ref.shape[1:]
    for j in range(jb):
        zz = jnp.concatenate([zr_ref[0, :, j, :], zi_ref[0, :, j, :]], axis=0).astype(BF16)
        b = _dot(m1_ref[...], zz)
        br, bi = b[:n2], b[n2:]
        reps = c // tr_ref.shape[-1]
        tr, ti = jnp.tile(tr_ref[j], (1, reps)), jnp.tile(ti_ref[j], (1, reps))
        o_ref[0, 0, :, j, :] = br * tr - bi * ti
        o_ref[0, 1, :, j, :] = br * ti + bi * tr


def _dft_stage1_call(z_r, z_i, m1, tr, ti, jb):
    b, n2, n1, c = z_r.shape
    blk = pl.BlockSpec((1, n2, jb, c), lambda i, j: (i, 0, j, 0))
    tw = pl.BlockSpec((jb,) + tr.shape[1:], lambda i, j: (j, 0, 0))
    return pl.pallas_call(
        _dft_stage1_kernel,
        grid=(b, n1 // jb),
        in_specs=[blk, blk, pl.BlockSpec(m1.shape, lambda i, j: (0, 0)), tw, tw],
        out_specs=pl.BlockSpec((1, 2, n2, jb, c), lambda i, j: (i, 0, 0, j, 0)),
        out_shape=jax.ShapeDtypeStruct((b, 2, n2, n1, c), F32),
        compiler_params=_params("parallel", "parallel"),
        name="dft_stage1",
    )(z_r, z_i, m1, tr, ti)


def _dft_stage2_kernel(a_ref, m3_ref, o_ref):
    kb = a_ref.shape[2]
    for j in range(kb):
        rhs = jnp.concatenate([a_ref[0, 0, j], a_ref[0, 1, j]], axis=0).astype(BF16)
        o_ref[0, :, j, :] = _dot(m3_ref[...], rhs)


def _dft_stage2_call(a, m3, kb):
    b, _, n2, n1, c = a.shape
    return pl.pallas_call(
        _dft_stage2_kernel,
        grid=(b, n2 // kb),
        in_specs=[pl.BlockSpec((1, 2, kb, n1, c), lambda i, j: (i, 0, j, 0, 0)),
                  pl.BlockSpec(m3.shape, lambda i, j: (0, 0))],
        out_specs=pl.BlockSpec((1, n1, kb, c), lambda i, j: (i, 0, j, 0)),
        out_shape=jax.ShapeDtypeStruct((b, n1, n2, c), F32),
        compiler_params=_params("parallel", "parallel"),
        name="dft_stage2",
    )(a, m3)


def _dft_constants(seq):
    c = FOURIER_GROUP_DIM
    n1, n2 = seq // DFT_N2, DFT_N2
    jc = np.arange(c)
    ang = 2.0 * np.pi * np.outer(jc, jc) / c
    cdft = np.concatenate([np.cos(ang), -np.sin(ang)], axis=1) / math.sqrt(c)
    j2 = np.arange(n2)
    a2 = 2.0 * np.pi * np.outer(j2, j2) / n2
    fr, fi = np.cos(a2) / math.sqrt(n2), -np.sin(a2) / math.sqrt(n2)
    m1 = np.block([[fr, -fi], [fi, fr]])
    j1 = np.arange(n1)
    a1 = 2.0 * np.pi * np.outer(j1, j1) / n1
    m3 = np.concatenate([np.cos(a1), np.sin(a1)], axis=1) / math.sqrt(n1)
    at = 2.0 * np.pi * np.outer(j1, j2) / seq
    lanes = np.ones((1, 1, 128))
    return (jnp.asarray(cdft, BF16), jnp.asarray(m1, BF16), jnp.asarray(m3, BF16),
            jnp.asarray(np.cos(at)[:, :, None] * lanes, F32), jnp.asarray(-np.sin(at)[:, :, None] * lanes, F32))


def _attn_kernel(q_ref, k_ref, vt_ref, o_ref, qs_ref, st_ref, p_ref, acc_ref, *, tq):
    n_chunks, _, tk = vt_ref.shape[1:]
    slots = st_ref.shape[0]
    assert slots == 3 and n_chunks >= 4 and (n_chunks - 4) % slots == 0
    nq = Q_PER_KV * tq
    heads = range(N_KV_HEADS)
    for j in heads:
        for g in range(Q_PER_KV):
            h = j * Q_PER_KV + g
            qs_ref[j, g * tq:(g + 1) * tq, :] = q_ref[:, h * HEAD_DIM:(h + 1) * HEAD_DIM]
    acc_ref[...] = jnp.zeros_like(acc_ref)

    def scores(j, c, slot):
        start = pl.multiple_of(c * tk, tk)
        st_ref[slot, j] = _dot_nt(k_ref[j, pl.ds(start, tk), :], qs_ref[j])

    def softmax(j, slot, m):
        st = st_ref[slot, j]
        m_new = jnp.maximum(m, jnp.max(st, axis=0, keepdims=True))
        p_ref[slot, j] = jnp.exp2(st - m_new).astype(BF16)
        return m_new, jnp.exp2(m - m_new)

    def values(j, c, slot, alpha):
        acc_ref[j] = alpha * acc_ref[j] + _dot(vt_ref[j, c], p_ref[slot, j])

    def step(i, slot, carry, do_values=True, do_scores=True):
        if do_values:
            for j in heads:
                values(j, i - 2, (slot + 1) % slots, carry[j][2])
        new = []
        for j in heads:
            m, alpha = softmax(j, slot, carry[j][0])
            new.append((m, alpha, carry[j][1]))
        if do_scores:
            for j in heads:
                scores(j, i + 2, (slot + 2) % slots)
        return tuple(new)

    for j in heads:
        scores(j, 0, 0)
        scores(j, 1, 1)
    zero = jnp.zeros((1, nq), F32)
    carry = tuple((jnp.full((1, nq), -jnp.inf, F32), zero, zero) for _ in heads)
    carry = step(0, 0, carry, do_values=False)
    carry = step(1, 1, carry, do_values=False)

    def body(g, carry):
        i = 2 + slots * g
        for k in range(slots):
            carry = step(i + k, (2 + k) % slots, carry)
        return carry

    carry = lax.fori_loop(0, (n_chunks - 4) // slots, body, carry)
    for i in (n_chunks - 2, n_chunks - 1):
        carry = step(i, i % slots, carry, do_scores=False)
    pieces = []
    for j in heads:
        values(j, n_chunks - 2, (n_chunks - 2) % slots, carry[j][2])
        values(j, n_chunks - 1, (n_chunks - 1) % slots, carry[j][1])
        ot = acc_ref[j, :HEAD_DIM, :] / acc_ref[j, HEAD_DIM:HEAD_DIM + 1, :]
        o = jnp.concatenate([ot, jnp.zeros_like(ot)], axis=0).T
        pieces += [o[g * tq:(g + 1) * tq, :HEAD_DIM] for g in range(Q_PER_KV)]
    o_ref[...] = jnp.concatenate(pieces, axis=1)


def _attn_call(q, k, vt, batch, seq, tq):
    n = q.shape[0]
    qblocks = seq // tq
    tk = vt.shape[-1]
    chunks = seq // tk
    nq = Q_PER_KV * tq
    return pl.pallas_call(
        functools.partial(_attn_kernel, tq=tq),
        grid=(batch, qblocks),
        in_specs=[pl.BlockSpec((tq, D_ATTN), lambda b, i: (b * qblocks + i, 0)),
                  pl.BlockSpec((N_KV_HEADS, seq, HEAD_DIM), lambda b, i: (0, b, 0)),
                  pl.BlockSpec((N_KV_HEADS, chunks, V_ROWS, tk), lambda b, i: (0, b, 0, 0))],
        out_specs=pl.BlockSpec((tq, D_ATTN), lambda b, i: (b * qblocks + i, 0)),
        out_shape=jax.ShapeDtypeStruct((n, D_ATTN), F32),
        scratch_shapes=[pltpu.VMEM((N_KV_HEADS, nq, HEAD_DIM), BF16),
                        pltpu.VMEM((3, N_KV_HEADS, tk, nq), F32),
                        pltpu.VMEM((3, N_KV_HEADS, tk, nq), BF16),
                        pltpu.VMEM((N_KV_HEADS, V_ROWS, nq), F32)],
        compiler_params=_params("parallel", "parallel"),
        name="attention",
    )(q, k, vt)


def _mix_kernel(yf_ref, ya_ref, gf_ref, ga_ref, x_ref, wf_ref, wa_ref, wo_ref, g_ref, b_ref, o_ref, *, alpha):
    mixed = (gf_ref[...] * _dot(yf_ref[...].astype(BF16), wf_ref[...])
             + ga_ref[...] * _dot(ya_ref[...].astype(BF16), wa_ref[...]))
    z = alpha * x_ref[...] + _dot(mixed.astype(BF16), wo_ref[...])
    o_ref[...] = _layer_norm(z, g_ref[...], b_ref[...])


def _mix_call(yf, ya, gf, ga, x, wf, wa, wo, g, b, tm, alpha):
    n, d = x.shape
    row = lambda i: (i, 0)
    const = lambda i: (0, 0)
    return pl.pallas_call(
        functools.partial(_mix_kernel, alpha=alpha),
        grid=(n // tm,),
        in_specs=[pl.BlockSpec((tm, D_FOURIER), row), pl.BlockSpec((tm, D_ATTN), row),
                  pl.BlockSpec((tm, d), row), pl.BlockSpec((tm, d), row), pl.BlockSpec((tm, d), row),
                  pl.BlockSpec(wf.shape, const), pl.BlockSpec(wa.shape, const), pl.BlockSpec(wo.shape, const),
                  pl.BlockSpec((1, d), const), pl.BlockSpec((1, d), const)],
        out_specs=pl.BlockSpec((tm, d), row),
        out_shape=jax.ShapeDtypeStruct((n, d), F32),
        compiler_params=_params("parallel"),
        name="mix",
    )(yf, ya, gf, ga, x, wf, wa, wo, g.reshape(1, d), b.reshape(1, d))


_NEG_INF = float("-inf")


def _extract_top(s, order, count, exact):
    rank = jnp.full(s.shape, float(count), F32)
    big = float(2 ** 20)
    vals = []
    for a in range(count):
        m = jnp.max(s, axis=0, keepdims=True)
        if exact:
            first = jnp.min(jnp.where(s == m, order, big), axis=0, keepdims=True)
            sel = order == first
        else:
            sel = s == m
        rank = jnp.where(sel, float(a), rank)
        s = jnp.where(sel, _NEG_INF, s)
        vals.append(m)
    extracted = jnp.sum(jnp.where(rank < float(count), 1.0, 0.0), axis=0, keepdims=True)
    return vals, rank, jnp.where(extracted == float(count), 0.0, 1.0)


def _candidate_rows(v1, v2, combine):
    v2_all = jnp.concatenate(v2, axis=0)
    v2_lo = v2_all[:8]
    pieces = [combine(v1[0], v2_all)]
    pieces += [combine(v1[a], v2_lo) for a in range(1, 8)]
    pieces.append(combine(jnp.concatenate(v1[8:], axis=0), v2[0]))
    return jnp.concatenate(pieces, axis=0)


def _candidate_positions(tokens):
    pos = ([b for b in range(16)] + [a * 16 + b for a in range(1, 8) for b in range(8)]
           + [a * 16 for a in range(8, 16)])
    col = jnp.asarray(np.asarray(pos, np.float32).reshape(-1, 1))
    return jnp.broadcast_to(col, (len(pos), tokens))


def _peer_route_kernel(x_ref, wq_ref, keys_ref, pos_ref, g1_ref, g2_ref):
    ambiguous = _route_tile(x_ref, wq_ref, keys_ref, pos_ref, g1_ref, g2_ref, exact=False)

    @pl.when(jnp.max(ambiguous) > 0.0)
    def _():
        _route_tile(x_ref, wq_ref, keys_ref, pos_ref, g1_ref, g2_ref, exact=True)


def _route_tile(x_ref, wq_ref, keys_ref, pos_ref, g1_ref, g2_ref, exact):
    tokens = x_ref.shape[0]
    xb = x_ref[...].astype(BF16)
    key_order = lax.broadcasted_iota(jnp.int32, (PEER_N_KEYS, tokens), 0).astype(F32)
    cand_pos = pos_ref[...]
    ambiguous = jnp.zeros((1, tokens), F32)
    for h in range(PEER_HEADS):
        col = 2 * h * PEER_HALF
        qp = _dot(xb, wq_ref[:, col:col + 2 * PEER_HALF]).astype(BF16)
        scores, vals, ranks = [], [], []
        for c in range(2):
            s = _dot_nt(keys_ref[h, c], qp[:, c * PEER_HALF:(c + 1) * PEER_HALF])
            v, r, amb = _extract_top(s, key_order, PEER_TOPK, exact)
            ambiguous = jnp.maximum(ambiguous, amb)
            scores.append(s)
            vals.append(v)
            ranks.append(r)
        v1, v2 = vals
        cand = _candidate_rows(v1, v2, lambda a, b: a + b)
        _, crank, amb = _extract_top(cand, cand_pos, PEER_TOPK, exact)
        ambiguous = jnp.maximum(ambiguous, amb)
        chosen = crank < float(PEER_TOPK)
        e1 = [jnp.exp(v - v1[0]) for v in v1]
        e2 = [jnp.exp(v - v2[0]) for v in v2]
        ecand = _candidate_rows(e1, e2, lambda a, b: a * b)
        z = jnp.sum(jnp.where(chosen, ecand, 0.0), axis=0, keepdims=True)
        cnt = jnp.where(chosen, 1.0, 0.0)
        limit = [jnp.sum(cnt[0:16], axis=0, keepdims=True)]
        limit += [jnp.sum(cnt[8 + 8 * a:16 + 8 * a], axis=0, keepdims=True) for a in range(1, 8)]
        limit += [cnt[72 + a:73 + a] for a in range(8)]
        lim_dense = jnp.zeros((PEER_N_KEYS, tokens), F32)
        for a in range(PEER_TOPK):
            lim_dense = jnp.where(ranks[0] == float(a), limit[a], lim_dense)
        g1_ref[0, h] = jnp.exp(scores[0] - v1[0]) / z
        g1_ref[1, h] = lim_dense
        g2_ref[0, h] = jnp.exp(scores[1] - v2[0]).astype(BF16)
        g2_ref[1, h] = ranks[1].astype(BF16)
    return ambiguous


def _peer_route_call(x, wq, keys, tm):
    n, d = x.shape
    pos = _candidate_positions(tm)
    return pl.pallas_call(
        _peer_route_kernel,
        grid=(n // tm,),
        in_specs=[pl.BlockSpec((tm, d), lambda i: (i, 0)),
                  pl.BlockSpec(wq.shape, lambda i: (0, 0)),
                  pl.BlockSpec(keys.shape, lambda i: (0, 0, 0, 0)),
                  pl.BlockSpec(pos.shape, lambda i: (0, 0))],
        out_specs=[pl.BlockSpec((2, PEER_HEADS, PEER_N_KEYS, tm), lambda i: (0, 0, 0, i)),
                   pl.BlockSpec((2, PEER_HEADS, PEER_N_KEYS, tm), lambda i: (0, 0, 0, i))],
        out_shape=[jax.ShapeDtypeStruct((2, PEER_HEADS, PEER_N_KEYS, n), F32),
                   jax.ShapeDtypeStruct((2, PEER_HEADS, PEER_N_KEYS, n), BF16)],
        compiler_params=_params("parallel"),
        name="peer_route",
    )(x, wq, keys, pos)


def _peer_expert_kernel(x_ref, g1_ref, g2_ref, u_ref, v_ref, g_ref, b_ref, o_ref, xb_ref, acc_ref, p_ref, *,
                        alpha, rows_per_step):
    j = pl.program_id(1)

    @pl.when(j == 0)
    def _():
        xb_ref[...] = x_ref[...].astype(BF16)
        acc_ref[...] = jnp.zeros_like(acc_ref)

    act = _dot_nt(u_ref[...], xb_ref[...])
    act = 0.5 * act * (1.0 + lax.erf(act * math.sqrt(0.5)))
    for r in range(rows_per_step):
        i1 = j * rows_per_step + r
        gate = None
        for h in range(PEER_HEADS):
            e1n = g1_ref[0, h, pl.ds(i1, 1), :].astype(BF16)
            lim = g1_ref[1, h, pl.ds(i1, 1), :].astype(BF16)
            term = jnp.where(g2_ref[1, h] < lim, g2_ref[0, h] * e1n, jnp.zeros((), BF16))
            gate = term if gate is None else gate + term
        lo, hi = r * PEER_N_KEYS, (r + 1) * PEER_N_KEYS
        p_ref[lo:hi, :] = gate * act[lo:hi, :].astype(BF16)
    acc_ref[...] += lax.dot_general(v_ref[...], p_ref[...], (((0,), (0,)), ((), ())),
                                    preferred_element_type=F32)

    @pl.when(j == pl.num_programs(1) - 1)
    def _():
        z = alpha * x_ref[...] + acc_ref[...].T
        o_ref[...] = _layer_norm(z, g_ref[...], b_ref[...])


def _peer_expert_call(x, g1, g2, u, vt, g, b, tm, rows_per_step, alpha):
    n, d = x.shape
    te = rows_per_step * PEER_N_KEYS
    n_exp = u.shape[0]
    return pl.pallas_call(
        functools.partial(_peer_expert_kernel, alpha=alpha, rows_per_step=rows_per_step),
        grid=(n // tm, n_exp // te),
        in_specs=[pl.BlockSpec((tm, d), lambda i, j: (i, 0)),
                  pl.BlockSpec((2, PEER_HEADS, PEER_N_KEYS, tm), lambda i, j: (0, 0, 0, i)),
                  pl.BlockSpec((2, PEER_HEADS, PEER_N_KEYS, tm), lambda i, j: (0, 0, 0, i)),
                  pl.BlockSpec((te, d), lambda i, j: (j, 0)),
                  pl.BlockSpec((te, d), lambda i, j: (j, 0)),
                  pl.BlockSpec((1, d), lambda i, j: (0, 0)),
                  pl.BlockSpec((1, d), lambda i, j: (0, 0))],
        out_specs=pl.BlockSpec((tm, d), lambda i, j: (i, 0)),
        out_shape=jax.ShapeDtypeStruct((n, d), F32),
        scratch_shapes=[pltpu.VMEM((tm, d), BF16), pltpu.VMEM((d, tm), F32), pltpu.VMEM((te, tm), BF16)],
        compiler_params=_params("parallel", "arbitrary"),
        name="peer_experts",
    )(x, g1, g2, u, vt, g.reshape(1, d), b.reshape(1, d))


def _rope_tables(seq):
    rows = seq // GRID_W
    row_id = jnp.repeat(jnp.arange(rows, dtype=F32), GRID_W)
    col_id = jnp.tile(jnp.arange(GRID_W, dtype=F32), rows)
    inv_freq = 1.0 / (ROPE_THETA ** (jnp.arange(0, ROPE_AXIS_DIM, 2, dtype=F32) / ROPE_AXIS_DIM))
    ang_r = row_id[:, None] * inv_freq[None, :]
    ang_c = col_id[:, None] * inv_freq[None, :]
    ang = jnp.concatenate([ang_r, ang_r, ang_c, ang_c], axis=-1)
    sign = jnp.where((jnp.arange(HEAD_DIM) % ROPE_AXIS_DIM) < ROPE_HALF, -1.0, 1.0).astype(F32)
    return jnp.cos(ang), jnp.sin(ang) * sign[None, :]


def _head_mean_matrix(heads):
    m = np.kron(np.eye(heads), np.full((HEAD_DIM, HEAD_DIM), 1.0 / HEAD_DIM))
    return jnp.asarray(m, BF16)


TM_LN = 512
TM_IN = 512
DFT_BLOCK = 8
ATTN_TQ = 128
TM_MIX = 256
TM_ROUTE = 256
TM_EXPERT = 512
EXPERT_ROWS_PER_STEP = 16


def kernel(x, ln0_g, ln0_b, w_in, b_gate, q_norm_g, k_norm_g, w_branch_fourier, w_branch_attn, w_out,
           ln1_g, ln1_b, peer_w_query, peer_sub_keys, peer_u, peer_v, ln2_g, ln2_b):
    batch, seq, d = x.shape
    n = batch * seq
    depth = w_in.shape[0]
    alpha = (2.0 * depth) ** 0.25
    n1 = seq // DFT_N2

    cos, sin_signed = _rope_tables(seq)
    cq, sq = jnp.tile(cos, (1, N_Q_HEADS)), jnp.tile(sin_signed, (1, N_Q_HEADS))
    ck, sk = jnp.tile(cos, (1, N_KV_HEADS)), jnp.tile(sin_signed, (1, N_KV_HEADS))
    hmq, hmk = _head_mean_matrix(N_Q_HEADS), _head_mean_matrix(N_KV_HEADS)
    cdft, m1, m3, tw_r, tw_i = _dft_constants(seq)

    xs = _ln_call(x.reshape(n, d), ln0_g, ln0_b, TM_LN)
    for l in range(depth):
        qg = jnp.tile(q_norm_g[l], N_Q_HEADS).reshape(1, D_ATTN)
        kg = jnp.tile(k_norm_g[l], N_KV_HEADS).reshape(1, D_KV)
        zr, zi, q, k, vt, gf, ga = _in_proj_call(
            xs, w_in[l].astype(BF16), b_gate[l], qg, kg, cq, sq, ck, sk, cdft, hmq, hmk, TM_IN, seq)
        a = _dft_stage1_call(zr.reshape(batch, DFT_N2, n1, D_FOURIER), zi.reshape(batch, DFT_N2, n1, D_FOURIER),
                             m1, tw_r, tw_i, DFT_BLOCK)
        yf = _dft_stage2_call(a, m3, DFT_BLOCK)
        ya = _attn_call(q, k, vt, batch, seq, ATTN_TQ)
        xs = _mix_call(yf.reshape(n, D_FOURIER), ya, gf, ga, xs,
                       w_branch_fourier[l].astype(BF16), w_branch_attn[l].astype(BF16), w_out[l].astype(BF16),
                       ln1_g[l], ln1_b[l], TM_MIX, alpha)
        g1, g2 = _peer_route_call(xs, peer_w_query[l].astype(BF16), peer_sub_keys[l].astype(BF16), TM_ROUTE)
        xs = _peer_expert_call(xs, g1, g2, peer_u[l].astype(BF16), peer_v[l].astype(BF16),
                               ln2_g[l], ln2_b[l], TM_EXPERT, EXPERT_ROWS_PER_STEP, alpha)
    return xs.reshape(batch, seq, d)
```

```python
import functools
import math

import numpy as np
import jax
import jax.numpy as jnp
from jax import lax
from jax.experimental import pallas as pl
from jax.experimental.pallas import tpu as pltpu

F32 = jnp.float32
BF16 = jnp.bfloat16

D_MODEL = 1024
GRID_W = 64
N_FOURIER_GROUPS = 4
FOURIER_GROUP_DIM = 128
D_FOURIER = N_FOURIER_GROUPS * FOURIER_GROUP_DIM
N_Q_HEADS = 8
N_KV_HEADS = 2
Q_PER_KV = N_Q_HEADS // N_KV_HEADS
HEAD_DIM = 64
D_ATTN = N_Q_HEADS * HEAD_DIM
D_KV = N_KV_HEADS * HEAD_DIM
ROPE_AXIS_DIM = HEAD_DIM // 2
ROPE_HALF = ROPE_AXIS_DIM // 2
ROPE_THETA = 10000.0
LOG2_E = math.log2(math.e)
V_ROWS = HEAD_DIM + 16
PEER_HEADS = 8
PEER_N_KEYS = 128
PEER_TOPK = 16
PEER_HALF = 128
LN_EPS = 1e-5
RMS_EPS = 1e-6

_C_XF = 0
_C_Q = D_FOURIER
_C_K = _C_Q + D_ATTN
_C_V = _C_K + D_KV
_C_GF = _C_V + D_KV
_C_GA = _C_GF + D_MODEL
_C_END = _C_GA + D_MODEL

DFT_N2 = 128

VMEM_LIMIT_BYTES = 56 * 1024 * 1024
LANES = 128
BF16_ROWS = 16
GATE_ROWS = 4
MXU_PIECE = 256


def _params(*sem):
    return pltpu.CompilerParams(dimension_semantics=sem, vmem_limit_bytes=VMEM_LIMIT_BYTES)


def _dot(a, b):
    return jnp.dot(a, b, preferred_element_type=F32)


def _dot_nt(a, b):
    return lax.dot_general(a, b, (((1,), (1,)), ((), ())), preferred_element_type=F32)


def _layer_norm(z, g, b):
    mu = jnp.mean(z, axis=-1, keepdims=True)
    zc = z - mu
    var = jnp.mean(zc * zc, axis=-1, keepdims=True)
    return zc * lax.rsqrt(var + LN_EPS) * g + b


def _ln_kernel(x_ref, g_ref, b_ref, o_ref):
    o_ref[...] = _layer_norm(x_ref[...], g_ref[...], b_ref[...])


def _ln_call(x, g, b, tm):
    n, d = x.shape
    return pl.pallas_call(
        _ln_kernel,
        grid=(n // tm,),
        in_specs=[pl.BlockSpec((tm, d), lambda i: (i, 0)),
                  pl.BlockSpec((1, d), lambda i: (0, 0)),
                  pl.BlockSpec((1, d), lambda i: (0, 0))],
        out_specs=pl.BlockSpec((tm, d), lambda i: (i, 0)),
        out_shape=jax.ShapeDtypeStruct((n, d), F32),
        compiler_params=_params("parallel"),
        name="ln0",
    )(x, g.reshape(1, d), b.reshape(1, d))


def _rms_rope(t, gain, head_mean, cos, sin_signed):
    sq = t * t
    hi = sq.astype(BF16)
    lo = (sq - hi.astype(F32)).astype(BF16)
    ms = _dot(hi, head_mean) + _dot(lo, head_mean)
    tn = t * lax.rsqrt(ms + RMS_EPS) * gain
    width = t.shape[-1]
    lane = lax.broadcasted_iota(jnp.int32, tn.shape, 1)
    first_half = (lane % ROPE_AXIS_DIM) < ROPE_HALF
    rot = jnp.where(first_half, pltpu.roll(tn, width - ROPE_HALF, 1), pltpu.roll(tn, ROPE_HALF, 1))
    return tn * cos + rot * sin_signed


def _in_proj_kernel(x_ref, w_ref, bg_ref, qg_ref, kg_ref, cq_ref, sq_ref, ck_ref, sk_ref,
                    cdft_ref, hmq_ref, hmk_ref,
                    zr_ref, zi_ref, q_ref, k_ref, vt_ref, gf_ref, ga_ref):
    xb = x_ref[...].astype(BF16)

    def seg(a, b):
        return _dot(xb, w_ref[:, a:b])

    xf = seg(_C_XF, _C_Q).astype(BF16)
    for g in range(N_FOURIER_GROUPS):
        lo, hi = g * FOURIER_GROUP_DIM, (g + 1) * FOURIER_GROUP_DIM
        z = _dot(xf[:, lo:hi], cdft_ref[...])
        zr_ref[:, lo:hi] = z[:, :FOURIER_GROUP_DIM]
        zi_ref[:, lo:hi] = z[:, FOURIER_GROUP_DIM:]

    q = _rms_rope(seg(_C_Q, _C_K), qg_ref[...], hmq_ref[...], cq_ref[...], sq_ref[...])
    q_ref[...] = (q * (HEAD_DIM ** -0.5 * LOG2_E)).astype(BF16)
    k = _rms_rope(seg(_C_K, _C_V), kg_ref[...], hmk_ref[...], ck_ref[...], sk_ref[...]).astype(BF16)
    vt = seg(_C_V, _C_GF).T.astype(BF16)
    pad_row = lax.broadcasted_iota(jnp.int32, (V_ROWS - HEAD_DIM, vt.shape[1]), 0)
    pad = jnp.where(pad_row == 0, 1.0, 0.0).astype(BF16)
    for j in range(N_KV_HEADS):
        k_ref[j] = k[:, j * HEAD_DIM:(j + 1) * HEAD_DIM]
        vt_ref[j, 0] = jnp.concatenate([vt[j * HEAD_DIM:(j + 1) * HEAD_DIM, :], pad], axis=0)

    gf_ref[...] = jax.nn.sigmoid(seg(_C_GF, _C_GA) + bg_ref[0:1, :])
    ga_ref[...] = jax.nn.sigmoid(seg(_C_GA, _C_END) + bg_ref[1:2, :])


def _in_proj_call(x, w, bg, qg, kg, cq, sq, ck, sk, cdft, hmq, hmk, tm, seq):
    n, d = x.shape
    pos_blocks = seq // tm
    row = lambda i: (i, 0)
    pos = lambda i: (i % pos_blocks, 0)
    const = lambda i: (0, 0)
    full = lambda a: pl.BlockSpec(a.shape, const)
    return pl.pallas_call(
        _in_proj_kernel,
        grid=(n // tm,),
        in_specs=[pl.BlockSpec((tm, d), row), full(w), full(bg), full(qg), full(kg),
                  pl.BlockSpec((tm, D_ATTN), pos), pl.BlockSpec((tm, D_ATTN), pos),
                  pl.BlockSpec((tm, D_KV), pos), pl.BlockSpec((tm, D_KV), pos),
                  full(cdft), full(hmq), full(hmk)],
        out_specs=[pl.BlockSpec((tm, D_FOURIER), row), pl.BlockSpec((tm, D_FOURIER), row),
                   pl.BlockSpec((tm, D_ATTN), row),
                   pl.BlockSpec((N_KV_HEADS, tm, HEAD_DIM), lambda i: (0, i, 0)),
                   pl.BlockSpec((N_KV_HEADS, 1, V_ROWS, tm), lambda i: (0, i, 0, 0)),
                   pl.BlockSpec((tm, d), row), pl.BlockSpec((tm, d), row)],
        out_shape=[jax.ShapeDtypeStruct((n, D_FOURIER), F32), jax.ShapeDtypeStruct((n, D_FOURIER), F32),
                   jax.ShapeDtypeStruct((n, D_ATTN), BF16),
                   jax.ShapeDtypeStruct((N_KV_HEADS, n, HEAD_DIM), BF16),
                   jax.ShapeDtypeStruct((N_KV_HEADS, n // tm, V_ROWS, tm), BF16),
                   jax.ShapeDtypeStruct((n, d), F32), jax.ShapeDtypeStruct((n, d), F32)],
        compiler_params=_params("parallel"),
        name="in_proj",
    )(x, w, bg, qg, kg, cq, sq, ck, sk, cdft, hmq, hmk)


def _dft_stage1_kernel(zr_ref, zi_ref, m1_ref, tr_ref, ti_ref, o_ref):
    n2, jb, c = zr_ref.shape[1:]
    for j in range(jb):
        zz = jnp.concatenate([zr_ref[0, :, j, :], zi_ref[0, :, j, :]], axis=0).astype(BF16)
        b = _dot(m1_ref[...], zz)
        br, bi = b[:n2], b[n2:]
        reps = c // tr_ref.shape[-1]
        tr, ti = jnp.tile(tr_ref[j], (1, reps)), jnp.tile(ti_ref[j], (1, reps))
        o_ref[0, 0, :, j, :] = br * tr - bi * ti
        o_ref[0, 1, :, j, :] = br * ti + bi * tr


def _dft_stage1_call(z_r, z_i, m1, tr, ti, jb):
    b, n2, n1, c = z_r.shape
    blk = pl.BlockSpec((1, n2, jb, c), lambda i, j: (i, 0, j, 0))
    tw = pl.BlockSpec((jb,) + tr.shape[1:], lambda i, j: (j, 0, 0))
    return pl.pallas_call(
        _dft_stage1_kernel,
        grid=(b, n1 // jb),
        in_specs=[blk, blk, pl.BlockSpec(m1.shape, lambda i, j: (0, 0)), tw, tw],
        out_specs=pl.BlockSpec((1, 2, n2, jb, c), lambda i, j: (i, 0, 0, j, 0)),
        out_shape=jax.ShapeDtypeStruct((b, 2, n2, n1, c), F32),
        compiler_params=_params("parallel", "parallel"),
        name="dft_stage1",
    )(z_r, z_i, m1, tr, ti)


def _dft_stage2_kernel(a_ref, m3_ref, o_ref):
    kb = a_ref.shape[2]
    for j in range(kb):
        rhs = jnp.concatenate([a_ref[0, 0, j], a_ref[0, 1, j]], axis=0).astype(BF16)
        o_ref[0, :, j, :] = _dot(m3_ref[...], rhs)


def _dft_stage2_call(a, m3, kb):
    b, _, n2, n1, c = a.shape
    return pl.pallas_call(
        _dft_stage2_kernel,
        grid=(b, n2 // kb),
        in_specs=[pl.BlockSpec((1, 2, kb, n1, c), lambda i, j: (i, 0, j, 0, 0)),
                  pl.BlockSpec(m3.shape, lambda i, j: (0, 0))],
        out_specs=pl.BlockSpec((1, n1, kb, c), lambda i, j: (i, 0, j, 0)),
        out_shape=jax.ShapeDtypeStruct((b, n1, n2, c), F32),
        compiler_params=_params("parallel", "parallel"),
        name="dft_stage2",
    )(a, m3)


def _dft_constants(seq):
    c = FOURIER_GROUP_DIM
    n1, n2 = seq // DFT_N2, DFT_N2
    jc = np.arange(c)
    ang = 2.0 * np.pi * np.outer(jc, jc) / c
    cdft = np.concatenate([np.cos(ang), -np.sin(ang)], axis=1) / math.sqrt(c)
    j2 = np.arange(n2)
    a2 = 2.0 * np.pi * np.outer(j2, j2) / n2
    fr, fi = np.cos(a2) / math.sqrt(n2), -np.sin(a2) / math.sqrt(n2)
    m1 = np.block([[fr, -fi], [fi, fr]])
    j1 = np.arange(n1)
    a1 = 2.0 * np.pi * np.outer(j1, j1) / n1
    m3 = np.concatenate([np.cos(a1), np.sin(a1)], axis=1) / math.sqrt(n1)
    at = 2.0 * np.pi * np.outer(j1, j2) / seq
    lanes = np.ones((1, 1, 128))
    return (jnp.asarray(cdft, BF16), jnp.asarray(m1, BF16), jnp.asarray(m3, BF16),
            jnp.asarray(np.cos(at)[:, :, None] * lanes, F32), jnp.asarray(-np.sin(at)[:, :, None] * lanes, F32))


def _attn_kernel(q_ref, k_ref, vt_ref, o_ref, qs_ref, st_ref, p_ref, acc_ref, *, tq):
    n_chunks, _, tk = vt_ref.shape[1:]
    slots = st_ref.shape[0]
    assert slots == 3 and n_chunks >= 4 and (n_chunks - 4) % slots == 0
    nq = Q_PER_KV * tq
    heads = range(N_KV_HEADS)
    for j in heads:
        for g in range(Q_PER_KV):
            h = j * Q_PER_KV + g
            qs_ref[j, g * tq:(g + 1) * tq, :] = q_ref[:, h * HEAD_DIM:(h + 1) * HEAD_DIM]
    acc_ref[...] = jnp.zeros_like(acc_ref)

    def scores(j, c, slot):
        start = pl.multiple_of(c * tk, tk)
        st_ref[slot, j] = _dot_nt(k_ref[j, pl.ds(start, tk), :], qs_ref[j])

    def softmax(j, slot, m):
        st = st_ref[slot, j]
        m_new = jnp.maximum(m, jnp.max(st, axis=0, keepdims=True))
        p_ref[slot, j] = jnp.exp2(st - m_new).astype(BF16)
        return m_new, jnp.exp2(m - m_new)

    def values(j, c, slot, alpha):
        acc_ref[j] = alpha * acc_ref[j] + _dot(vt_ref[j, c], p_ref[slot, j])

    def step(i, slot, carry, do_values=True, do_scores=True):
        if do_values:
            for j in heads:
                values(j, i - 2, (slot + 1) % slots, carry[j][2])
        new = []
        for j in heads:
            m, alpha = softmax(j, slot, carry[j][0])
            new.append((m, alpha, carry[j][1]))
        if do_scores:
            for j in heads:
                scores(j, i + 2, (slot + 2) % slots)
        return tuple(new)

    for j in heads:
        scores(j, 0, 0)
        scores(j, 1, 1)
    zero = jnp.zeros((1, nq), F32)
    carry = tuple((jnp.full((1, nq), -jnp.inf, F32), zero, zero) for _ in heads)
    carry = step(0, 0, carry, do_values=False)
    carry = step(1, 1, carry, do_values=False)

    def body(g, carry):
        i = 2 + slots * g
        for k in range(slots):
            carry = step(i + k, (2 + k) % slots, carry)
        return carry

    carry = lax.fori_loop(0, (n_chunks - 4) // slots, body, carry)
    for i in (n_chunks - 2, n_chunks - 1):
        carry = step(i, i % slots, carry, do_scores=False)
    pieces = []
    for j in heads:
        values(j, n_chunks - 2, (n_chunks - 2) % slots, carry[j][2])
        values(j, n_chunks - 1, (n_chunks - 1) % slots, carry[j][1])
        ot = acc_ref[j, :HEAD_DIM, :] / acc_ref[j, HEAD_DIM:HEAD_DIM + 1, :]
        o = jnp.concatenate([ot, jnp.zeros_like(ot)], axis=0).T
        pieces += [o[g * tq:(g + 1) * tq, :HEAD_DIM] for g in range(Q_PER_KV)]
    o_ref[...] = jnp.concatenate(pieces, axis=1)


def _attn_call(q, k, vt, batch, seq, tq):
    n = q.shape[0]
    qblocks = seq // tq
    tk = vt.shape[-1]
    chunks = seq // tk
    nq = Q_PER_KV * tq
    return pl.pallas_call(
        functools.partial(_attn_kernel, tq=tq),
        grid=(batch, qblocks),
        in_specs=[pl.BlockSpec((tq, D_ATTN), lambda b, i: (b * qblocks + i, 0)),
                  pl.BlockSpec((N_KV_HEADS, seq, HEAD_DIM), lambda b, i: (0, b, 0)),
                  pl.BlockSpec((N_KV_HEADS, chunks, V_ROWS, tk), lambda b, i: (0, b, 0, 0))],
        out_specs=pl.BlockSpec((tq, D_ATTN), lambda b, i: (b * qblocks + i, 0)),
        out_shape=jax.ShapeDtypeStruct((n, D_ATTN), F32),
        scratch_shapes=[pltpu.VMEM((N_KV_HEADS, nq, HEAD_DIM), BF16),
                        pltpu.VMEM((3, N_KV_HEADS, tk, nq), F32),
                        pltpu.VMEM((3, N_KV_HEADS, tk, nq), BF16),
                        pltpu.VMEM((N_KV_HEADS, V_ROWS, nq), F32)],
        compiler_params=_params("parallel", "parallel"),
        name="attention",
    )(q, k, vt)


def _mix_kernel(yf_ref, ya_ref, gf_ref, ga_ref, x_ref, wf_ref, wa_ref, wo_ref, g_ref, b_ref, o_ref, *, alpha):
    mixed = (gf_ref[...] * _dot(yf_ref[...].astype(BF16), wf_ref[...])
             + ga_ref[...] * _dot(ya_ref[...].astype(BF16), wa_ref[...]))
    z = alpha * x_ref[...] + _dot(mixed.astype(BF16), wo_ref[...])
    o_ref[...] = _layer_norm(z, g_ref[...], b_ref[...])


def _mix_call(yf, ya, gf, ga, x, wf, wa, wo, g, b, tm, alpha):
    n, d = x.shape
    row = lambda i: (i, 0)
    const = lambda i: (0, 0)
    return pl.pallas_call(
        functools.partial(_mix_kernel, alpha=alpha),
        grid=(n // tm,),
        in_specs=[pl.BlockSpec((tm, D_FOURIER), row), pl.BlockSpec((tm, D_ATTN), row),
                  pl.BlockSpec((tm, d), row), pl.BlockSpec((tm, d), row), pl.BlockSpec((tm, d), row),
                  pl.BlockSpec(wf.shape, const), pl.BlockSpec(wa.shape, const), pl.BlockSpec(wo.shape, const),
                  pl.BlockSpec((1, d), const), pl.BlockSpec((1, d), const)],
        out_specs=pl.BlockSpec((tm, d), row),
        out_shape=jax.ShapeDtypeStruct((n, d), F32),
        compiler_params=_params("parallel"),
        name="mix",
    )(yf, ya, gf, ga, x, wf, wa, wo, g.reshape(1, d), b.reshape(1, d))


_NEG_INF = float("-inf")


def _extract_top(s, order, count, exact):
    rank = jnp.full(s.shape, float(count), F32)
    big = float(2 ** 20)
    vals = []
    for a in range(count):
        m = jnp.max(s, axis=0, keepdims=True)
        if exact:
            first = jnp.min(jnp.where(s == m, order, big), axis=0, keepdims=True)
            sel = order == first
        else:
            sel = s == m
        rank = jnp.where(sel, float(a), rank)
        s = jnp.where(sel, _NEG_INF, s)
        vals.append(m)
    extracted = jnp.sum(jnp.where(rank < float(count), 1.0, 0.0), axis=0, keepdims=True)
    return vals, rank, jnp.where(extracted == float(count), 0.0, 1.0)


def _candidate_rows(v1, v2, combine):
    v2_all = jnp.concatenate(v2, axis=0)
    v2_lo = v2_all[:8]
    pieces = [combine(v1[0], v2_all)]
    pieces += [combine(v1[a], v2_lo) for a in range(1, 8)]
    pieces.append(combine(jnp.concatenate(v1[8:], axis=0), v2[0]))
    return jnp.concatenate(pieces, axis=0)


def _candidate_positions(tokens):
    pos = ([b for b in range(16)] + [a * 16 + b for a in range(1, 8) for b in range(8)]
           + [a * 16 for a in range(8, 16)])
    col = jnp.asarray(np.asarray(pos, np.float32).reshape(-1, 1))
    return jnp.broadcast_to(col, (len(pos), tokens))


def _peer_route_kernel(x_ref, wq_ref, keys_ref, pos_ref, g1_ref, g2_ref):
    ambiguous = _route_tile(x_ref, wq_ref, keys_ref, pos_ref, g1_ref, g2_ref, exact=False)

    @pl.when(jnp.max(ambiguous) > 0.0)
    def _():
        _route_tile(x_ref, wq_ref, keys_ref, pos_ref, g1_ref, g2_ref, exact=True)


def _route_tile(x_ref, wq_ref, keys_ref, pos_ref, g1_ref, g2_ref, exact):
    tokens = x_ref.shape[0]
    xb = x_ref[...].astype(BF16)
    key_order = lax.broadcasted_iota(jnp.int32, (PEER_N_KEYS, tokens), 0).astype(F32)
    cand_pos = pos_ref[...]
    ambiguous = jnp.zeros((1, tokens), F32)
    for h in range(PEER_HEADS):
        col = 2 * h * PEER_HALF
        qp = _dot(xb, wq_ref[:, col:col + 2 * PEER_HALF]).astype(BF16)
        scores, vals, ranks = [], [], []
        for c in range(2):
            s = _dot_nt(keys_ref[h, c], qp[:, c * PEER_HALF:(c + 1) * PEER_HALF])
            v, r, amb = _extract_top(s, key_order, PEER_TOPK, exact)
            ambiguous = jnp.maximum(ambiguous, amb)
            scores.append(s)
            vals.append(v)
            ranks.append(r)
        v1, v2 = vals
        cand = _candidate_rows(v1, v2, lambda a, b: a + b)
        _, crank, amb = _extract_top(cand, cand_pos, PEER_TOPK, exact)
        ambiguous = jnp.maximum(ambiguous, amb)
        chosen = crank < float(PEER_TOPK)
        e1 = [jnp.exp(v - v1[0]) for v in v1]
        e2 = [jnp.exp(v - v2[0]) for v in v2]
        ecand = _candidate_rows(e1, e2, lambda a, b: a * b)
        z = jnp.sum(jnp.where(chosen, ecand, 0.0), axis=0, keepdims=True)
        cnt = jnp.where(chosen, 1.0, 0.0)
        limit = [jnp.sum(cnt[0:16], axis=0, keepdims=True)]
        limit += [jnp.sum(cnt[8 + 8 * a:16 + 8 * a], axis=0, keepdims=True) for a in range(1, 8)]
        limit += [cnt[72 + a:73 + a] for a in range(8)]
        lim_dense = jnp.zeros((PEER_N_KEYS, tokens), F32)
        for a in range(PEER_TOPK):
            lim_dense = jnp.where(ranks[0] == float(a), limit[a], lim_dense)
        g1_ref[0, h] = jnp.exp(scores[0] - v1[0]) / z
        g1_ref[1, h] = lim_dense
        g2_ref[0, h] = pltpu.bitcast(jnp.exp(scores[1] - v2[0]).astype(BF16), jnp.uint32)
        g2_ref[1, h] = pltpu.bitcast(ranks[1].astype(BF16), jnp.uint32)
    return ambiguous


def _peer_route_call(x, wq, keys, tm):
    n, d = x.shape
    pos = _candidate_positions(tm)
    return pl.pallas_call(
        _peer_route_kernel,
        grid=(n // tm,),
        in_specs=[pl.BlockSpec((tm, d), lambda i: (i, 0)),
                  pl.BlockSpec(wq.shape, lambda i: (0, 0)),
                  pl.BlockSpec(keys.shape, lambda i: (0, 0, 0, 0)),
                  pl.BlockSpec(pos.shape, lambda i: (0, 0))],
        out_specs=[pl.BlockSpec((2, PEER_HEADS, PEER_N_KEYS, tm), lambda i: (0, 0, 0, i)),
                   pl.BlockSpec((2, PEER_HEADS, PEER_N_KEYS // 2, tm), lambda i: (0, 0, 0, i))],
        out_shape=[jax.ShapeDtypeStruct((2, PEER_HEADS, PEER_N_KEYS, n), F32),
                   jax.ShapeDtypeStruct((2, PEER_HEADS, PEER_N_KEYS // 2, n), jnp.uint32)],
        compiler_params=_params("parallel"),
        name="peer_route",
    )(x, wq, keys, pos)


def _peer_expert_kernel(x_ref, g1_ref, g2_ref, u_ref, vt_ref, g_ref, b_ref, o_ref,
                        xb_ref, act0_ref, act1_ref, p0_ref, p1_ref, acc_ref, rowb_ref, *, alpha, rows_per_step):
    s = pl.program_id(1)
    last = pl.num_programs(1) - 1
    n_chunks = last - 1

    @pl.when(s == 0)
    def _():
        xb_ref[...] = x_ref[...].astype(BF16)
        acc_ref[...] = jnp.zeros_like(acc_ref)
        act1_ref[...] = jnp.zeros_like(act1_ref)
        p1_ref[...] = jnp.zeros_like(p1_ref)

    def stages(act_new, act_old, p_new, p_old):
        chunk = jnp.clip(s - 1, 0, n_chunks - 1)
        tokens = xb_ref.shape[0]
        te = act_new.shape[0]
        d = acc_ref.shape[0]
        for r in range(rows_per_step):
            i1 = chunk * rows_per_step + r
            for h in range(PEER_HEADS):
                for kind in range(2):
                    row = g1_ref[kind, h, pl.ds(i1, 1), :].astype(BF16)
                    rowb_ref[r, h, kind] = jnp.broadcast_to(row, (BF16_ROWS, tokens))

        def weigh(r0, lt):
            groups = PEER_N_KEYS // BF16_ROWS
            ls = slice(lt * LANES, (lt + 1) * LANES)
            gates = [[None] * groups for _ in range(GATE_ROWS)]
            for h in range(PEER_HEADS):
                rows = [(rowb_ref[r0 + r, h, 0, :, ls], rowb_ref[r0 + r, h, 1, :, ls]) for r in range(GATE_ROWS)]
                for k in range(groups):
                    ks = slice(k * BF16_ROWS // 2, (k + 1) * BF16_ROWS // 2)
                    e2 = pltpu.bitcast(g2_ref[0, h, ks, ls], BF16)
                    rank2 = pltpu.bitcast(g2_ref[1, h, ks, ls], BF16)
                    for r, (e1n, lim) in enumerate(rows):
                        term = jnp.where(rank2 < lim, e2 * e1n, jnp.zeros((), BF16))
                        gates[r][k] = term if gates[r][k] is None else gates[r][k] + term
            for r in range(GATE_ROWS):
                for k in range(groups):
                    lo = (r0 + r) * PEER_N_KEYS + k * BF16_ROWS
                    a = act_old[lo:lo + BF16_ROWS, ls]
                    gelu = 0.5 * a * (1.0 + lax.erf(a * math.sqrt(0.5)))
                    p_new[lo:lo + BF16_ROWS, ls] = gates[r][k] * gelu.astype(BF16)

        def activations(piece):
            lo = piece * MXU_PIECE
            act_new[lo:lo + MXU_PIECE, :] = _dot_nt(u_ref[lo:lo + MXU_PIECE, :], xb_ref[...])

        def outputs(piece):
            lo = piece * MXU_PIECE
            acc_ref[lo:lo + MXU_PIECE, :] += _dot(vt_ref[lo:lo + MXU_PIECE, :], p_old[...])

        matmuls = [functools.partial(activations, i) for i in range(te // MXU_PIECE)]
        matmuls += [functools.partial(outputs, i) for i in range(d // MXU_PIECE)]
        units = [(r0, lt) for r0 in range(0, rows_per_step, GATE_ROWS) for lt in range(tokens // LANES)]
        for i in range(max(len(matmuls), len(units))):
            if i < len(units):
                weigh(*units[i])
            if i < len(matmuls):
                matmuls[i]()

    @pl.when(s % 2 == 0)
    def _():
        stages(act0_ref, act1_ref, p0_ref, p1_ref)

    @pl.when(s % 2 == 1)
    def _():
        stages(act1_ref, act0_ref, p1_ref, p0_ref)

    @pl.when(s == last)
    def _():
        z = alpha * x_ref[...] + acc_ref[...].T
        o_ref[...] = _layer_norm(z, g_ref[...], b_ref[...])


def _peer_expert_call(x, g1, g2, u, vt, g, b, tm, rows_per_step, alpha):
    n, d = x.shape
    te = rows_per_step * PEER_N_KEYS
    chunks = u.shape[0] // te
    return pl.pallas_call(
        functools.partial(_peer_expert_kernel, alpha=alpha, rows_per_step=rows_per_step),
        grid=(n // tm, chunks + 2),
        in_specs=[pl.BlockSpec((tm, d), lambda i, s: (i, 0)),
                  pl.BlockSpec((2, PEER_HEADS, PEER_N_KEYS, tm), lambda i, s: (0, 0, 0, i)),
                  pl.BlockSpec((2, PEER_HEADS, PEER_N_KEYS // 2, tm), lambda i, s: (0, 0, 0, i)),
                  pl.BlockSpec((te, d), lambda i, s: (jnp.minimum(s, chunks - 1), 0)),
                  pl.BlockSpec((d, te), lambda i, s: (0, jnp.clip(s - 2, 0, chunks - 1))),
                  pl.BlockSpec((1, d), lambda i, s: (0, 0)),
                  pl.BlockSpec((1, d), lambda i, s: (0, 0))],
        out_specs=pl.BlockSpec((tm, d), lambda i, s: (i, 0)),
        out_shape=jax.ShapeDtypeStruct((n, d), F32),
        scratch_shapes=[pltpu.VMEM((tm, d), BF16),
                        pltpu.VMEM((te, tm), F32), pltpu.VMEM((te, tm), F32),
                        pltpu.VMEM((te, tm), BF16), pltpu.VMEM((te, tm), BF16),
                        pltpu.VMEM((d, tm), F32),
                        pltpu.VMEM((rows_per_step, PEER_HEADS, 2, BF16_ROWS, tm), BF16)],
        compiler_params=_params("parallel", "arbitrary"),
        name="peer_experts",
    )(x, g1, g2, u, vt, g.reshape(1, d), b.reshape(1, d))


def _rope_tables(seq):
    rows = seq // GRID_W
    row_id = jnp.repeat(jnp.arange(rows, dtype=F32), GRID_W)
    col_id = jnp.tile(jnp.arange(GRID_W, dtype=F32), rows)
    inv_freq = 1.0 / (ROPE_THETA ** (jnp.arange(0, ROPE_AXIS_DIM, 2, dtype=F32) / ROPE_AXIS_DIM))
    ang_r = row_id[:, None] * inv_freq[None, :]
    ang_c = col_id[:, None] * inv_freq[None, :]
    ang = jnp.concatenate([ang_r, ang_r, ang_c, ang_c], axis=-1)
    sign = jnp.where((jnp.arange(HEAD_DIM) % ROPE_AXIS_DIM) < ROPE_HALF, -1.0, 1.0).astype(F32)
    return jnp.cos(ang), jnp.sin(ang) * sign[None, :]


def _head_mean_matrix(heads):
    m = np.kron(np.eye(heads), np.full((HEAD_DIM, HEAD_DIM), 1.0 / HEAD_DIM))
    return jnp.asarray(m, BF16)


TM_LN = 512
TM_IN = 512
DFT_BLOCK = 8
ATTN_TQ = 128
TM_MIX = 256
TM_ROUTE = 256
TM_EXPERT = 512
EXPERT_ROWS_PER_STEP = 8


def kernel(x, ln0_g, ln0_b, w_in, b_gate, q_norm_g, k_norm_g, w_branch_fourier, w_branch_attn, w_out,
           ln1_g, ln1_b, peer_w_query, peer_sub_keys, peer_u, peer_v, ln2_g, ln2_b):
    batch, seq, d = x.shape
    n = batch * seq
    depth = w_in.shape[0]
    alpha = (2.0 * depth) ** 0.25
    n1 = seq // DFT_N2

    cos, sin_signed = _rope_tables(seq)
    cq, sq = jnp.tile(cos, (1, N_Q_HEADS)), jnp.tile(sin_signed, (1, N_Q_HEADS))
    ck, sk = jnp.tile(cos, (1, N_KV_HEADS)), jnp.tile(sin_signed, (1, N_KV_HEADS))
    hmq, hmk = _head_mean_matrix(N_Q_HEADS), _head_mean_matrix(N_KV_HEADS)
    cdft, m1, m3, tw_r, tw_i = _dft_constants(seq)

    xs = _ln_call(x.reshape(n, d), ln0_g, ln0_b, TM_LN)
    for l in range(depth):
        qg = jnp.tile(q_norm_g[l], N_Q_HEADS).reshape(1, D_ATTN)
        kg = jnp.tile(k_norm_g[l], N_KV_HEADS).reshape(1, D_KV)
        zr, zi, q, k, vt, gf, ga = _in_proj_call(
            xs, w_in[l].astype(BF16), b_gate[l], qg, kg, cq, sq, ck, sk, cdft, hmq, hmk, TM_IN, seq)
        a = _dft_stage1_call(zr.reshape(batch, DFT_N2, n1, D_FOURIER), zi.reshape(batch, DFT_N2, n1, D_FOURIER),
                             m1, tw_r, tw_i, DFT_BLOCK)
        yf = _dft_stage2_call(a, m3, DFT_BLOCK)
        ya = _attn_call(q, k, vt, batch, seq, ATTN_TQ)
        xs = _mix_call(yf.reshape(n, D_FOURIER), ya, gf, ga, xs,
                       w_branch_fourier[l].astype(BF16), w_branch_attn[l].astype(BF16), w_out[l].astype(BF16),
                       ln1_g[l], ln1_b[l], TM_MIX, alpha)
        g1, g2 = _peer_route_call(xs, peer_w_query[l].astype(BF16), peer_sub_keys[l].astype(BF16), TM_ROUTE)
        xs = _peer_expert_call(xs, g1, g2, peer_u[l].astype(BF16), peer_v[l].astype(BF16).T,
                               ln2_g[l], ln2_b[l], TM_EXPERT, EXPERT_ROWS_PER_STEP, alpha)
    return xs.reshape(batch, seq, d)
```

```python
import functools
import math

import numpy as np
import jax
import jax.numpy as jnp
from jax import lax
from jax.experimental import pallas as pl
from jax.experimental.pallas import tpu as pltpu

F32 = jnp.float32
BF16 = jnp.bfloat16

D_MODEL = 1024
GRID_W = 64
N_FOURIER_GROUPS = 4
FOURIER_GROUP_DIM = 128
D_FOURIER = N_FOURIER_GROUPS * FOURIER_GROUP_DIM
N_Q_HEADS = 8
N_KV_HEADS = 2
Q_PER_KV = N_Q_HEADS // N_KV_HEADS
HEAD_DIM = 64
D_ATTN = N_Q_HEADS * HEAD_DIM
D_KV = N_KV_HEADS * HEAD_DIM
ROPE_AXIS_DIM = HEAD_DIM // 2
ROPE_HALF = ROPE_AXIS_DIM // 2
ROPE_THETA = 10000.0
LOG2_E = math.log2(math.e)
V_ROWS = HEAD_DIM + 16
PEER_HEADS = 8
PEER_N_KEYS = 128
PEER_TOPK = 16
PEER_HALF = 128
LN_EPS = 1e-5
RMS_EPS = 1e-6

_C_XF = 0
_C_Q = D_FOURIER
_C_K = _C_Q + D_ATTN
_C_V = _C_K + D_KV
_C_GF = _C_V + D_KV
_C_GA = _C_GF + D_MODEL
_C_END = _C_GA + D_MODEL

DFT_N2 = 128

VMEM_LIMIT_BYTES = 56 * 1024 * 1024


def _params(*sem):
    return pltpu.CompilerParams(dimension_semantics=sem, vmem_limit_bytes=VMEM_LIMIT_BYTES)


def _dot(a, b):
    return jnp.dot(a, b, preferred_element_type=F32)


def _dot_nt(a, b):
    return lax.dot_general(a, b, (((1,), (1,)), ((), ())), preferred_element_type=F32)


def _layer_norm(z, g, b):
    mu = jnp.mean(z, axis=-1, keepdims=True)
    zc = z - mu
    var = jnp.mean(zc * zc, axis=-1, keepdims=True)
    return zc * lax.rsqrt(var + LN_EPS) * g + b


def _ln_kernel(x_ref, g_ref, b_ref, o_ref):
    o_ref[...] = _layer_norm(x_ref[...], g_ref[...], b_ref[...])


def _ln_call(x, g, b, tm):
    n, d = x.shape
    return pl.pallas_call(
        _ln_kernel,
        grid=(n // tm,),
        in_specs=[pl.BlockSpec((tm, d), lambda i: (i, 0)),
                  pl.BlockSpec((1, d), lambda i: (0, 0)),
                  pl.BlockSpec((1, d), lambda i: (0, 0))],
        out_specs=pl.BlockSpec((tm, d), lambda i: (i, 0)),
        out_shape=jax.ShapeDtypeStruct((n, d), F32),
        compiler_params=_params("parallel"),
        name="ln0",
    )(x, g.reshape(1, d), b.reshape(1, d))


def _rms_rope(t, gain, head_mean, cos, sin_signed):
    sq = t * t
    hi = sq.astype(BF16)
    lo = (sq - hi.astype(F32)).astype(BF16)
    ms = _dot(hi, head_mean) + _dot(lo, head_mean)
    tn = t * lax.rsqrt(ms + RMS_EPS) * gain
    width = t.shape[-1]
    lane = lax.broadcasted_iota(jnp.int32, tn.shape, 1)
    first_half = (lane % ROPE_AXIS_DIM) < ROPE_HALF
    rot = jnp.where(first_half, pltpu.roll(tn, width - ROPE_HALF, 1), pltpu.roll(tn, ROPE_HALF, 1))
    return tn * cos + rot * sin_signed


def _in_proj_kernel(x_ref, w_ref, bg_ref, qg_ref, kg_ref, cq_ref, sq_ref, ck_ref, sk_ref,
                    cdft_ref, hmq_ref, hmk_ref,
                    zr_ref, zi_ref, q_ref, k_ref, vt_ref, gf_ref, ga_ref):
    xb = x_ref[...].astype(BF16)

    def seg(a, b):
        return _dot(xb, w_ref[:, a:b])

    xf = seg(_C_XF, _C_Q).astype(BF16)
    for g in range(N_FOURIER_GROUPS):
        lo, hi = g * FOURIER_GROUP_DIM, (g + 1) * FOURIER_GROUP_DIM
        z = _dot(xf[:, lo:hi], cdft_ref[...])
        zr_ref[:, lo:hi] = z[:, :FOURIER_GROUP_DIM]
        zi_ref[:, lo:hi] = z[:, FOURIER_GROUP_DIM:]

    q = _rms_rope(seg(_C_Q, _C_K), qg_ref[...], hmq_ref[...], cq_ref[...], sq_ref[...])
    q_ref[...] = (q * (HEAD_DIM ** -0.5 * LOG2_E)).astype(BF16)
    k = _rms_rope(seg(_C_K, _C_V), kg_ref[...], hmk_ref[...], ck_ref[...], sk_ref[...]).astype(BF16)
    vt = seg(_C_V, _C_GF).T.astype(BF16)
    pad_row = lax.broadcasted_iota(jnp.int32, (V_ROWS - HEAD_DIM, vt.shape[1]), 0)
    pad = jnp.where(pad_row == 0, 1.0, 0.0).astype(BF16)
    for j in range(N_KV_HEADS):
        k_ref[j] = k[:, j * HEAD_DIM:(j + 1) * HEAD_DIM]
        vt_ref[j, 0] = jnp.concatenate([vt[j * HEAD_DIM:(j + 1) * HEAD_DIM, :], pad], axis=0)

    gf_ref[...] = jax.nn.sigmoid(seg(_C_GF, _C_GA) + bg_ref[0:1, :])
    ga_ref[...] = jax.nn.sigmoid(seg(_C_GA, _C_END) + bg_ref[1:2, :])


def _in_proj_call(x, w, bg, qg, kg, cq, sq, ck, sk, cdft, hmq, hmk, tm, seq):
    n, d = x.shape
    pos_blocks = seq // tm
    row = lambda i: (i, 0)
    pos = lambda i: (i % pos_blocks, 0)
    const = lambda i: (0, 0)
    full = lambda a: pl.BlockSpec(a.shape, const)
    return pl.pallas_call(
        _in_proj_kernel,
        grid=(n // tm,),
        in_specs=[pl.BlockSpec((tm, d), row), full(w), full(bg), full(qg), full(kg),
                  pl.BlockSpec((tm, D_ATTN), pos), pl.BlockSpec((tm, D_ATTN), pos),
                  pl.BlockSpec((tm, D_KV), pos), pl.BlockSpec((tm, D_KV), pos),
                  full(cdft), full(hmq), full(hmk)],
        out_specs=[pl.BlockSpec((tm, D_FOURIER), row), pl.BlockSpec((tm, D_FOURIER), row),
                   pl.BlockSpec((tm, D_ATTN), row),
                   pl.BlockSpec((N_KV_HEADS, tm, HEAD_DIM), lambda i: (0, i, 0)),
                   pl.BlockSpec((N_KV_HEADS, 1, V_ROWS, tm), lambda i: (0, i, 0, 0)),
                   pl.BlockSpec((tm, d), row), pl.BlockSpec((tm, d), row)],
        out_shape=[jax.ShapeDtypeStruct((n, D_FOURIER), F32), jax.ShapeDtypeStruct((n, D_FOURIER), F32),
                   jax.ShapeDtypeStruct((n, D_ATTN), BF16),
                   jax.ShapeDtypeStruct((N_KV_HEADS, n, HEAD_DIM), BF16),
                   jax.ShapeDtypeStruct((N_KV_HEADS, n // tm, V_ROWS, tm), BF16),
                   jax.ShapeDtypeStruct((n, d), F32), jax.ShapeDtypeStruct((n, d), F32)],
        compiler_params=_params("parallel"),
        name="in_proj",
    )(x, w, bg, qg, kg, cq, sq, ck, sk, cdft, hmq, hmk)


def _dft_stage1_kernel(zr_ref, zi_ref, m1_ref, tr_ref, ti_ref, o_ref):
    n2, jb, c = zr_ref.shape[1:]
    for j in range(jb):
        zz = jnp.concatenate([zr_ref[0, :, j, :], zi_ref[0, :, j, :]], axis=0).astype(BF16)
        b = _dot(m1_ref[...], zz)
        br, bi = b[:n2], b[n2:]
        reps = c // tr_ref.shape[-1]
        tr, ti = jnp.tile(tr_ref[j], (1, reps)), jnp.tile(ti_ref[j], (1, reps))
        o_ref[0, 0, :, j, :] = br * tr - bi * ti
        o_ref[0, 1, :, j, :] = br * ti + bi * tr


def _dft_stage1_call(z_r, z_i, m1, tr, ti, jb):
    b, n2, n1, c = z_r.shape
    blk = pl.BlockSpec((1, n2, jb, c), lambda i, j: (i, 0, j, 0))
    tw = pl.BlockSpec((jb,) + tr.shape[1:], lambda i, j: (j, 0, 0))
    return pl.pallas_call(
        _dft_stage1_kernel,
        grid=(b, n1 // jb),
        in_specs=[blk, blk, pl.BlockSpec(m1.shape, lambda i, j: (0, 0)), tw, tw],
        out_specs=pl.BlockSpec((1, 2, n2, jb, c), lambda i, j: (i, 0, 0, j, 0)),
        out_shape=jax.ShapeDtypeStruct((b, 2, n2, n1, c), F32),
        compiler_params=_params("parallel", "parallel"),
        name="dft_stage1",
    )(z_r, z_i, m1, tr, ti)


def _dft_stage2_kernel(a_ref, m3_ref, o_ref):
    kb = a_ref.shape[2]
    for j in range(kb):
        rhs = jnp.concatenate([a_ref[0, 0, j], a_ref[0, 1, j]], axis=0).astype(BF16)
        o_ref[0, :, j, :] = _dot(m3_ref[...], rhs)


def _dft_stage2_call(a, m3, kb):
    b, _, n2, n1, c = a.shape
    return pl.pallas_call(
        _dft_stage2_kernel,
        grid=(b, n2 // kb),
        in_specs=[pl.BlockSpec((1, 2, kb, n1, c), lambda i, j: (i, 0, j, 0, 0)),
                  pl.BlockSpec(m3.shape, lambda i, j: (0, 0))],
        out_specs=pl.BlockSpec((1, n1, kb, c), lambda i, j: (i, 0, j, 0)),
        out_shape=jax.ShapeDtypeStruct((b, n1, n2, c), F32),
        compiler_params=_params("parallel", "parallel"),
        name="dft_stage2",
    )(a, m3)


def _dft_constants(seq):
    c = FOURIER_GROUP_DIM
    n1, n2 = seq // DFT_N2, DFT_N2
    jc = np.arange(c)
    ang = 2.0 * np.pi * np.outer(jc, jc) / c
    cdft = np.concatenate([np.cos(ang), -np.sin(ang)], axis=1) / math.sqrt(c)
    j2 = np.arange(n2)
    a2 = 2.0 * np.pi * np.outer(j2, j2) / n2
    fr, fi = np.cos(a2) / math.sqrt(n2), -np.sin(a2) / math.sqrt(n2)
    m1 = np.block([[fr, -fi], [fi, fr]])
    j1 = np.arange(n1)
    a1 = 2.0 * np.pi * np.outer(j1, j1) / n1
    m3 = np.concatenate([np.cos(a1), np.sin(a1)], axis=1) / math.sqrt(n1)
    at = 2.0 * np.pi * np.outer(j1, j2) / seq
    lanes = np.ones((1, 1, 128))
    return (jnp.asarray(cdft, BF16), jnp.asarray(m1, BF16), jnp.asarray(m3, BF16),
            jnp.asarray(np.cos(at)[:, :, None] * lanes, F32), jnp.asarray(-np.sin(at)[:, :, None] * lanes, F32))


def _attn_kernel(q_ref, k_ref, vt_ref, o_ref, qs_ref, st_ref, p_ref, acc_ref, *, tq):
    n_chunks, _, tk = vt_ref.shape[1:]
    slots = st_ref.shape[0]
    assert slots == 3 and n_chunks >= 4 and (n_chunks - 4) % slots == 0
    nq = Q_PER_KV * tq
    heads = range(N_KV_HEADS)
    for j in heads:
        for g in range(Q_PER_KV):
            h = j * Q_PER_KV + g
            qs_ref[j, g * tq:(g + 1) * tq, :] = q_ref[:, h * HEAD_DIM:(h + 1) * HEAD_DIM]
    acc_ref[...] = jnp.zeros_like(acc_ref)

    def scores(j, c, slot):
        start = pl.multiple_of(c * tk, tk)
        st_ref[slot, j] = _dot_nt(k_ref[j, pl.ds(start, tk), :], qs_ref[j])

    def softmax(j, slot, m):
        st = st_ref[slot, j]
        m_new = jnp.maximum(m, jnp.max(st, axis=0, keepdims=True))
        p_ref[slot, j] = jnp.exp2(st - m_new).astype(BF16)
        return m_new, jnp.exp2(m - m_new)

    def values(j, c, slot, alpha):
        acc_ref[j] = alpha * acc_ref[j] + _dot(vt_ref[j, c], p_ref[slot, j])

    def step(i, slot, carry, do_values=True, do_scores=True):
        if do_values:
            for j in heads:
                values(j, i - 2, (slot + 1) % slots, carry[j][2])
        new = []
        for j in heads:
            m, alpha = softmax(j, slot, carry[j][0])
            new.append((m, alpha, carry[j][1]))
        if do_scores:
            for j in heads:
                scores(j, i + 2, (slot + 2) % slots)
        return tuple(new)

    for j in heads:
        scores(j, 0, 0)
        scores(j, 1, 1)
    zero = jnp.zeros((1, nq), F32)
    carry = tuple((jnp.full((1, nq), -jnp.inf, F32), zero, zero) for _ in heads)
    carry = step(0, 0, carry, do_values=False)
    carry = step(1, 1, carry, do_values=False)

    def body(g, carry):
        i = 2 + slots * g
        for k in range(slots):
            carry = step(i + k, (2 + k) % slots, carry)
        return carry

    carry = lax.fori_loop(0, (n_chunks - 4) // slots, body, carry)
    for i in (n_chunks - 2, n_chunks - 1):
        carry = step(i, i % slots, carry, do_scores=False)
    pieces = []
    for j in heads:
        values(j, n_chunks - 2, (n_chunks - 2) % slots, carry[j][2])
        values(j, n_chunks - 1, (n_chunks - 1) % slots, carry[j][1])
        ot = acc_ref[j, :HEAD_DIM, :] / acc_ref[j, HEAD_DIM:HEAD_DIM + 1, :]
        o = jnp.concatenate([ot, jnp.zeros_like(ot)], axis=0).T
        pieces += [o[g * tq:(g + 1) * tq, :HEAD_DIM] for g in range(Q_PER_KV)]
    o_ref[...] = jnp.concatenate(pieces, axis=1)


def _attn_call(q, k, vt, batch, seq, tq):
    n = q.shape[0]
    qblocks = seq // tq
    tk = vt.shape[-1]
    chunks = seq // tk
    nq = Q_PER_KV * tq
    return pl.pallas_call(
        functools.partial(_attn_kernel, tq=tq),
        grid=(batch, qblocks),
        in_specs=[pl.BlockSpec((tq, D_ATTN), lambda b, i: (b * qblocks + i, 0)),
                  pl.BlockSpec((N_KV_HEADS, seq, HEAD_DIM), lambda b, i: (0, b, 0)),
                  pl.BlockSpec((N_KV_HEADS, chunks, V_ROWS, tk), lambda b, i: (0, b, 0, 0))],
        out_specs=pl.BlockSpec((tq, D_ATTN), lambda b, i: (b * qblocks + i, 0)),
        out_shape=jax.ShapeDtypeStruct((n, D_ATTN), F32),
        scratch_shapes=[pltpu.VMEM((N_KV_HEADS, nq, HEAD_DIM), BF16),
                        pltpu.VMEM((3, N_KV_HEADS, tk, nq), F32),
                        pltpu.VMEM((3, N_KV_HEADS, tk, nq), BF16),
                        pltpu.VMEM((N_KV_HEADS, V_ROWS, nq), F32)],
        compiler_params=_params("parallel", "parallel"),
        name="attention",
    )(q, k, vt)


def _mix_kernel(yf_ref, ya_ref, gf_ref, ga_ref, x_ref, wf_ref, wa_ref, wo_ref, g_ref, b_ref, o_ref, *, alpha):
    mixed = (gf_ref[...] * _dot(yf_ref[...].astype(BF16), wf_ref[...])
             + ga_ref[...] * _dot(ya_ref[...].astype(BF16), wa_ref[...]))
    z = alpha * x_ref[...] + _dot(mixed.astype(BF16), wo_ref[...])
    o_ref[...] = _layer_norm(z, g_ref[...], b_ref[...])


def _mix_call(yf, ya, gf, ga, x, wf, wa, wo, g, b, tm, alpha):
    n, d = x.shape
    row = lambda i: (i, 0)
    const = lambda i: (0, 0)
    return pl.pallas_call(
        functools.partial(_mix_kernel, alpha=alpha),
        grid=(n // tm,),
        in_specs=[pl.BlockSpec((tm, D_FOURIER), row), pl.BlockSpec((tm, D_ATTN), row),
                  pl.BlockSpec((tm, d), row), pl.BlockSpec((tm, d), row), pl.BlockSpec((tm, d), row),
                  pl.BlockSpec(wf.shape, const), pl.BlockSpec(wa.shape, const), pl.BlockSpec(wo.shape, const),
                  pl.BlockSpec((1, d), const), pl.BlockSpec((1, d), const)],
        out_specs=pl.BlockSpec((tm, d), row),
        out_shape=jax.ShapeDtypeStruct((n, d), F32),
        compiler_params=_params("parallel"),
        name="mix",
    )(yf, ya, gf, ga, x, wf, wa, wo, g.reshape(1, d), b.reshape(1, d))


_NEG_INF = float("-inf")
_MARK = -float(2 ** 100)


def _extract_top(s, order, count, exact):
    vals = []
    if exact:
        rank = jnp.full(s.shape, float(count), F32)
        big = float(2 ** 20)
        for a in range(count):
            m = jnp.max(s, axis=0, keepdims=True)
            first = jnp.min(jnp.where(s == m, order, big), axis=0, keepdims=True)
            sel = order == first
            rank = jnp.where(sel, float(a), rank)
            s = jnp.where(sel, _NEG_INF, s)
            vals.append(m)
        return vals, rank, jnp.zeros_like(s[:1])
    lowest = jnp.min(s, axis=0, keepdims=True)
    for a in range(count):
        m = jnp.max(s, axis=0, keepdims=True)
        s = jnp.where(s == m, _MARK * (1.0 + a / 64.0), s)
        vals.append(m)
    marked = s <= 0.5 * _MARK
    rank = jnp.where(marked, s * (64.0 / _MARK) - 64.0, float(count))
    extracted = jnp.sum(jnp.where(marked, 1.0, 0.0), axis=0, keepdims=True)
    clean = (extracted == float(count)) & (lowest > 0.5 * _MARK)
    return vals, rank, jnp.where(clean, 0.0, 1.0)


def _candidate_rows(v1, v2, combine):
    v2_all = jnp.concatenate(v2, axis=0)
    v2_lo = v2_all[:8]
    pieces = [combine(v1[0], v2_all)]
    pieces += [combine(v1[a], v2_lo) for a in range(1, 8)]
    pieces.append(combine(jnp.concatenate(v1[8:], axis=0), v2[0]))
    return jnp.concatenate(pieces, axis=0)


def _candidate_positions(tokens):
    pos = ([b for b in range(16)] + [a * 16 + b for a in range(1, 8) for b in range(8)]
           + [a * 16 for a in range(8, 16)])
    col = jnp.asarray(np.asarray(pos, np.float32).reshape(-1, 1))
    return jnp.broadcast_to(col, (len(pos), tokens))


def _peer_route_kernel(x_ref, wq_ref, keys_ref, pos_ref, g1_ref, g2_ref):
    ambiguous = _route_tile(x_ref, wq_ref, keys_ref, pos_ref, g1_ref, g2_ref, exact=False)

    @pl.when(jnp.max(ambiguous) > 0.0)
    def _():
        _route_tile(x_ref, wq_ref, keys_ref, pos_ref, g1_ref, g2_ref, exact=True)


def _route_tile(x_ref, wq_ref, keys_ref, pos_ref, g1_ref, g2_ref, exact):
    tokens = x_ref.shape[0]
    xb = x_ref[...].astype(BF16)
    key_order = lax.broadcasted_iota(jnp.int32, (PEER_N_KEYS, tokens), 0).astype(F32)
    cand_pos = pos_ref[...]
    ambiguous = jnp.zeros((1, tokens), F32)
    for h in range(PEER_HEADS):
        col = 2 * h * PEER_HALF
        qp = _dot(xb, wq_ref[:, col:col + 2 * PEER_HALF]).astype(BF16)
        scores, vals, ranks = [], [], []
        for c in range(2):
            s = _dot_nt(keys_ref[h, c], qp[:, c * PEER_HALF:(c + 1) * PEER_HALF])
            v, r, amb = _extract_top(s, key_order, PEER_TOPK, exact)
            ambiguous = jnp.maximum(ambiguous, amb)
            scores.append(s)
            vals.append(v)
            ranks.append(r)
        v1, v2 = vals
        cand = _candidate_rows(v1, v2, lambda a, b: a + b)
        _, crank, amb = _extract_top(cand, cand_pos, PEER_TOPK, exact)
        ambiguous = jnp.maximum(ambiguous, amb)
        chosen = crank < float(PEER_TOPK)
        e1 = [jnp.exp(v - v1[0]) for v in v1]
        e2 = [jnp.exp(v - v2[0]) for v in v2]
        ecand = _candidate_rows(e1, e2, lambda a, b: a * b)
        z = jnp.sum(jnp.where(chosen, ecand, 0.0), axis=0, keepdims=True)
        cnt = jnp.where(chosen, 1.0, 0.0)
        limit = [jnp.sum(cnt[0:16], axis=0, keepdims=True)]
        limit += [jnp.sum(cnt[8 + 8 * a:16 + 8 * a], axis=0, keepdims=True) for a in range(1, 8)]
        limit += [cnt[72 + a:73 + a] for a in range(8)]
        lim_dense = jnp.zeros((PEER_N_KEYS, tokens), F32)
        for a in range(PEER_TOPK):
            lim_dense = jnp.where(ranks[0] == float(a), limit[a], lim_dense)
        g1_ref[0, h] = jnp.exp(scores[0] - v1[0]) / z
        g1_ref[1, h] = lim_dense
        g2_ref[0, h] = jnp.exp(scores[1] - v2[0]).astype(BF16)
        g2_ref[1, h] = ranks[1].astype(BF16)
    return ambiguous


def _peer_route_call(x, wq, keys, tm):
    n, d = x.shape
    pos = _candidate_positions(tm)
    return pl.pallas_call(
        _peer_route_kernel,
        grid=(n // tm,),
        in_specs=[pl.BlockSpec((tm, d), lambda i: (i, 0)),
                  pl.BlockSpec(wq.shape, lambda i: (0, 0)),
                  pl.BlockSpec(keys.shape, lambda i: (0, 0, 0, 0)),
                  pl.BlockSpec(pos.shape, lambda i: (0, 0))],
        out_specs=[pl.BlockSpec((2, PEER_HEADS, PEER_N_KEYS, tm), lambda i: (0, 0, 0, i)),
                   pl.BlockSpec((2, PEER_HEADS, PEER_N_KEYS, tm), lambda i: (0, 0, 0, i))],
        out_shape=[jax.ShapeDtypeStruct((2, PEER_HEADS, PEER_N_KEYS, n), F32),
                   jax.ShapeDtypeStruct((2, PEER_HEADS, PEER_N_KEYS, n), BF16)],
        compiler_params=_params("parallel"),
        name="peer_route",
    )(x, wq, keys, pos)


def _peer_expert_kernel(x_ref, g1_ref, g2_ref, u_ref, vt_ref, g_ref, b_ref, o_ref, xb_ref, acc_ref, p_ref, *,
                        alpha, rows_per_step):
    j = pl.program_id(1)

    @pl.when(j == 0)
    def _():
        xb_ref[...] = x_ref[...].astype(BF16)
        acc_ref[...] = jnp.zeros_like(acc_ref)

    act = _dot_nt(u_ref[...], xb_ref[...])
    act = 0.5 * act * (1.0 + lax.erf(act * math.sqrt(0.5)))
    for r in range(rows_per_step):
        i1 = j * rows_per_step + r
        gate = None
        for h in range(PEER_HEADS):
            e1n = g1_ref[0, h, pl.ds(i1, 1), :].astype(BF16)
            lim = g1_ref[1, h, pl.ds(i1, 1), :].astype(BF16)
            term = jnp.where(g2_ref[1, h] < lim, g2_ref[0, h] * e1n, jnp.zeros((), BF16))
            gate = term if gate is None else gate + term
        lo, hi = r * PEER_N_KEYS, (r + 1) * PEER_N_KEYS
        p_ref[lo:hi, :] = gate * act[lo:hi, :].astype(BF16)
    acc_ref[...] += _dot(vt_ref[...], p_ref[...])

    @pl.when(j == pl.num_programs(1) - 1)
    def _():
        z = alpha * x_ref[...] + acc_ref[...].T
        o_ref[...] = _layer_norm(z, g_ref[...], b_ref[...])


def _peer_expert_call(x, g1, g2, u, vt, g, b, tm, rows_per_step, alpha):
    n, d = x.shape
    te = rows_per_step * PEER_N_KEYS
    n_exp = u.shape[0]
    return pl.pallas_call(
        functools.partial(_peer_expert_kernel, alpha=alpha, rows_per_step=rows_per_step),
        grid=(n // tm, n_exp // te),
        in_specs=[pl.BlockSpec((tm, d), lambda i, j: (i, 0)),
                  pl.BlockSpec((2, PEER_HEADS, PEER_N_KEYS, tm), lambda i, j: (0, 0, 0, i)),
                  pl.BlockSpec((2, PEER_HEADS, PEER_N_KEYS, tm), lambda i, j: (0, 0, 0, i)),
                  pl.BlockSpec((te, d), lambda i, j: (j, 0)),
                  pl.BlockSpec((d, te), lambda i, j: (0, j)),
                  pl.BlockSpec((1, d), lambda i, j: (0, 0)),
                  pl.BlockSpec((1, d), lambda i, j: (0, 0))],
        out_specs=pl.BlockSpec((tm, d), lambda i, j: (i, 0)),
        out_shape=jax.ShapeDtypeStruct((n, d), F32),
        scratch_shapes=[pltpu.VMEM((tm, d), BF16), pltpu.VMEM((d, tm), F32), pltpu.VMEM((te, tm), BF16)],
        compiler_params=_params("parallel", "arbitrary"),
        name="peer_experts",
    )(x, g1, g2, u, vt, g.reshape(1, d), b.reshape(1, d))


def _rope_tables(seq):
    rows = seq // GRID_W
    row_id = jnp.repeat(jnp.arange(rows, dtype=F32), GRID_W)
    col_id = jnp.tile(jnp.arange(GRID_W, dtype=F32), rows)
    inv_freq = 1.0 / (ROPE_THETA ** (jnp.arange(0, ROPE_AXIS_DIM, 2, dtype=F32) / ROPE_AXIS_DIM))
    ang_r = row_id[:, None] * inv_freq[None, :]
    ang_c = col_id[:, None] * inv_freq[None, :]
    ang = jnp.concatenate([ang_r, ang_r, ang_c, ang_c], axis=-1)
    sign = jnp.where((jnp.arange(HEAD_DIM) % ROPE_AXIS_DIM) < ROPE_HALF, -1.0, 1.0).astype(F32)
    return jnp.cos(ang), jnp.sin(ang) * sign[None, :]


def _head_mean_matrix(heads):
    m = np.kron(np.eye(heads), np.full((HEAD_DIM, HEAD_DIM), 1.0 / HEAD_DIM))
    return jnp.asarray(m, BF16)


TM_LN = 512
TM_IN = 512
DFT_BLOCK = 8
ATTN_TQ = 128
TM_MIX = 256
TM_ROUTE = 256
TM_EXPERT = 512
EXPERT_ROWS_PER_STEP = 16


def kernel(x, ln0_g, ln0_b, w_in, b_gate, q_norm_g, k_norm_g, w_branch_fourier, w_branch_attn, w_out,
           ln1_g, ln1_b, peer_w_query, peer_sub_keys, peer_u, peer_v, ln2_g, ln2_b):
    batch, seq, d = x.shape
    n = batch * seq
    depth = w_in.shape[0]
    alpha = (2.0 * depth) ** 0.25
    n1 = seq // DFT_N2

    cos, sin_signed = _rope_tables(seq)
    cq, sq = jnp.tile(cos, (1, N_Q_HEADS)), jnp.tile(sin_signed, (1, N_Q_HEADS))
    ck, sk = jnp.tile(cos, (1, N_KV_HEADS)), jnp.tile(sin_signed, (1, N_KV_HEADS))
    hmq, hmk = _head_mean_matrix(N_Q_HEADS), _head_mean_matrix(N_KV_HEADS)
    cdft, m1, m3, tw_r, tw_i = _dft_constants(seq)

    xs = _ln_call(x.reshape(n, d), ln0_g, ln0_b, TM_LN)
    for l in range(depth):
        qg = jnp.tile(q_norm_g[l], N_Q_HEADS).reshape(1, D_ATTN)
        kg = jnp.tile(k_norm_g[l], N_KV_HEADS).reshape(1, D_KV)
        zr, zi, q, k, vt, gf, ga = _in_proj_call(
            xs, w_in[l].astype(BF16), b_gate[l], qg, kg, cq, sq, ck, sk, cdft, hmq, hmk, TM_IN, seq)
        a = _dft_stage1_call(zr.reshape(batch, DFT_N2, n1, D_FOURIER), zi.reshape(batch, DFT_N2, n1, D_FOURIER),
                             m1, tw_r, tw_i, DFT_BLOCK)
        yf = _dft_stage2_call(a, m3, DFT_BLOCK)
        ya = _attn_call(q, k, vt, batch, seq, ATTN_TQ)
        xs = _mix_call(yf.reshape(n, D_FOURIER), ya, gf, ga, xs,
                       w_branch_fourier[l].astype(BF16), w_branch_attn[l].astype(BF16), w_out[l].astype(BF16),
                       ln1_g[l], ln1_b[l], TM_MIX, alpha)
        g1, g2 = _peer_route_call(xs, peer_w_query[l].astype(BF16), peer_sub_keys[l].astype(BF16), TM_ROUTE)
        xs = _peer_expert_call(xs, g1, g2, peer_u[l].astype(BF16), peer_v[l].astype(BF16).T,
                               ln2_g[l], ln2_b[l], TM_EXPERT, EXPERT_ROWS_PER_STEP, alpha)
    return xs.reshape(batch, seq, d)
```

```python
import functools
import math

import numpy as np
import jax
import jax.numpy as jnp
from jax import lax
from jax.experimental import pallas as pl
from jax.experimental.pallas import tpu as pltpu

F32 = jnp.float32
BF16 = jnp.bfloat16

D_MODEL = 1024
GRID_W = 64
N_FOURIER_GROUPS = 4
FOURIER_GROUP_DIM = 128
D_FOURIER = N_FOURIER_GROUPS * FOURIER_GROUP_DIM
N_Q_HEADS = 8
N_KV_HEADS = 2
Q_PER_KV = N_Q_HEADS // N_KV_HEADS
HEAD_DIM = 64
D_ATTN = N_Q_HEADS * HEAD_DIM
D_KV = N_KV_HEADS * HEAD_DIM
ROPE_AXIS_DIM = HEAD_DIM // 2
ROPE_HALF = ROPE_AXIS_DIM // 2
ROPE_THETA = 10000.0
LOG2_E = math.log2(math.e)
V_ROWS = HEAD_DIM + 16
PEER_HEADS = 8
PEER_N_KEYS = 128
PEER_TOPK = 16
PEER_HALF = 128
LN_EPS = 1e-5
RMS_EPS = 1e-6

_C_XF = 0
_C_Q = D_FOURIER
_C_K = _C_Q + D_ATTN
_C_V = _C_K + D_KV
_C_GF = _C_V + D_KV
_C_GA = _C_GF + D_MODEL
_C_END = _C_GA + D_MODEL

DFT_N2 = 128

VMEM_LIMIT_BYTES = 56 * 1024 * 1024


def _params(*sem):
    return pltpu.CompilerParams(dimension_semantics=sem, vmem_limit_bytes=VMEM_LIMIT_BYTES)


def _dot(a, b):
    return jnp.dot(a, b, preferred_element_type=F32)


def _dot_nt(a, b):
    return lax.dot_general(a, b, (((1,), (1,)), ((), ())), preferred_element_type=F32)


def _layer_norm(z, g, b):
    mu = jnp.mean(z, axis=-1, keepdims=True)
    zc = z - mu
    var = jnp.mean(zc * zc, axis=-1, keepdims=True)
    return zc * lax.rsqrt(var + LN_EPS) * g + b


def _ln_kernel(x_ref, g_ref, b_ref, o_ref):
    o_ref[...] = _layer_norm(x_ref[...], g_ref[...], b_ref[...])


def _ln_call(x, g, b, tm):
    n, d = x.shape
    return pl.pallas_call(
        _ln_kernel,
        grid=(n // tm,),
        in_specs=[pl.BlockSpec((tm, d), lambda i: (i, 0)),
                  pl.BlockSpec((1, d), lambda i: (0, 0)),
                  pl.BlockSpec((1, d), lambda i: (0, 0))],
        out_specs=pl.BlockSpec((tm, d), lambda i: (i, 0)),
        out_shape=jax.ShapeDtypeStruct((n, d), F32),
        compiler_params=_params("parallel"),
        name="ln0",
    )(x, g.reshape(1, d), b.reshape(1, d))


def _rms_rope(t, gain, head_mean, cos, sin_signed):
    sq = t * t
    hi = sq.astype(BF16)
    lo = (sq - hi.astype(F32)).astype(BF16)
    ms = _dot(hi, head_mean) + _dot(lo, head_mean)
    tn = t * lax.rsqrt(ms + RMS_EPS) * gain
    width = t.shape[-1]
    lane = lax.broadcasted_iota(jnp.int32, tn.shape, 1)
    first_half = (lane % ROPE_AXIS_DIM) < ROPE_HALF
    rot = jnp.where(first_half, pltpu.roll(tn, width - ROPE_HALF, 1), pltpu.roll(tn, ROPE_HALF, 1))
    return tn * cos + rot * sin_signed


def _in_proj_kernel(x_ref, w_ref, bg_ref, qg_ref, kg_ref, cq_ref, sq_ref, ck_ref, sk_ref,
                    cdft_ref, hmq_ref, hmk_ref,
                    zr_ref, zi_ref, q_ref, k_ref, vt_ref, gf_ref, ga_ref):
    xb = x_ref[...].astype(BF16)

    def seg(a, b):
        return _dot(xb, w_ref[:, a:b])

    xf = seg(_C_XF, _C_Q).astype(BF16)
    for g in range(N_FOURIER_GROUPS):
        lo, hi = g * FOURIER_GROUP_DIM, (g + 1) * FOURIER_GROUP_DIM
        z = _dot(xf[:, lo:hi], cdft_ref[...])
        zr_ref[:, lo:hi] = z[:, :FOURIER_GROUP_DIM]
        zi_ref[:, lo:hi] = z[:, FOURIER_GROUP_DIM:]

    q = _rms_rope(seg(_C_Q, _C_K), qg_ref[...], hmq_ref[...], cq_ref[...], sq_ref[...])
    q_ref[...] = (q * (HEAD_DIM ** -0.5 * LOG2_E)).astype(BF16)
    k = _rms_rope(seg(_C_K, _C_V), kg_ref[...], hmk_ref[...], ck_ref[...], sk_ref[...]).astype(BF16)
    vt = seg(_C_V, _C_GF).T.astype(BF16)
    pad_row = lax.broadcasted_iota(jnp.int32, (V_ROWS - HEAD_DIM, vt.shape[1]), 0)
    pad = jnp.where(pad_row == 0, 1.0, 0.0).astype(BF16)
    for j in range(N_KV_HEADS):
        k_ref[j] = k[:, j * HEAD_DIM:(j + 1) * HEAD_DIM]
        vt_ref[j, 0] = jnp.concatenate([vt[j * HEAD_DIM:(j + 1) * HEAD_DIM, :], pad], axis=0)

    gf_ref[...] = jax.nn.sigmoid(seg(_C_GF, _C_GA) + bg_ref[0:1, :])
    ga_ref[...] = jax.nn.sigmoid(seg(_C_GA, _C_END) + bg_ref[1:2, :])


def _in_proj_call(x, w, bg, qg, kg, cq, sq, ck, sk, cdft, hmq, hmk, tm, seq):
    n, d = x.shape
    pos_blocks = seq // tm
    row = lambda i: (i, 0)
    pos = lambda i: (i % pos_blocks, 0)
    const = lambda i: (0, 0)
    full = lambda a: pl.BlockSpec(a.shape, const)
    return pl.pallas_call(
        _in_proj_kernel,
        grid=(n // tm,),
        in_specs=[pl.BlockSpec((tm, d), row), full(w), full(bg), full(qg), full(kg),
                  pl.BlockSpec((tm, D_ATTN), pos), pl.BlockSpec((tm, D_ATTN), pos),
                  pl.BlockSpec((tm, D_KV), pos), pl.BlockSpec((tm, D_KV), pos),
                  full(cdft), full(hmq), full(hmk)],
        out_specs=[pl.BlockSpec((tm, D_FOURIER), row), pl.BlockSpec((tm, D_FOURIER), row),
                   pl.BlockSpec((tm, D_ATTN), row),
                   pl.BlockSpec((N_KV_HEADS, tm, HEAD_DIM), lambda i: (0, i, 0)),
                   pl.BlockSpec((N_KV_HEADS, 1, V_ROWS, tm), lambda i: (0, i, 0, 0)),
                   pl.BlockSpec((tm, d), row), pl.BlockSpec((tm, d), row)],
        out_shape=[jax.ShapeDtypeStruct((n, D_FOURIER), F32), jax.ShapeDtypeStruct((n, D_FOURIER), F32),
                   jax.ShapeDtypeStruct((n, D_ATTN), BF16),
                   jax.ShapeDtypeStruct((N_KV_HEADS, n, HEAD_DIM), BF16),
                   jax.ShapeDtypeStruct((N_KV_HEADS, n // tm, V_ROWS, tm), BF16),
                   jax.ShapeDtypeStruct((n, d), F32), jax.ShapeDtypeStruct((n, d), F32)],
        compiler_params=_params("parallel"),
        name="in_proj",
    )(x, w, bg, qg, kg, cq, sq, ck, sk, cdft, hmq, hmk)


def _dft_stage1_kernel(zr_ref, zi_ref, m1_ref, tr_ref, ti_ref, o_ref):
    n2, jb, c = zr_ref.shape[1:]
    for j in range(jb):
        zz = jnp.concatenate([zr_ref[0, :, j, :], zi_ref[0, :, j, :]], axis=0).astype(BF16)
        b = _dot(m1_ref[...], zz)
        br, bi = b[:n2], b[n2:]
        reps = c // tr_ref.shape[-1]
        tr, ti = jnp.tile(tr_ref[j], (1, reps)), jnp.tile(ti_ref[j], (1, reps))
        o_ref[0, 0, :, j, :] = br * tr - bi * ti
        o_ref[0, 1, :, j, :] = br * ti + bi * tr


def _dft_stage1_call(z_r, z_i, m1, tr, ti, jb):
    b, n2, n1, c = z_r.shape
    blk = pl.BlockSpec((1, n2, jb, c), lambda i, j: (i, 0, j, 0))
    tw = pl.BlockSpec((jb,) + tr.shape[1:], lambda i, j: (j, 0, 0))
    return pl.pallas_call(
        _dft_stage1_kernel,
        grid=(b, n1 // jb),
        in_specs=[blk, blk, pl.BlockSpec(m1.shape, lambda i, j: (0, 0)), tw, tw],
        out_specs=pl.BlockSpec((1, 2, n2, jb, c), lambda i, j: (i, 0, 0, j, 0)),
        out_shape=jax.ShapeDtypeStruct((b, 2, n2, n1, c), F32),
        compiler_params=_params("parallel", "parallel"),
        name="dft_stage1",
    )(z_r, z_i, m1, tr, ti)


def _dft_stage2_kernel(a_ref, m3_ref, o_ref):
    kb = a_ref.shape[2]
    for j in range(kb):
        rhs = jnp.concatenate([a_ref[0, 0, j], a_ref[0, 1, j]], axis=0).astype(BF16)
        o_ref[0, :, j, :] = _dot(m3_ref[...], rhs)


def _dft_stage2_call(a, m3, kb):
    b, _, n2, n1, c = a.shape
    return pl.pallas_call(
        _dft_stage2_kernel,
        grid=(b, n2 // kb),
        in_specs=[pl.BlockSpec((1, 2, kb, n1, c), lambda i, j: (i, 0, j, 0, 0)),
                  pl.BlockSpec(m3.shape, lambda i, j: (0, 0))],
        out_specs=pl.BlockSpec((1, n1, kb, c), lambda i, j: (i, 0, j, 0)),
        out_shape=jax.ShapeDtypeStruct((b, n1, n2, c), F32),
        compiler_params=_params("parallel", "parallel"),
        name="dft_stage2",
    )(a, m3)


def _dft_constants(seq):
    c = FOURIER_GROUP_DIM
    n1, n2 = seq // DFT_N2, DFT_N2
    jc = np.arange(c)
    ang = 2.0 * np.pi * np.outer(jc, jc) / c
    cdft = np.concatenate([np.cos(ang), -np.sin(ang)], axis=1) / math.sqrt(c)
    j2 = np.arange(n2)
    a2 = 2.0 * np.pi * np.outer(j2, j2) / n2
    fr, fi = np.cos(a2) / math.sqrt(n2), -np.sin(a2) / math.sqrt(n2)
    m1 = np.block([[fr, -fi], [fi, fr]])
    j1 = np.arange(n1)
    a1 = 2.0 * np.pi * np.outer(j1, j1) / n1
    m3 = np.concatenate([np.cos(a1), np.sin(a1)], axis=1) / math.sqrt(n1)
    at = 2.0 * np.pi * np.outer(j1, j2) / seq
    lanes = np.ones((1, 1, 128))
    return (jnp.asarray(cdft, BF16), jnp.asarray(m1, BF16), jnp.asarray(m3, BF16),
            jnp.asarray(np.cos(at)[:, :, None] * lanes, F32), jnp.asarray(-np.sin(at)[:, :, None] * lanes, F32))


def _attn_kernel(q_ref, k_ref, vt_ref, o_ref, qs_ref, st_ref, p_ref, acc_ref, *, tq):
    n_chunks, _, tk = vt_ref.shape[1:]
    slots = st_ref.shape[0]
    assert slots == 3 and n_chunks >= 4 and (n_chunks - 4) % slots == 0
    nq = Q_PER_KV * tq
    heads = range(N_KV_HEADS)
    for j in heads:
        for g in range(Q_PER_KV):
            h = j * Q_PER_KV + g
            qs_ref[j, g * tq:(g + 1) * tq, :] = q_ref[:, h * HEAD_DIM:(h + 1) * HEAD_DIM]
    acc_ref[...] = jnp.zeros_like(acc_ref)

    def scores(j, c, slot):
        start = pl.multiple_of(c * tk, tk)
        st_ref[slot, j] = _dot_nt(k_ref[j, pl.ds(start, tk), :], qs_ref[j])

    def softmax(j, slot, m):
        st = st_ref[slot, j]
        m_new = jnp.maximum(m, jnp.max(st, axis=0, keepdims=True))
        p_ref[slot, j] = jnp.exp2(st - m_new).astype(BF16)
        return m_new, jnp.exp2(m - m_new)

    def values(j, c, slot, alpha):
        acc_ref[j] = alpha * acc_ref[j] + _dot(vt_ref[j, c], p_ref[slot, j])

    def step(i, slot, carry, do_values=True, do_scores=True):
        if do_values:
            for j in heads:
                values(j, i - 2, (slot + 1) % slots, carry[j][2])
        new = []
        for j in heads:
            m, alpha = softmax(j, slot, carry[j][0])
            new.append((m, alpha, carry[j][1]))
        if do_scores:
            for j in heads:
                scores(j, i + 2, (slot + 2) % slots)
        return tuple(new)

    for j in heads:
        scores(j, 0, 0)
        scores(j, 1, 1)
    zero = jnp.zeros((1, nq), F32)
    carry = tuple((jnp.full((1, nq), -jnp.inf, F32), zero, zero) for _ in heads)
    carry = step(0, 0, carry, do_values=False)
    carry = step(1, 1, carry, do_values=False)

    def body(g, carry):
        i = 2 + slots * g
        for k in range(slots):
            carry = step(i + k, (2 + k) % slots, carry)
        return carry

    carry = lax.fori_loop(0, (n_chunks - 4) // slots, body, carry)
    for i in (n_chunks - 2, n_chunks - 1):
        carry = step(i, i % slots, carry, do_scores=False)
    pieces = []
    for j in heads:
        values(j, n_chunks - 2, (n_chunks - 2) % slots, carry[j][2])
        values(j, n_chunks - 1, (n_chunks - 1) % slots, carry[j][1])
        ot = acc_ref[j, :HEAD_DIM, :] / acc_ref[j, HEAD_DIM:HEAD_DIM + 1, :]
        o = jnp.concatenate([ot, jnp.zeros_like(ot)], axis=0).T
        pieces += [o[g * tq:(g + 1) * tq, :HEAD_DIM] for g in range(Q_PER_KV)]
    o_ref[...] = jnp.concatenate(pieces, axis=1)


def _attn_call(q, k, vt, batch, seq, tq):
    n = q.shape[0]
    qblocks = seq // tq
    tk = vt.shape[-1]
    chunks = seq // tk
    nq = Q_PER_KV * tq
    return pl.pallas_call(
        functools.partial(_attn_kernel, tq=tq),
        grid=(batch, qblocks),
        in_specs=[pl.BlockSpec((tq, D_ATTN), lambda b, i: (b * qblocks + i, 0)),
                  pl.BlockSpec((N_KV_HEADS, seq, HEAD_DIM), lambda b, i: (0, b, 0)),
                  pl.BlockSpec((N_KV_HEADS, chunks, V_ROWS, tk), lambda b, i: (0, b, 0, 0))],
        out_specs=pl.BlockSpec((tq, D_ATTN), lambda b, i: (b * qblocks + i, 0)),
        out_shape=jax.ShapeDtypeStruct((n, D_ATTN), F32),
        scratch_shapes=[pltpu.VMEM((N_KV_HEADS, nq, HEAD_DIM), BF16),
                        pltpu.VMEM((3, N_KV_HEADS, tk, nq), F32),
                        pltpu.VMEM((3, N_KV_HEADS, tk, nq), BF16),
                        pltpu.VMEM((N_KV_HEADS, V_ROWS, nq), F32)],
        compiler_params=_params("parallel", "parallel"),
        name="attention",
    )(q, k, vt)


def _mix_kernel(yf_ref, ya_ref, gf_ref, ga_ref, x_ref, wf_ref, wa_ref, wo_ref, g_ref, b_ref, o_ref, *, alpha):
    mixed = (gf_ref[...] * _dot(yf_ref[...].astype(BF16), wf_ref[...])
             + ga_ref[...] * _dot(ya_ref[...].astype(BF16), wa_ref[...]))
    z = alpha * x_ref[...] + _dot(mixed.astype(BF16), wo_ref[...])
    o_ref[...] = _layer_norm(z, g_ref[...], b_ref[...])


def _mix_call(yf, ya, gf, ga, x, wf, wa, wo, g, b, tm, alpha):
    n, d = x.shape
    row = lambda i: (i, 0)
    const = lambda i: (0, 0)
    return pl.pallas_call(
        functools.partial(_mix_kernel, alpha=alpha),
        grid=(n // tm,),
        in_specs=[pl.BlockSpec((tm, D_FOURIER), row), pl.BlockSpec((tm, D_ATTN), row),
                  pl.BlockSpec((tm, d), row), pl.BlockSpec((tm, d), row), pl.BlockSpec((tm, d), row),
                  pl.BlockSpec(wf.shape, const), pl.BlockSpec(wa.shape, const), pl.BlockSpec(wo.shape, const),
                  pl.BlockSpec((1, d), const), pl.BlockSpec((1, d), const)],
        out_specs=pl.BlockSpec((tm, d), row),
        out_shape=jax.ShapeDtypeStruct((n, d), F32),
        compiler_params=_params("parallel"),
        name="mix",
    )(yf, ya, gf, ga, x, wf, wa, wo, g.reshape(1, d), b.reshape(1, d))


_NEG_INF = float("-inf")
_MARK = -float(2 ** 100)


def _extract_top(s, order, count, exact):
    vals = []
    if exact:
        rank = jnp.full(s.shape, float(count), F32)
        big = float(2 ** 20)
        for a in range(count):
            m = jnp.max(s, axis=0, keepdims=True)
            first = jnp.min(jnp.where(s == m, order, big), axis=0, keepdims=True)
            sel = order == first
            rank = jnp.where(sel, float(a), rank)
            s = jnp.where(sel, _NEG_INF, s)
            vals.append(m)
        return vals, rank, jnp.zeros_like(s[:1])
    for a in range(count):
        m = jnp.max(s, axis=0, keepdims=True)
        s = jnp.where(s == m, _MARK * (1.0 + a / 64.0), s)
        vals.append(m)
    marked = s <= 0.5 * _MARK
    rank = jnp.where(marked, s * (64.0 / _MARK) - 64.0, float(count))
    extracted = jnp.sum(jnp.where(marked, 1.0, 0.0), axis=0, keepdims=True)
    return vals, rank, jnp.where(extracted == float(count), 0.0, 1.0)


def _candidate_rows(v1, v2, combine):
    v2_all = jnp.concatenate(v2, axis=0)
    v2_lo = v2_all[:8]
    pieces = [combine(v1[0], v2_all)]
    pieces += [combine(v1[a], v2_lo) for a in range(1, 8)]
    pieces.append(combine(jnp.concatenate(v1[8:], axis=0), v2[0]))
    return jnp.concatenate(pieces, axis=0)


def _candidate_positions(tokens):
    pos = ([b for b in range(16)] + [a * 16 + b for a in range(1, 8) for b in range(8)]
           + [a * 16 for a in range(8, 16)])
    col = jnp.asarray(np.asarray(pos, np.float32).reshape(-1, 1))
    return jnp.broadcast_to(col, (len(pos), tokens))


def _peer_route_kernel(x_ref, wq_ref, keys_ref, pos_ref, g1_ref, g2_ref):
    ambiguous = _route_tile(x_ref, wq_ref, keys_ref, pos_ref, g1_ref, g2_ref, exact=False)

    @pl.when(jnp.max(ambiguous) > 0.0)
    def _():
        _route_tile(x_ref, wq_ref, keys_ref, pos_ref, g1_ref, g2_ref, exact=True)


def _route_tile(x_ref, wq_ref, keys_ref, pos_ref, g1_ref, g2_ref, exact):
    tokens = x_ref.shape[0]
    xb = x_ref[...].astype(BF16)
    key_order = lax.broadcasted_iota(jnp.int32, (PEER_N_KEYS, tokens), 0).astype(F32)
    cand_pos = pos_ref[...]
    ambiguous = jnp.zeros((1, tokens), F32)
    for h in range(PEER_HEADS):
        col = 2 * h * PEER_HALF
        qp = _dot(xb, wq_ref[:, col:col + 2 * PEER_HALF]).astype(BF16)
        scores, vals, ranks = [], [], []
        for c in range(2):
            s = _dot_nt(keys_ref[h, c], qp[:, c * PEER_HALF:(c + 1) * PEER_HALF])
            v, r, amb = _extract_top(s, key_order, PEER_TOPK, exact)
            ambiguous = jnp.maximum(ambiguous, amb)
            scores.append(s)
            vals.append(v)
            ranks.append(r)
        v1, v2 = vals
        cand = _candidate_rows(v1, v2, lambda a, b: a + b)
        _, crank, amb = _extract_top(cand, cand_pos, PEER_TOPK, exact)
        ambiguous = jnp.maximum(ambiguous, amb)
        chosen = crank < float(PEER_TOPK)
        e1 = [jnp.exp(v - v1[0]) for v in v1]
        e2 = [jnp.exp(v - v2[0]) for v in v2]
        ecand = _candidate_rows(e1, e2, lambda a, b: a * b)
        z = jnp.sum(jnp.where(chosen, ecand, 0.0), axis=0, keepdims=True)
        cnt = jnp.where(chosen, 1.0, 0.0)
        limit = [jnp.sum(cnt[0:16], axis=0, keepdims=True)]
        limit += [jnp.sum(cnt[8 + 8 * a:16 + 8 * a], axis=0, keepdims=True) for a in range(1, 8)]
        cut = 8.0 + jnp.sum(cnt[72:80], axis=0, keepdims=True)
        lim_dense = jnp.where(ranks[0] < cut, 1.0, 0.0)
        for a in range(8):
            lim_dense = jnp.where(ranks[0] == float(a), limit[a], lim_dense)
        g1_ref[0, h] = jnp.exp(scores[0] - v1[0]) / z
        g1_ref[1, h] = lim_dense
        g2_ref[0, h] = jnp.exp(scores[1] - v2[0]).astype(BF16)
        g2_ref[1, h] = ranks[1].astype(BF16)
    return ambiguous


def _peer_route_call(x, wq, keys, tm):
    n, d = x.shape
    pos = _candidate_positions(tm)
    return pl.pallas_call(
        _peer_route_kernel,
        grid=(n // tm,),
        in_specs=[pl.BlockSpec((tm, d), lambda i: (i, 0)),
                  pl.BlockSpec(wq.shape, lambda i: (0, 0)),
                  pl.BlockSpec(keys.shape, lambda i: (0, 0, 0, 0)),
                  pl.BlockSpec(pos.shape, lambda i: (0, 0))],
        out_specs=[pl.BlockSpec((2, PEER_HEADS, PEER_N_KEYS, tm), lambda i: (0, 0, 0, i)),
                   pl.BlockSpec((2, PEER_HEADS, PEER_N_KEYS, tm), lambda i: (0, 0, 0, i))],
        out_shape=[jax.ShapeDtypeStruct((2, PEER_HEADS, PEER_N_KEYS, n), F32),
                   jax.ShapeDtypeStruct((2, PEER_HEADS, PEER_N_KEYS, n), BF16)],
        compiler_params=_params("parallel"),
        name="peer_route",
    )(x, wq, keys, pos)


def _peer_expert_kernel(x_ref, g1_ref, g2_ref, u_ref, vt_ref, g_ref, b_ref, o_ref, xb_ref, acc_ref, p_ref, *,
                        alpha, rows_per_step):
    j = pl.program_id(1)

    @pl.when(j == 0)
    def _():
        xb_ref[...] = x_ref[...].astype(BF16)
        acc_ref[...] = jnp.zeros_like(acc_ref)

    act = _dot_nt(u_ref[...], xb_ref[...])
    act = 0.5 * act * (1.0 + lax.erf(act * math.sqrt(0.5)))
    for r in range(rows_per_step):
        i1 = j * rows_per_step + r
        gate = None
        for h in range(PEER_HEADS):
            e1n = g1_ref[0, h, pl.ds(i1, 1), :].astype(BF16)
            lim = g1_ref[1, h, pl.ds(i1, 1), :].astype(BF16)
            term = jnp.where(g2_ref[1, h] < lim, g2_ref[0, h] * e1n, jnp.zeros((), BF16))
            gate = term if gate is None else gate + term
        lo, hi = r * PEER_N_KEYS, (r + 1) * PEER_N_KEYS
        p_ref[lo:hi, :] = gate * act[lo:hi, :].astype(BF16)
    acc_ref[...] += _dot(vt_ref[...], p_ref[...])

    @pl.when(j == pl.num_programs(1) - 1)
    def _():
        z = alpha * x_ref[...] + acc_ref[...].T
        o_ref[...] = _layer_norm(z, g_ref[...], b_ref[...])


def _peer_expert_call(x, g1, g2, u, vt, g, b, tm, rows_per_step, alpha):
    n, d = x.shape
    te = rows_per_step * PEER_N_KEYS
    n_exp = u.shape[0]
    return pl.pallas_call(
        functools.partial(_peer_expert_kernel, alpha=alpha, rows_per_step=rows_per_step),
        grid=(n // tm, n_exp // te),
        in_specs=[pl.BlockSpec((tm, d), lambda i, j: (i, 0)),
                  pl.BlockSpec((2, PEER_HEADS, PEER_N_KEYS, tm), lambda i, j: (0, 0, 0, i)),
                  pl.BlockSpec((2, PEER_HEADS, PEER_N_KEYS, tm), lambda i, j: (0, 0, 0, i)),
                  pl.BlockSpec((te, d), lambda i, j: (j, 0)),
                  pl.BlockSpec((d, te), lambda i, j: (0, j)),
                  pl.BlockSpec((1, d), lambda i, j: (0, 0)),
                  pl.BlockSpec((1, d), lambda i, j: (0, 0))],
        out_specs=pl.BlockSpec((tm, d), lambda i, j: (i, 0)),
        out_shape=jax.ShapeDtypeStruct((n, d), F32),
        scratch_shapes=[pltpu.VMEM((tm, d), BF16), pltpu.VMEM((d, tm), F32), pltpu.VMEM((te, tm), BF16)],
        compiler_params=_params("parallel", "arbitrary"),
        name="peer_experts",
    )(x, g1, g2, u, vt, g.reshape(1, d), b.reshape(1, d))


def _rope_tables(seq):
    rows = seq // GRID_W
    row_id = jnp.repeat(jnp.arange(rows, dtype=F32), GRID_W)
    col_id = jnp.tile(jnp.arange(GRID_W, dtype=F32), rows)
    inv_freq = 1.0 / (ROPE_THETA ** (jnp.arange(0, ROPE_AXIS_DIM, 2, dtype=F32) / ROPE_AXIS_DIM))
    ang_r = row_id[:, None] * inv_freq[None, :]
    ang_c = col_id[:, None] * inv_freq[None, :]
    ang = jnp.concatenate([ang_r, ang_r, ang_c, ang_c], axis=-1)
    sign = jnp.where((jnp.arange(HEAD_DIM) % ROPE_AXIS_DIM) < ROPE_HALF, -1.0, 1.0).astype(F32)
    return jnp.cos(ang), jnp.sin(ang) * sign[None, :]


def _head_mean_matrix(heads):
    m = np.kron(np.eye(heads), np.full((HEAD_DIM, HEAD_DIM), 1.0 / HEAD_DIM))
    return jnp.asarray(m, BF16)


TM_LN = 512
TM_IN = 512
DFT_BLOCK = 8
ATTN_TQ = 128
TM_MIX = 256
TM_ROUTE = 256
TM_EXPERT = 512
EXPERT_ROWS_PER_STEP = 16


def kernel(x, ln0_g, ln0_b, w_in, b_gate, q_norm_g, k_norm_g, w_branch_fourier, w_branch_attn, w_out,
           ln1_g, ln1_b, peer_w_query, peer_sub_keys, peer_u, peer_v, ln2_g, ln2_b):
    batch, seq, d = x.shape
    n = batch * seq
    depth = w_in.shape[0]
    alpha = (2.0 * depth) ** 0.25
    n1 = seq // DFT_N2

    cos, sin_signed = _rope_tables(seq)
    cq, sq = jnp.tile(cos, (1, N_Q_HEADS)), jnp.tile(sin_signed, (1, N_Q_HEADS))
    ck, sk = jnp.tile(cos, (1, N_KV_HEADS)), jnp.tile(sin_signed, (1, N_KV_HEADS))
    hmq, hmk = _head_mean_matrix(N_Q_HEADS), _head_mean_matrix(N_KV_HEADS)
    cdft, m1, m3, tw_r, tw_i = _dft_constants(seq)

    xs = _ln_call(x.reshape(n, d), ln0_g, ln0_b, TM_LN)
    for l in range(depth):
        qg = jnp.tile(q_norm_g[l], N_Q_HEADS).reshape(1, D_ATTN)
        kg = jnp.tile(k_norm_g[l], N_KV_HEADS).reshape(1, D_KV)
        zr, zi, q, k, vt, gf, ga = _in_proj_call(
            xs, w_in[l].astype(BF16), b_gate[l], qg, kg, cq, sq, ck, sk, cdft, hmq, hmk, TM_IN, seq)
        a = _dft_stage1_call(zr.reshape(batch, DFT_N2, n1, D_FOURIER), zi.reshape(batch, DFT_N2, n1, D_FOURIER),
                             m1, tw_r, tw_i, DFT_BLOCK)
        yf = _dft_stage2_call(a, m3, DFT_BLOCK)
        ya = _attn_call(q, k, vt, batch, seq, ATTN_TQ)
        xs = _mix_call(yf.reshape(n, D_FOURIER), ya, gf, ga, xs,
                       w_branch_fourier[l].astype(BF16), w_branch_attn[l].astype(BF16), w_out[l].astype(BF16),
                       ln1_g[l], ln1_b[l], TM_MIX, alpha)
        g1, g2 = _peer_route_call(xs, peer_w_query[l].astype(BF16), peer_sub_keys[l].astype(BF16), TM_ROUTE)
        xs = _peer_expert_call(xs, g1, g2, peer_u[l].astype(BF16), peer_v[l].astype(BF16).T,
                               ln2_g[l], ln2_b[l], TM_EXPERT, EXPERT_ROWS_PER_STEP, alpha)
    return xs.reshape(batch, seq, d)
```
